```python
import jax, jax.numpy as jnp
from jax import lax
import numpy as np

D_MODEL = 1024
BATCH = 8
SEQ = 4096
DEPTH = 2

CHUNK = 64
MIX_WIDTH = D_MODEL
GM_WIDTH = MIX_WIDTH // 2
GM_BLOCK = 128
GM_GROUPS = 4
GM_CH = GM_WIDTH // GM_GROUPS
MLA_HEADS = 8
QK_NOPE = 64
QK_ROPE = 32
QK_DIM = QK_NOPE + QK_ROPE
V_DIM = 64
MLA_WIDTH = MLA_HEADS * V_DIM
Q_LORA = 384
KV_LORA = 256
ROPE_THETA = 10000.0
Q_BLOCK = 128
IN_COLS = 2 * GM_WIDTH + Q_LORA + KV_LORA + QK_ROPE
N_EXPERTS = 64
TOP_K = 8
N_GROUPS = 8
TOPK_GROUPS = 4
EXPERT_FF = 256
SHARED_FF = 256
ROUTED_SCALE = 2.5
EXPERT_BLOCK = 8
EPS = 1e-6

kernel_name = "hybrid_gmlp_mla_moe_adaln_trunk"


def rms_norm(x, g):
    xf = x.astype(jnp.float32)
    y = xf * lax.rsqrt(jnp.mean(xf * xf, axis=-1, keepdims=True) + EPS)
    return (y * g.astype(jnp.float32)).astype(x.dtype)


def layer_norm(x, g, b):
    xf = x.astype(jnp.float32)
    mu = jnp.mean(xf, axis=-1, keepdims=True)
    xc = xf - mu
    y = xc * lax.rsqrt(jnp.mean(xc * xc, axis=-1, keepdims=True) + EPS)
    return (y * g.astype(jnp.float32) + b.astype(jnp.float32)).astype(x.dtype)


def rope_tables(positions):
    inv_freq = 1.0 / (ROPE_THETA ** (jnp.arange(0, QK_ROPE, 2, dtype=jnp.float32) / QK_ROPE))
    ang = positions.astype(jnp.float32)[..., None] * inv_freq
    return jnp.cos(ang)[:, :, None, :], jnp.sin(ang)[:, :, None, :]


def apply_rope(x, cos, sin):
    xf = x.astype(jnp.float32)
    x1, x2 = jnp.split(xf, 2, axis=-1)
    return jnp.concatenate([x1 * cos - x2 * sin, x1 * sin + x2 * cos], axis=-1).astype(x.dtype)


def gmlp_mixer(uv, ln_g, ln_b, w_s, b_s):
    B, S, _ = uv.shape
    uv = jax.nn.gelu(uv, approximate=False)
    u, v = jnp.split(uv, 2, axis=-1)
    v = v.reshape(B, S // GM_BLOCK, GM_BLOCK, GM_GROUPS, GM_CH)
    v = layer_norm(v, ln_g, ln_b)
    mask = jnp.tril(jnp.ones((GM_BLOCK, GM_BLOCK), dtype=w_s.dtype))
    ws = w_s * mask
    gate = jnp.einsum('gts,bnsgc->bntgc', ws, v) + b_s.T[:, :, None]
    return u * gate.reshape(B, S, GM_WIDTH)


def chunk_causal_attention(q, k, v):
    B, S, H, Dq = q.shape
    nb = S // Q_BLOCK
    scale = QK_DIM ** -0.5
    k_chunk = jnp.arange(S) // CHUNK
    qb = jnp.moveaxis(q.reshape(B, nb, Q_BLOCK, H, Dq), 1, 0)

    def block(args):
        q_blk, i = args
        q_chunk = (i * Q_BLOCK + jnp.arange(Q_BLOCK)) // CHUNK
        s = jnp.einsum('bqhd,bkhd->bhqk', q_blk, k).astype(jnp.float32) * scale
        allowed = k_chunk[None, :] <= q_chunk[:, None]
        s = jnp.where(allowed, s, -jnp.inf)
        p = jax.nn.softmax(s, axis=-1).astype(v.dtype)
        return jnp.einsum('bhqk,bkhd->bqhd', p, v)

    out = lax.map(block, (qb, jnp.arange(nb)))
    return jnp.moveaxis(out, 0, 1).reshape(B, S, H, v.shape[-1])


def mla_mixer(q_lat, kv_lat, k_pe, cos, sin, q_a_g, w_q_up, kv_a_g, w_kv_up, q_norm_g, k_norm_g):
    B, S, _ = q_lat.shape
    q = (rms_norm(q_lat, q_a_g) @ w_q_up).reshape(B, S, MLA_HEADS, QK_DIM)
    kv = (rms_norm(kv_lat, kv_a_g) @ w_kv_up).reshape(B, S, MLA_HEADS, QK_NOPE + V_DIM)
    k_nope, v = kv[..., :QK_NOPE], kv[..., QK_NOPE:]
    k_pe_h = jnp.broadcast_to(k_pe[:, :, None, :], (B, S, MLA_HEADS, QK_ROPE))
    k = jnp.concatenate([k_nope, k_pe_h], axis=-1)
    q = rms_norm(q, q_norm_g)
    k = rms_norm(k, k_norm_g)
    q = jnp.concatenate([q[..., :QK_NOPE], apply_rope(q[..., QK_NOPE:], cos, sin)], axis=-1)
    k = jnp.concatenate([k[..., :QK_NOPE], apply_rope(k[..., QK_NOPE:], cos, sin)], axis=-1)
    out = chunk_causal_attention(q, k, v)
    return out.reshape(B, S, MLA_WIDTH)


def moe_ffn(h, w_router, router_bias, w_gate, w_up, w_down, ws_gate, ws_up, ws_down):
    B, S, D = h.shape
    t = h.reshape(-1, D)
    n_tok = t.shape[0]
    scores = jax.nn.sigmoid((t @ w_router).astype(jnp.float32))
    biased = scores + router_bias.astype(jnp.float32)
    grp = biased.reshape(n_tok, N_GROUPS, N_EXPERTS // N_GROUPS)
    grp_score = jnp.sum(lax.top_k(grp, 2)[0], axis=-1)
    _, gidx = lax.top_k(grp_score, TOPK_GROUPS)
    gmask = jnp.sum(jax.nn.one_hot(gidx, N_GROUPS, dtype=jnp.float32), axis=-2)
    emask = jnp.repeat(gmask, N_EXPERTS // N_GROUPS, axis=-1) > 0
    _, eidx = lax.top_k(jnp.where(emask, biased, -jnp.inf), TOP_K)
    w = jnp.take_along_axis(scores, eidx, axis=-1)
    w = w / jnp.sum(w, axis=-1, keepdims=True) * ROUTED_SCALE
    combine = jnp.sum(jax.nn.one_hot(eidx, N_EXPERTS, dtype=jnp.float32) * w[..., None], axis=1)
    combine = combine.astype(t.dtype)
    routed = jnp.zeros_like(t)
    for e0 in range(0, N_EXPERTS, EXPERT_BLOCK):
        sl = slice(e0, e0 + EXPERT_BLOCK)
        hg = jnp.einsum('nd,edf->nef', t, w_gate[sl])
        hu = jnp.einsum('nd,edf->nef', t, w_up[sl])
        hw = jax.nn.silu(hg) * hu * combine[:, sl, None]
        routed = routed + jnp.einsum('nef,efd->nd', hw, w_down[sl])
    shared = (jax.nn.silu(t @ ws_gate) * (t @ ws_up)) @ ws_down
    return (routed + shared).reshape(B, S, D)


def setup_inputs(seed: int = 0) -> dict:
    key = jax.random.key(seed)
    ks = jax.random.split(key, 32)
    L, D = DEPTH, D_MODEL

    def nrm(k, shape, scale):
        return jax.random.normal(k, shape, dtype=jnp.float32) * scale

    def gain(k, shape):
        return 1.0 + 0.02 * jax.random.normal(k, shape, dtype=jnp.float32)

    offset = jax.random.randint(ks[2], (BATCH, 1), 0, 256, dtype=jnp.int32) * CHUNK
    positions = offset + jnp.arange(SEQ, dtype=jnp.int32)[None, :]
    return {
        "x": nrm(ks[0], (BATCH, SEQ, D), 1.0),
        "c": nrm(ks[1], (BATCH, D), 1.0),
        "positions": positions,
        "norm_mix_g": gain(ks[3], (L, D)),
        "norm_ffn_g": gain(ks[4], (L, D)),
        "w_ada": nrm(ks[5], (L, D, 6 * D), 0.5 * D ** -0.5),
        "b_ada": nrm(ks[6], (L, 6 * D), 0.02),
        "w_in": nrm(ks[7], (L, D, IN_COLS), D ** -0.5),
        "gm_ln_g": gain(ks[8], (L, GM_GROUPS, GM_CH)),
        "gm_ln_b": nrm(ks[9], (L, GM_GROUPS, GM_CH), 0.02),
        "gm_ws": nrm(ks[10], (L, GM_GROUPS, GM_BLOCK, GM_BLOCK), GM_BLOCK ** -0.5),
        "gm_bs": gain(ks[11], (L, GM_GROUPS, GM_BLOCK)),
        "q_a_g": gain(ks[12], (L, Q_LORA)),
        "w_q_up": nrm(ks[13], (L, Q_LORA, MLA_HEADS * QK_DIM), Q_LORA ** -0.5),
        "kv_a_g": gain(ks[14], (L, KV_LORA)),
        "w_kv_up": nrm(ks[15], (L, KV_LORA, MLA_HEADS * (QK_NOPE + V_DIM)), KV_LORA ** -0.5),
        "q_norm_g": gain(ks[16], (L, QK_DIM)),
        "k_norm_g": gain(ks[17], (L, QK_DIM)),
        "out_norm_g": gain(ks[18], (L, MIX_WIDTH)),
        "w_out": nrm(ks[19], (L, MIX_WIDTH, D), MIX_WIDTH ** -0.5),
        "w_router": nrm(ks[20], (L, D, N_EXPERTS), D ** -0.5),
        "router_bias": nrm(ks[21], (L, N_EXPERTS), 0.01),
        "w_exp_gate": nrm(ks[22], (L, N_EXPERTS, D, EXPERT_FF), D ** -0.5),
        "w_exp_up": nrm(ks[23], (L, N_EXPERTS, D, EXPERT_FF), D ** -0.5),
        "w_exp_down": nrm(ks[24], (L, N_EXPERTS, EXPERT_FF, D), EXPERT_FF ** -0.5),
        "w_sh_gate": nrm(ks[25], (L, D, SHARED_FF), D ** -0.5),
        "w_sh_up": nrm(ks[26], (L, D, SHARED_FF), D ** -0.5),
        "w_sh_down": nrm(ks[27], (L, SHARED_FF, D), SHARED_FF ** -0.5),
    }


def reference(x, c, positions, norm_mix_g, norm_ffn_g, w_ada, b_ada, w_in,
              gm_ln_g, gm_ln_b, gm_ws, gm_bs, q_a_g, w_q_up, kv_a_g, w_kv_up,
              q_norm_g, k_norm_g, out_norm_g, w_out, w_router, router_bias,
              w_exp_gate, w_exp_up, w_exp_down, w_sh_gate, w_sh_up, w_sh_down):
    cos, sin = rope_tables(positions)
    c_act = jax.nn.silu(c)
    o1 = 2 * GM_WIDTH
    o2 = o1 + Q_LORA
    o3 = o2 + KV_LORA
    for l in range(DEPTH):
        mod = c_act @ w_ada[l] + b_ada[l]
        sh1, sc1, g1, sh2, sc2, g2 = [m[:, None, :] for m in jnp.split(mod, 6, axis=-1)]
        h = rms_norm(x, norm_mix_g[l]) * (1.0 + sc1) + sh1
        proj = h @ w_in[l]
        gm_out = gmlp_mixer(proj[..., :o1], gm_ln_g[l], gm_ln_b[l], gm_ws[l], gm_bs[l])
        mla_out = mla_mixer(proj[..., o1:o2], proj[..., o2:o3], proj[..., o3:], cos, sin,
                            q_a_g[l], w_q_up[l], kv_a_g[l], w_kv_up[l], q_norm_g[l], k_norm_g[l])
        mixed = jnp.concatenate([rms_norm(gm_out, out_norm_g[l, :GM_WIDTH]),
                                 rms_norm(mla_out, out_norm_g[l, GM_WIDTH:])], axis=-1)
        x = x + g1 * (mixed @ w_out[l])
        h = rms_norm(x, norm_ffn_g[l]) * (1.0 + sc2) + sh2
        x = x + g2 * moe_ffn(h, w_router[l], router_bias[l], w_exp_gate[l], w_exp_up[l],
                             w_exp_down[l], w_sh_gate[l], w_sh_up[l], w_sh_down[l])
    return x
```

```python
import functools

import jax
import jax.numpy as jnp
from jax import lax
from jax.experimental import pallas as pl
from jax.experimental.pallas import tpu as pltpu

D_MODEL = 1024
BATCH = 8
SEQ = 4096
DEPTH = 2
N_TOK = BATCH * SEQ
CHUNK = 64
GM_WIDTH = 512
GM_BLOCK = 128
GM_GROUPS = 4
GM_CH = 128
MLA_HEADS = 8
QK_NOPE = 64
QK_ROPE = 32
QK_DIM = 96
V_DIM = 64
MLA_WIDTH = 512
Q_LORA = 384
KV_LORA = 256
ROPE_THETA = 10000.0
N_EXPERTS = 64
TOP_K = 8
N_GROUPS = 8
GROUP_SIZE = N_EXPERTS // N_GROUPS
TOPK_GROUPS = 4
EXPERT_FF = 256
ROUTED_SCALE = 2.5
EPS = 1e-6

LANES = 128
HEAD_PAD = LANES
IN_COLS_PAD = 2 * GM_WIDTH + Q_LORA + KV_LORA + LANES
O_Q = 2 * GM_WIDTH
O_KV = O_Q + Q_LORA
O_PE = O_KV + KV_LORA

TT = 256
TQ = 256
TM_DENSE = 1024
VMEM_LIMIT = 56 * 1024 * 1024

F32 = jnp.float32
BF16 = jnp.bfloat16


def _cparams(sem):
    return pltpu.CompilerParams(dimension_semantics=sem, vmem_limit_bytes=VMEM_LIMIT)


def _rms_scale(v, width):
    return lax.rsqrt(jnp.sum(v * v, axis=-1, keepdims=True) * (1.0 / width) + EPS)


def _rope_kernel(pos_ref, inv_ref, cos_ref, sin_ref):
    ang = pos_ref[...].astype(F32) * inv_ref[...]
    lane = lax.broadcasted_iota(jnp.int32, ang.shape, 1)
    c = jnp.cos(ang)
    s = jnp.sin(ang)
    cos_ref[...] = jnp.where(lane < QK_NOPE, 1.0, jnp.where(lane < QK_DIM, c, 0.0))
    sin_ref[...] = jnp.where((lane >= QK_NOPE) & (lane < QK_NOPE + 16), -s,
                             jnp.where((lane >= QK_NOPE + 16) & (lane < QK_DIM), s, 0.0))


def _rope_tables(positions):
    inv_freq = 1.0 / (ROPE_THETA ** (jnp.arange(0, QK_ROPE, 2, dtype=F32) / QK_ROPE))
    inv = jnp.zeros((1, LANES), F32)
    inv = inv.at[0, QK_NOPE:QK_NOPE + 16].set(inv_freq)
    inv = inv.at[0, QK_NOPE + 16:QK_DIM].set(inv_freq)
    pos = positions.reshape(N_TOK, 1)
    tile = 2048
    return pl.pallas_call(
        _rope_kernel,
        out_shape=(jax.ShapeDtypeStruct((N_TOK, LANES), F32),
                   jax.ShapeDtypeStruct((N_TOK, LANES), F32)),
        grid=(N_TOK // tile,),
        in_specs=[pl.BlockSpec((tile, 1), lambda i: (i, 0)),
                  pl.BlockSpec((1, LANES), lambda i: (0, 0))],
        out_specs=(pl.BlockSpec((tile, LANES), lambda i: (i, 0)),
                   pl.BlockSpec((tile, LANES), lambda i: (i, 0))),
        compiler_params=_cparams(("arbitrary",)),
        name="rope_tables",
    )(pos, inv)


def _ada_kernel(c_ref, w_ref, b_ref, o_ref):
    c = c_ref[...]
    ca = c * jax.nn.sigmoid(c)
    o_ref[...] = jnp.dot(ca, w_ref[...], precision=lax.Precision.HIGHEST,
                         preferred_element_type=F32) + b_ref[...]


def _ada_mod(c, w_ada, b_ada):
    return pl.pallas_call(
        _ada_kernel,
        out_shape=jax.ShapeDtypeStruct((DEPTH, BATCH, 6 * D_MODEL), F32),
        grid=(DEPTH, 6),
        in_specs=[pl.BlockSpec((BATCH, D_MODEL), lambda l, j: (0, 0)),
                  pl.BlockSpec((None, D_MODEL, D_MODEL), lambda l, j: (l, 0, j)),
                  pl.BlockSpec((None, 1, D_MODEL), lambda l, j: (l, 0, j))],
        out_specs=pl.BlockSpec((None, BATCH, D_MODEL), lambda l, j: (l, 0, j)),
        compiler_params=_cparams(("arbitrary", "arbitrary")),
        name="ada_mod",
    )(c, w_ada, b_ada.reshape(DEPTH, 1, 6 * D_MODEL))


def _rope_rot(t, cos_t, sin_t, lane):
    swapped = jnp.where(lane < QK_NOPE + 16, pltpu.roll(t, LANES - 16, 1), pltpu.roll(t, 16, 1))
    return t * cos_t + swapped * sin_t


def _mix_in_kernel(x_ref, mod_ref, ng_ref, win_ref, lng_ref, lnb_ref, ws_ref, bs_ref,
                   qag_ref, wq_ref, kvg_ref, wkv_ref, qng_ref, kng_ref, og_ref,
                   cos_ref, sin_ref,
                   gm_ref, q_ref, k_ref, v_ref):
    x = x_ref[...]
    sh1 = mod_ref[0:1, :]
    sc1 = mod_ref[1:2, :]
    h = x * _rms_scale(x, D_MODEL) * ng_ref[...]
    h = h * (1.0 + sc1) + sh1
    proj = jnp.dot(h.astype(BF16), win_ref[...], preferred_element_type=F32)

    uv_pre = proj[:, :2 * GM_WIDTH]
    uv = 0.5 * uv_pre * (1.0 + lax.erf(uv_pre * (2.0 ** -0.5)))
    row = lax.broadcasted_iota(jnp.int32, (TT, TT), 0)
    col = lax.broadcasted_iota(jnp.int32, (TT, TT), 1)
    causal = (col <= row) & ((col // GM_BLOCK) == (row // GM_BLOCK))
    outs = []
    ssq = jnp.zeros((TT, 1), F32)
    for g in range(GM_GROUPS):
        sl = slice(g * GM_CH, (g + 1) * GM_CH)
        vg = uv[:, GM_WIDTH + g * GM_CH:GM_WIDTH + (g + 1) * GM_CH]
        mu = jnp.mean(vg, axis=-1, keepdims=True)
        xc = vg - mu
        var = jnp.mean(xc * xc, axis=-1, keepdims=True)
        vln = xc * lax.rsqrt(var + EPS) * lng_ref[:, sl] + lnb_ref[:, sl]
        w = jnp.where(causal, ws_ref[g], jnp.zeros((), BF16))
        gate = jnp.dot(w, vln.astype(BF16), preferred_element_type=F32) + bs_ref[:, sl]
        og = uv[:, sl] * gate
        ssq = ssq + jnp.sum(og * og, axis=-1, keepdims=True)
        outs.append(og)
    rs = lax.rsqrt(ssq * (1.0 / GM_WIDTH) + EPS)
    for g in range(GM_GROUPS):
        sl = slice(g * GM_CH, (g + 1) * GM_CH)
        gm_ref[:, sl] = (outs[g] * rs * og_ref[:, sl]).astype(BF16)

    cos_t = cos_ref[...]
    sin_t = sin_ref[...]
    lane = lax.broadcasted_iota(jnp.int32, (TT, LANES), 1)
    ql = proj[:, O_Q:O_KV]
    qn = ql * _rms_scale(ql, Q_LORA) * qag_ref[...]
    qa = jnp.dot(qn.astype(BF16), wq_ref[...], preferred_element_type=F32)
    kl = proj[:, O_KV:O_PE]
    kn = kl * _rms_scale(kl, KV_LORA) * kvg_ref[...]
    kva = jnp.dot(kn.astype(BF16), wkv_ref[...], preferred_element_type=F32)
    pe = proj[:, O_PE:O_PE + LANES]
    pe_ssq = jnp.sum(pe * pe, axis=-1, keepdims=True)
    pe_rot = _rope_rot(pe * kng_ref[...], cos_t, sin_t, lane)
    qscale = QK_DIM ** -0.5
    for hd in range(MLA_HEADS):
        sl = slice(hd * HEAD_PAD, (hd + 1) * HEAD_PAD)
        qh = qa[:, sl]
        qg = qh * _rms_scale(qh, QK_DIM) * qng_ref[...]
        q_ref[hd] = (_rope_rot(qg, cos_t, sin_t, lane) * qscale).astype(BF16)
        kh = kva[:, sl]
        krs = lax.rsqrt((jnp.sum(kh * kh, axis=-1, keepdims=True) + pe_ssq) * (1.0 / QK_DIM) + EPS)
        k_ref[hd] = ((kh * kng_ref[...] + pe_rot) * krs).astype(BF16)
        v_ref[hd] = kva[:, MLA_HEADS * HEAD_PAD + hd * HEAD_PAD:
                        MLA_HEADS * HEAD_PAD + (hd + 1) * HEAD_PAD].astype(BF16)


def _mix_in(x, mod_l, p, cos_t, sin_t):
    tiles_per_seq = SEQ // TT
    full = lambda shape: pl.BlockSpec(shape, lambda i: (0,) * len(shape))
    head_spec = pl.BlockSpec((None, MLA_HEADS, TT, HEAD_PAD),
                             lambda i: (i // tiles_per_seq, 0, i % tiles_per_seq, 0))
    head_shape = jax.ShapeDtypeStruct((BATCH, MLA_HEADS, SEQ, HEAD_PAD), BF16)
    return pl.pallas_call(
        _mix_in_kernel,
        out_shape=(jax.ShapeDtypeStruct((N_TOK, GM_WIDTH), BF16), head_shape, head_shape, head_shape),
        grid=(N_TOK // TT,),
        in_specs=[pl.BlockSpec((TT, D_MODEL), lambda i: (i, 0)),
                  pl.BlockSpec((None, 6, D_MODEL), lambda i: (i // tiles_per_seq, 0, 0)),
                  full((1, D_MODEL)),
                  full((D_MODEL, IN_COLS_PAD)),
                  full((1, GM_WIDTH)), full((1, GM_WIDTH)),
                  full((GM_GROUPS, TT, TT)), full((TT, GM_WIDTH)),
                  full((1, Q_LORA)), full((Q_LORA, MLA_HEADS * HEAD_PAD)),
                  full((1, KV_LORA)), full((KV_LORA, 2 * MLA_HEADS * HEAD_PAD)),
                  full((1, LANES)), full((1, LANES)), full((1, GM_WIDTH)),
                  pl.BlockSpec((TT, LANES), lambda i: (i, 0)),
                  pl.BlockSpec((TT, LANES), lambda i: (i, 0))],
        out_specs=(pl.BlockSpec((TT, GM_WIDTH), lambda i: (i, 0)), head_spec, head_spec, head_spec),
        compiler_params=_cparams(("arbitrary",)),
        name="mix_in",
    )(x, mod_l, p["norm_mix_g"], p["w_in"], p["gm_ln_g"], p["gm_ln_b"], p["gm_ws"], p["gm_bs"],
      p["q_a_g"], p["w_q_up"], p["kv_a_g"], p["w_kv_up"], p["q_norm_g"], p["k_norm_g"],
      p["out_norm_gm"], cos_t, sin_t)


def _attn_kernel(q_ref, k_ref, v_ref, o_ref):
    qi = pl.program_id(2)
    row = lax.broadcasted_iota(jnp.int32, (TQ, TQ), 0)
    col = lax.broadcasted_iota(jnp.int32, (TQ, TQ), 1)
    diag_ok = (col // CHUNK) <= (row // CHUNK)
    nt = (((1,), (1,)), ((), ()))
    outs = []
    for hh in range(2):
        q = q_ref[hh]

        def tile(j, carry, masked):
            m, l, acc = carry
            start = pl.multiple_of(j * TQ, TQ)
            kj = k_ref[hh, pl.ds(start, TQ), :]
            vj = v_ref[hh, pl.ds(start, TQ), :]
            s = lax.dot_general(q, kj, nt, preferred_element_type=F32)
            if masked:
                s = jnp.where(diag_ok, s, -jnp.inf)
            m_new = jnp.maximum(m, jnp.max(s, axis=-1, keepdims=True))
            p = jnp.exp(s - m_new)
            alpha = jnp.exp(m - m_new)
            l = alpha * l + jnp.sum(p, axis=-1, keepdims=True)
            acc = alpha * acc + jnp.dot(p.astype(BF16), vj, preferred_element_type=F32)
            return m_new, l, acc

        init = (jnp.full((TQ, 1), -jnp.inf, F32), jnp.zeros((TQ, 1), F32),
                jnp.zeros((TQ, HEAD_PAD), F32))
        carry = lax.fori_loop(0, qi, functools.partial(tile, masked=False), init)
        m, l, acc = tile(qi, carry, True)
        outs.append(acc / l)
    o_ref[...] = jnp.concatenate([outs[0][:, :V_DIM], outs[1][:, :V_DIM]], axis=-1)


def _attention(q, k, v):
    qspec = pl.BlockSpec((None, 2, TQ, HEAD_PAD), lambda b, hp, i: (b, hp, i, 0))
    kvspec = pl.BlockSpec((None, 2, SEQ, HEAD_PAD), lambda b, hp, i: (b, hp, 0, 0))
    return pl.pallas_call(
        _attn_kernel,
        out_shape=jax.ShapeDtypeStruct((BATCH, SEQ, MLA_WIDTH), F32),
        grid=(BATCH, MLA_HEADS // 2, SEQ // TQ),
        in_specs=[qspec, kvspec, kvspec],
        out_specs=pl.BlockSpec((None, TQ, 2 * V_DIM), lambda b, hp, i: (b, i, hp)),
        compiler_params=_cparams(("arbitrary", "arbitrary", "arbitrary")),
        name="mla_attention",
    )(q, k, v)


def _route(logits_t, bias_col):
    t = logits_t.shape[1]
    scores = jax.nn.sigmoid(logits_t)
    biased = scores + bias_col
    neg = -jnp.inf
    b3 = biased.reshape(N_GROUPS, GROUP_SIZE, t)
    m1 = jnp.max(b3, axis=1, keepdims=True)
    n_max = jnp.sum((b3 == m1).astype(F32), axis=1, keepdims=True)
    m2 = jnp.max(jnp.where(b3 < m1, b3, neg), axis=1, keepdims=True)
    gs = (m1 + jnp.where(n_max >= 2.0, m1, m2)).reshape(N_GROUPS, t)
    gidx = lax.broadcasted_iota(jnp.int32, (N_GROUPS, t), 0)
    grank = jnp.zeros((N_GROUPS, t), F32)
    for g in range(N_GROUPS):
        other = gs[g:g + 1, :]
        ahead = (other > gs) | ((other == gs) & (gidx > g))
        grank = grank + ahead.astype(F32)
    gsel = grank < float(TOPK_GROUPS)
    emask = jnp.broadcast_to(gsel.reshape(N_GROUPS, 1, t), (N_GROUPS, GROUP_SIZE, t)).reshape(N_EXPERTS, t)
    ms = jnp.where(emask, biased, neg)
    eidx = lax.broadcasted_iota(jnp.int32, (N_EXPERTS, t), 0)
    rank = jnp.zeros((N_EXPERTS, t), F32)
    for e in range(N_EXPERTS):
        other = ms[e:e + 1, :]
        ahead = (other > ms) | ((other == ms) & (eidx > e))
        rank = rank + ahead.astype(F32)
    sel = rank < float(TOP_K)
    w = jnp.where(sel, scores, 0.0)
    denom = jnp.sum(w, axis=0, keepdims=True)
    return (w / denom) * ROUTED_SCALE


def _mix_out_kernel(x_ref, gm_ref, mla_ref, mod_ref, og_ref, wout_ref, fg_ref, wr_ref, rb_ref,
                    x1_ref, h2_ref, comb_ref):
    mla = mla_ref[...]
    mla_n = mla * _rms_scale(mla, MLA_WIDTH) * og_ref[...]
    mixed = jnp.concatenate([gm_ref[...], mla_n.astype(BF16)], axis=-1)
    y = jnp.dot(mixed, wout_ref[...], preferred_element_type=F32)
    g1 = mod_ref[2:3, :]
    sh2 = mod_ref[3:4, :]
    sc2 = mod_ref[4:5, :]
    x1 = x_ref[...] + g1 * y
    x1_ref[...] = x1
    h2 = x1 * _rms_scale(x1, D_MODEL) * fg_ref[...]
    h2 = h2 * (1.0 + sc2) + sh2
    h2_ref[...] = h2.astype(BF16)
    logits_t = lax.dot_general(wr_ref[...], h2, (((1,), (1,)), ((), ())),
                               precision=lax.Precision.HIGHEST, preferred_element_type=F32)
    comb_ref[...] = _route(logits_t, rb_ref[...])


def _mix_out(x, gm, mla, mod_l, p):
    tiles_per_seq = SEQ // TT
    full = lambda shape: pl.BlockSpec(shape, lambda i: (0,) * len(shape))
    tok = lambda width: pl.BlockSpec((TT, width), lambda i: (i, 0))
    return pl.pallas_call(
        _mix_out_kernel,
        out_shape=(jax.ShapeDtypeStruct((N_TOK, D_MODEL), F32),
                   jax.ShapeDtypeStruct((N_TOK, D_MODEL), BF16),
                   jax.ShapeDtypeStruct((N_EXPERTS, N_TOK), F32)),
        grid=(N_TOK // TT,),
        in_specs=[tok(D_MODEL), tok(GM_WIDTH), tok(MLA_WIDTH),
                  pl.BlockSpec((None, 6, D_MODEL), lambda i: (i // tiles_per_seq, 0, 0)),
                  full((1, MLA_WIDTH)), full((D_MODEL, D_MODEL)), full((1, D_MODEL)),
                  full((N_EXPERTS, D_MODEL)), full((N_EXPERTS, 1))],
        out_specs=(tok(D_MODEL), tok(D_MODEL), pl.BlockSpec((N_EXPERTS, TT), lambda i: (0, i))),
        compiler_params=_cparams(("arbitrary",)),
        name="mix_out_route",
    )(x, gm, mla, mod_l, p["out_norm_mla"], p["w_out"], p["norm_ffn_g"], p["w_router_t"],
      p["router_bias"])


def _moe_dense_kernel(h_ref, comb_ref, x1_ref, mod_ref, wgu_ref, wd_ref, sgu_ref, sd_ref,
                      o_ref, acc_ref):
    e = pl.program_id(1)
    h = h_ref[...]

    @pl.when(e == 0)
    def _():
        hs = jnp.dot(h, sgu_ref[...], preferred_element_type=F32)
        g = hs[:, :EXPERT_FF]
        a = (g * jax.nn.sigmoid(g)) * hs[:, EXPERT_FF:]
        acc_ref[...] = jnp.dot(a.astype(BF16), sd_ref[...], preferred_element_type=F32)

    hgu = jnp.dot(h, wgu_ref[...], preferred_element_type=F32)
    g = hgu[:, :EXPERT_FF]
    lane = lax.broadcasted_iota(jnp.int32, comb_ref.shape, 1)
    ce = jnp.sum(jnp.where(lane == e, comb_ref[...], 0.0), axis=-1, keepdims=True)
    a = (g * jax.nn.sigmoid(g)) * hgu[:, EXPERT_FF:] * ce
    acc_ref[...] += jnp.dot(a.astype(BF16), wd_ref[...], preferred_element_type=F32)

    @pl.when(e == N_EXPERTS - 1)
    def _():
        o_ref[...] = x1_ref[...] + mod_ref[5:6, :] * acc_ref[...]


def _moe_dense(h2, comb, x1, mod_l, p):
    tm = TM_DENSE
    tiles_per_seq = SEQ // tm
    return pl.pallas_call(
        _moe_dense_kernel,
        out_shape=jax.ShapeDtypeStruct((N_TOK, D_MODEL), F32),
        grid=(N_TOK // tm, N_EXPERTS),
        in_specs=[pl.BlockSpec((tm, D_MODEL), lambda i, e: (i, 0)),
                  pl.BlockSpec((tm, N_EXPERTS), lambda i, e: (i, 0)),
                  pl.BlockSpec((tm, D_MODEL), lambda i, e: (i, 0)),
                  pl.BlockSpec((None, 6, D_MODEL), lambda i, e: (i // tiles_per_seq, 0, 0)),
                  pl.BlockSpec((None, D_MODEL, 2 * EXPERT_FF), lambda i, e: (e, 0, 0)),
                  pl.BlockSpec((None, EXPERT_FF, D_MODEL), lambda i, e: (e, 0, 0)),
                  pl.BlockSpec((D_MODEL, 2 * EXPERT_FF), lambda i, e: (0, 0)),
                  pl.BlockSpec((EXPERT_FF, D_MODEL), lambda i, e: (0, 0))],
        out_specs=pl.BlockSpec((tm, D_MODEL), lambda i, e: (i, 0)),
        scratch_shapes=[pltpu.VMEM((tm, D_MODEL), F32)],
        compiler_params=_cparams(("arbitrary", "arbitrary")),
        name="moe_dense",
    )(h2, comb, x1, mod_l, p["w_exp_gu"], p["w_exp_down"], p["w_sh_gu"], p["w_sh_down"])


def _pad_heads(w, per_head):
    k = w.shape[0]
    w = w.reshape(k, MLA_HEADS, per_head)
    w = jnp.pad(w, ((0, 0), (0, 0), (0, HEAD_PAD - per_head)))
    return w.reshape(k, MLA_HEADS * HEAD_PAD)


def _layer_params(l, a):
    w_in = a["w_in"][l]
    pe_cols = jnp.zeros((D_MODEL, LANES), F32).at[:, QK_NOPE:QK_DIM].set(w_in[:, O_PE:])
    w_in_p = jnp.concatenate([w_in[:, :O_PE], pe_cols], axis=1).astype(BF16)
    w_kv = a["w_kv_up"][l].reshape(KV_LORA, MLA_HEADS, QK_NOPE + V_DIM)
    w_k = _pad_heads(w_kv[:, :, :QK_NOPE].reshape(KV_LORA, MLA_HEADS * QK_NOPE), QK_NOPE)
    w_v = _pad_heads(w_kv[:, :, QK_NOPE:].reshape(KV_LORA, MLA_HEADS * V_DIM), V_DIM)
    blocks = TT // GM_BLOCK
    eye = jnp.eye(blocks, dtype=F32)
    ws_big = jnp.einsum("ab,gts->gatbs", eye, a["gm_ws"][l]).reshape(GM_GROUPS, TT, TT)
    bs_tile = jnp.tile(jnp.repeat(a["gm_bs"][l].T, GM_CH, axis=1), (blocks, 1))
    pad96 = lambda g: jnp.pad(g, (0, LANES - QK_DIM)).reshape(1, LANES)
    return {
        "norm_mix_g": a["norm_mix_g"][l].reshape(1, D_MODEL),
        "norm_ffn_g": a["norm_ffn_g"][l].reshape(1, D_MODEL),
        "w_in": w_in_p,
        "gm_ln_g": a["gm_ln_g"][l].reshape(1, GM_WIDTH),
        "gm_ln_b": a["gm_ln_b"][l].reshape(1, GM_WIDTH),
        "gm_ws": ws_big.astype(BF16),
        "gm_bs": bs_tile,
        "q_a_g": a["q_a_g"][l].reshape(1, Q_LORA),
        "w_q_up": _pad_heads(a["w_q_up"][l], QK_DIM).astype(BF16),
        "kv_a_g": a["kv_a_g"][l].reshape(1, KV_LORA),
        "w_kv_up": jnp.concatenate([w_k, w_v], axis=1).astype(BF16),
        "q_norm_g": pad96(a["q_norm_g"][l]),
        "k_norm_g": pad96(a["k_norm_g"][l]),
        "out_norm_gm": a["out_norm_g"][l, :GM_WIDTH].reshape(1, GM_WIDTH),
        "out_norm_mla": a["out_norm_g"][l, GM_WIDTH:].reshape(1, MLA_WIDTH),
        "w_out": a["w_out"][l].astype(BF16),
        "w_router_t": a["w_router"][l].T,
        "router_bias": a["router_bias"][l].reshape(N_EXPERTS, 1),
        "w_exp_gu": jnp.concatenate([a["w_exp_gate"][l], a["w_exp_up"][l]], axis=-1).astype(BF16),
        "w_exp_down": a["w_exp_down"][l].astype(BF16),
        "w_sh_gu": jnp.concatenate([a["w_sh_gate"][l], a["w_sh_up"][l]], axis=-1).astype(BF16),
        "w_sh_down": a["w_sh_down"][l].astype(BF16),
    }


def kernel(x, c, positions, norm_mix_g, norm_ffn_g, w_ada, b_ada, w_in, gm_ln_g, gm_ln_b, gm_ws, gm_bs, q_a_g, w_q_up, kv_a_g, w_kv_up, q_norm_g, k_norm_g, out_norm_g, w_out, w_router, router_bias, w_exp_gate, w_exp_up, w_exp_down, w_sh_gate, w_sh_up, w_sh_down):
    a = dict(norm_mix_g=norm_mix_g, norm_ffn_g=norm_ffn_g, w_in=w_in, gm_ln_g=gm_ln_g,
             gm_ln_b=gm_ln_b, gm_ws=gm_ws, gm_bs=gm_bs, q_a_g=q_a_g, w_q_up=w_q_up,
             kv_a_g=kv_a_g, w_kv_up=w_kv_up, q_norm_g=q_norm_g, k_norm_g=k_norm_g,
             out_norm_g=out_norm_g, w_out=w_out, w_router=w_router, router_bias=router_bias,
             w_exp_gate=w_exp_gate, w_exp_up=w_exp_up, w_exp_down=w_exp_down,
             w_sh_gate=w_sh_gate, w_sh_up=w_sh_up, w_sh_down=w_sh_down)
    cos_t, sin_t = _rope_tables(positions)
    mod = _ada_mod(c, w_ada, b_ada).reshape(DEPTH, BATCH, 6, D_MODEL)
    xt = x.reshape(N_TOK, D_MODEL)
    for l in range(DEPTH):
        p = _layer_params(l, a)
        gm, q, k, v = _mix_in(xt, mod[l], p, cos_t, sin_t)
        mla = _attention(q, k, v).reshape(N_TOK, MLA_WIDTH)
        x1, h2, comb_t = _mix_out(xt, gm, mla, mod[l], p)
        xt = _moe_dense(h2, comb_t.T, x1, mod[l], p)
    return xt.reshape(BATCH, SEQ, D_MODEL)
```

```python
import functools

import jax
import jax.numpy as jnp
from jax import lax
from jax.experimental import pallas as pl
from jax.experimental.pallas import tpu as pltpu

D_MODEL = 1024
BATCH = 8
SEQ = 4096
DEPTH = 2
N_TOK = BATCH * SEQ
CHUNK = 64
GM_WIDTH = 512
GM_BLOCK = 128
GM_GROUPS = 4
GM_CH = 128
MLA_HEADS = 8
QK_NOPE = 64
QK_ROPE = 32
QK_DIM = 96
V_DIM = 64
MLA_WIDTH = 512
Q_LORA = 384
KV_LORA = 256
ROPE_THETA = 10000.0
N_EXPERTS = 64
TOP_K = 8
N_GROUPS = 8
GROUP_SIZE = N_EXPERTS // N_GROUPS
TOPK_GROUPS = 4
EXPERT_FF = 256
ROUTED_SCALE = 2.5
EPS = 1e-6

LANES = 128
HEAD_PAD = LANES
IN_COLS_PAD = 2 * GM_WIDTH + Q_LORA + KV_LORA + LANES
O_Q = 2 * GM_WIDTH
O_KV = O_Q + Q_LORA
O_PE = O_KV + KV_LORA

TT = 256
TM = 512
TQ = 512
TK = 256
ATT_HEADS = 8
V_AUG = 80
N_TILES = N_TOK // TT
H_AUG = D_MODEL + LANES
SEG_ALIGN = 16
CHUNK_ROWS = 256
LOCAL_ROWS = TT * TOP_K + N_EXPERTS * SEG_ALIGN
MAX_GROUPS = LOCAL_ROWS // SEG_ALIGN
TM_FFN = 512
FFN_RING = 3
SORTED_ROWS = -(-(N_TOK * TOP_K + N_TILES * N_EXPERTS * (SEG_ALIGN - 1)
                  + N_EXPERTS * (TM_FFN - 1)) // TM_FFN) * TM_FFN
FFN_TILES = SORTED_ROWS // TM_FFN
GROUPS_PER_CHUNK = CHUNK_ROWS // SEG_ALIGN
TRASH_ROWS = 2 * CHUNK_ROWS
STATIC_CHUNKS = 10
VMEM_LIMIT = 56 * 1024 * 1024

F32 = jnp.float32
BF16 = jnp.bfloat16


def _cparams(sem, flags=None):
    return pltpu.CompilerParams(dimension_semantics=sem, vmem_limit_bytes=VMEM_LIMIT, flags=flags)


def _rms_scale(v, width):
    return lax.rsqrt(jnp.sum(v * v, axis=-1, keepdims=True) * (1.0 / width) + EPS)


def _rope_kernel(pos_ref, inv_ref, cos_ref, sin_ref):
    ang = pos_ref[...].astype(F32) * inv_ref[...]
    lane = lax.broadcasted_iota(jnp.int32, ang.shape, 1)
    c = jnp.cos(ang)
    s = jnp.sin(ang)
    cos_ref[...] = jnp.where(lane < QK_NOPE, 1.0, jnp.where(lane < QK_DIM, c, 0.0))
    sin_ref[...] = jnp.where((lane >= QK_NOPE) & (lane < QK_NOPE + 16), -s,
                             jnp.where((lane >= QK_NOPE + 16) & (lane < QK_DIM), s, 0.0))


def _rope_tables(positions):
    inv_freq = 1.0 / (ROPE_THETA ** (jnp.arange(0, QK_ROPE, 2, dtype=F32) / QK_ROPE))
    inv = jnp.zeros((1, LANES), F32)
    inv = inv.at[0, QK_NOPE:QK_NOPE + 16].set(inv_freq)
    inv = inv.at[0, QK_NOPE + 16:QK_DIM].set(inv_freq)
    pos = positions.reshape(N_TOK, 1)
    tile = 2048
    return pl.pallas_call(
        _rope_kernel,
        out_shape=(jax.ShapeDtypeStruct((N_TOK, LANES), F32),
                   jax.ShapeDtypeStruct((N_TOK, LANES), F32)),
        grid=(N_TOK // tile,),
        in_specs=[pl.BlockSpec((tile, 1), lambda i: (i, 0)),
                  pl.BlockSpec((1, LANES), lambda i: (0, 0))],
        out_specs=(pl.BlockSpec((tile, LANES), lambda i: (i, 0)),
                   pl.BlockSpec((tile, LANES), lambda i: (i, 0))),
        compiler_params=_cparams(("arbitrary",)),
        name="rope_tables",
    )(pos, inv)


def _ada_kernel(c_ref, w_ref, b_ref, o_ref):
    c = c_ref[...]
    ca = c * jax.nn.sigmoid(c)
    o_ref[...] = jnp.dot(ca, w_ref[...], precision=lax.Precision.HIGHEST,
                         preferred_element_type=F32) + b_ref[...]


def _ada_mod(c, w_ada, b_ada):
    return pl.pallas_call(
        _ada_kernel,
        out_shape=jax.ShapeDtypeStruct((DEPTH, BATCH, 6 * D_MODEL), F32),
        grid=(DEPTH, 6),
        in_specs=[pl.BlockSpec((BATCH, D_MODEL), lambda l, j: (0, 0)),
                  pl.BlockSpec((None, D_MODEL, D_MODEL), lambda l, j: (l, 0, j)),
                  pl.BlockSpec((None, 1, D_MODEL), lambda l, j: (l, 0, j))],
        out_specs=pl.BlockSpec((None, BATCH, D_MODEL), lambda l, j: (l, 0, j)),
        compiler_params=_cparams(("arbitrary", "arbitrary")),
        name="ada_mod",
    )(c, w_ada, b_ada.reshape(DEPTH, 1, 6 * D_MODEL))


def _rope_rot(t, cos_t, sin_t, lane):
    swapped = jnp.where(lane < QK_NOPE + 16, pltpu.roll(t, LANES - 16, 1), pltpu.roll(t, 16, 1))
    return t * cos_t + swapped * sin_t


def _mix_in_kernel(x_ref, mod_ref, ng_ref, win_ref, lng_ref, lnb_ref, ws_ref, bs_ref,
                   qag_ref, wq_ref, kvg_ref, wkv_ref, wvt_ref, vone_ref, qng_ref, kng_ref, og_ref,
                   cos_ref, sin_ref,
                   gm_ref, q_ref, k_ref, vt_ref):
    x = x_ref[...]
    sh1 = mod_ref[0:1, :]
    sc1 = mod_ref[1:2, :]
    h = x * _rms_scale(x, D_MODEL) * ng_ref[...]
    h = h * (1.0 + sc1) + sh1
    proj = jnp.dot(h.astype(BF16), win_ref[...], preferred_element_type=F32)

    uv_pre = proj[:, :2 * GM_WIDTH]
    uv = 0.5 * uv_pre * (1.0 + lax.erf(uv_pre * (2.0 ** -0.5)))
    row = lax.broadcasted_iota(jnp.int32, (TT, TT), 0)
    col = lax.broadcasted_iota(jnp.int32, (TT, TT), 1)
    causal = (col <= row) & ((col // GM_BLOCK) == (row // GM_BLOCK))
    groups = range(GM_GROUPS)
    gsl = [slice(g * GM_CH, (g + 1) * GM_CH) for g in groups]
    vg = [uv[:, GM_WIDTH + g * GM_CH:GM_WIDTH + (g + 1) * GM_CH] for g in groups]
    mu = [jnp.mean(vg[g], axis=-1, keepdims=True) for g in groups]
    xc = [vg[g] - mu[g] for g in groups]
    var = [jnp.mean(xc[g] * xc[g], axis=-1, keepdims=True) for g in groups]
    vln = [(xc[g] * lax.rsqrt(var[g] + EPS) * lng_ref[:, gsl[g]] + lnb_ref[:, gsl[g]]).astype(BF16)
           for g in groups]
    ws = [jnp.where(causal, ws_ref[g], jnp.zeros((), BF16)) for g in groups]
    gate = [jnp.concatenate([jnp.dot(ws[g], vln[g][s * TT:(s + 1) * TT], preferred_element_type=F32)
                             for s in range(TM // TT)], axis=0) for g in groups]
    outs = [uv[:, gsl[g]] * (gate[g] + bs_ref[:, gsl[g]]) for g in groups]
    sq = [jnp.sum(outs[g] * outs[g], axis=-1, keepdims=True) for g in groups]
    rs = lax.rsqrt((sq[0] + sq[1] + sq[2] + sq[3]) * (1.0 / GM_WIDTH) + EPS)
    for g in groups:
        gm_ref[:, gsl[g]] = (outs[g] * rs * og_ref[:, gsl[g]]).astype(BF16)

    cos_t = cos_ref[...]
    sin_t = sin_ref[...]
    lane = lax.broadcasted_iota(jnp.int32, (TM, LANES), 1)
    ql = proj[:, O_Q:O_KV]
    qn = ql * _rms_scale(ql, Q_LORA) * qag_ref[...]
    qa = jnp.dot(qn.astype(BF16), wq_ref[...], preferred_element_type=F32)
    kl = proj[:, O_KV:O_PE]
    kn = kl * _rms_scale(kl, KV_LORA) * kvg_ref[...]
    kva = jnp.dot(kn.astype(BF16), wkv_ref[...], preferred_element_type=F32)
    pe = proj[:, O_PE:O_PE + LANES]
    pe_ssq = jnp.sum(pe * pe, axis=-1, keepdims=True)
    pe_rot = _rope_rot(pe * kng_ref[...], cos_t, sin_t, lane)
    qscale = (QK_DIM ** -0.5) * 1.4426950408889634
    v_t = lax.dot_general(wvt_ref[...], kn.astype(BF16), (((1,), (1,)), ((), ())),
                          preferred_element_type=F32) + vone_ref[...]
    vt_ref[...] = v_t.reshape(MLA_HEADS, V_AUG, TM).astype(BF16)
    heads = range(MLA_HEADS)
    qh = [qa[:, hd * HEAD_PAD:(hd + 1) * HEAD_PAD] for hd in heads]
    kh = [kva[:, hd * HEAD_PAD:(hd + 1) * HEAD_PAD] for hd in heads]
    q_ssq = [jnp.sum(qh[hd] * qh[hd], axis=-1, keepdims=True) for hd in heads]
    k_ssq = [jnp.sum(kh[hd] * kh[hd], axis=-1, keepdims=True) for hd in heads]
    qg = [qh[hd] * lax.rsqrt(q_ssq[hd] * (1.0 / QK_DIM) + EPS) * qng_ref[...] for hd in heads]
    q_lo = [pltpu.roll(qg[hd], LANES - 16, 1) for hd in heads]
    q_hi = [pltpu.roll(qg[hd], 16, 1) for hd in heads]
    first_half = lane < QK_NOPE + 16
    for hd in heads:
        rot = qg[hd] * cos_t + jnp.where(first_half, q_lo[hd], q_hi[hd]) * sin_t
        q_ref[hd] = (rot * qscale).astype(BF16)
    for hd in heads:
        krs = lax.rsqrt((k_ssq[hd] + pe_ssq) * (1.0 / QK_DIM) + EPS)
        k_ref[hd] = ((kh[hd] * kng_ref[...] + pe_rot) * krs).astype(BF16)


def _mix_in(x, mod_l, p, cos_t, sin_t):
    tiles_per_seq = SEQ // TM
    full = lambda shape: pl.BlockSpec(shape, lambda i: (0,) * len(shape))
    head_spec = pl.BlockSpec((None, MLA_HEADS, TM, HEAD_PAD),
                             lambda i: (i // tiles_per_seq, 0, i % tiles_per_seq, 0))
    head_shape = jax.ShapeDtypeStruct((BATCH, MLA_HEADS, SEQ, HEAD_PAD), BF16)
    return pl.pallas_call(
        _mix_in_kernel,
        out_shape=(jax.ShapeDtypeStruct((N_TOK, GM_WIDTH), BF16), head_shape, head_shape,
                   jax.ShapeDtypeStruct((BATCH, MLA_HEADS, V_AUG, SEQ), BF16)),
        grid=(N_TOK // TM,),
        in_specs=[pl.BlockSpec((TM, D_MODEL), lambda i: (i, 0)),
                  pl.BlockSpec((None, 6, D_MODEL), lambda i: (i // tiles_per_seq, 0, 0)),
                  full((1, D_MODEL)),
                  full((D_MODEL, IN_COLS_PAD)),
                  full((1, GM_WIDTH)), full((1, GM_WIDTH)),
                  full((GM_GROUPS, TT, TT)), full((TM, GM_WIDTH)),
                  full((1, Q_LORA)), full((Q_LORA, MLA_HEADS * HEAD_PAD)),
                  full((1, KV_LORA)), full((KV_LORA, MLA_HEADS * HEAD_PAD)),
                  full((MLA_HEADS * V_AUG, KV_LORA)), full((MLA_HEADS * V_AUG, 1)),
                  full((1, LANES)), full((1, LANES)), full((1, GM_WIDTH)),
                  pl.BlockSpec((TM, LANES), lambda i: (i, 0)),
                  pl.BlockSpec((TM, LANES), lambda i: (i, 0))],
        out_specs=(pl.BlockSpec((TM, GM_WIDTH), lambda i: (i, 0)), head_spec, head_spec,
                   pl.BlockSpec((None, MLA_HEADS, V_AUG, TM),
                                lambda i: (i // tiles_per_seq, 0, 0, i % tiles_per_seq))),
        compiler_params=_cparams(("arbitrary",)),
        name="mix_in",
    )(x, mod_l, p["norm_mix_g"], p["w_in"], p["gm_ln_g"], p["gm_ln_b"], p["gm_ws"], p["gm_bs"],
      p["q_a_g"], p["w_q_up"], p["kv_a_g"], p["w_k_up"], p["w_v_t"], p["v_ones"], p["q_norm_g"],
      p["k_norm_g"],
      p["out_norm_gm"], cos_t, sin_t)


def _attn_kernel(q_ref, k_ref, vt_ref, o_ref):
    qi = pl.program_id(2)
    krow = lax.broadcasted_iota(jnp.int32, (TK, TQ), 0)
    qcol = lax.broadcasted_iota(jnp.int32, (TK, TQ), 1)
    nt = (((1,), (1,)), ((), ()))
    per_q = TQ // TK

    def tile(j, carry, diag):
        start = pl.multiple_of(j * TK, TK)
        heads = range(ATT_HEADS)
        s = [lax.dot_general(k_ref[hh, pl.ds(start, TK), :], q_ref[hh], nt,
                             preferred_element_type=F32) for hh in heads]
        if diag is not None:
            allowed = ((krow + diag * TK) // CHUNK) <= (qcol // CHUNK)
            s = [jnp.where(allowed, sh, -jnp.inf) for sh in s]
        m_new = [jnp.maximum(carry[hh][0], jnp.max(s[hh], axis=0, keepdims=True)) for hh in heads]
        p = [jnp.exp2(s[hh] - m_new[hh]) for hh in heads]
        alpha = [jnp.exp2(carry[hh][0] - m_new[hh]) for hh in heads]
        pv = [jnp.dot(vt_ref[hh, :, pl.ds(start, TK)], p[hh].astype(BF16),
                      preferred_element_type=F32) for hh in heads]
        return tuple((m_new[hh], alpha[hh] * carry[hh][1] + pv[hh]) for hh in heads)

    init = tuple((jnp.full((1, TQ), -jnp.inf, F32), jnp.zeros((V_AUG, TQ), F32))
                 for _ in range(ATT_HEADS))
    carry = lax.fori_loop(0, qi * per_q, functools.partial(tile, diag=None), init)
    for t in range(per_q):
        carry = tile(qi * per_q + t, carry, t)
    out_t = jnp.concatenate([acc[:V_DIM] / acc[V_DIM:V_DIM + 1] for (_, acc) in carry], axis=0)
    o_ref[...] = out_t.T


def _attention(q, k, vt):
    groups = MLA_HEADS // ATT_HEADS
    qspec = pl.BlockSpec((None, ATT_HEADS, TQ, HEAD_PAD), lambda b, hp, i: (b, hp, i, 0))
    kspec = pl.BlockSpec((None, ATT_HEADS, SEQ, HEAD_PAD), lambda b, hp, i: (b, hp, 0, 0))
    vspec = pl.BlockSpec((None, ATT_HEADS, V_AUG, SEQ), lambda b, hp, i: (b, hp, 0, 0))
    return pl.pallas_call(
        _attn_kernel,
        out_shape=jax.ShapeDtypeStruct((BATCH, SEQ, MLA_WIDTH), F32),
        grid=(BATCH, groups, SEQ // TQ),
        in_specs=[qspec, kspec, vspec],
        out_specs=pl.BlockSpec((None, TQ, ATT_HEADS * V_DIM), lambda b, hp, i: (b, i, hp)),
        compiler_params=_cparams(("arbitrary", "arbitrary", "arbitrary")),
        name="mla_attention",
    )(q, k, vt)


def _route(logits_t, bias_col):
    t = logits_t.shape[1]
    scores = jax.nn.sigmoid(logits_t)
    biased = scores + bias_col
    neg = -jnp.inf
    b3 = biased.reshape(N_GROUPS, GROUP_SIZE, t)
    m1 = jnp.max(b3, axis=1, keepdims=True)
    n_max = jnp.sum((b3 == m1).astype(F32), axis=1, keepdims=True)
    m2 = jnp.max(jnp.where(b3 < m1, b3, neg), axis=1, keepdims=True)
    gs = (m1 + jnp.where(n_max >= 2.0, m1, m2)).reshape(N_GROUPS, t)
    gidx = lax.broadcasted_iota(jnp.int32, (N_GROUPS, t), 0)
    grank = jnp.zeros((N_GROUPS, t), F32)
    for g in range(N_GROUPS):
        other = gs[g:g + 1, :]
        ahead = (other > gs) | ((other == gs) & (gidx > g))
        grank = grank + ahead.astype(F32)
    gsel = grank < float(TOPK_GROUPS)
    emask = jnp.broadcast_to(gsel.reshape(N_GROUPS, 1, t), (N_GROUPS, GROUP_SIZE, t)).reshape(N_EXPERTS, t)
    ms = jnp.where(emask, biased, neg)
    eidx = lax.broadcasted_iota(jnp.int32, (N_EXPERTS, t), 0).astype(F32)
    picked = jnp.zeros((N_EXPERTS, t), F32)
    for _ in range(TOP_K):
        top = jnp.max(ms, axis=0, keepdims=True)
        first = jnp.min(jnp.where(ms == top, eidx, float(N_EXPERTS)), axis=0, keepdims=True)
        hit = eidx == first
        picked = jnp.where(hit, 1.0, picked)
        ms = jnp.where(hit, neg, ms)
    sel = picked > 0.5
    w = jnp.where(sel, scores, 0.0)
    denom = jnp.sum(w, axis=0, keepdims=True)
    return (w / denom) * ROUTED_SCALE, sel


def _mix_out_kernel(x_ref, gm_ref, mla_ref, mod_ref, og_ref, wout_ref, fg_ref, wr_ref, rb_ref,
                    x1_ref, h2_ref, rkt_ref, rktok_ref, cnt_ref):
    mla = mla_ref[...]
    mla_n = mla * _rms_scale(mla, MLA_WIDTH) * og_ref[...]
    mixed = jnp.concatenate([gm_ref[...], mla_n.astype(BF16)], axis=-1)
    y = jnp.dot(mixed, wout_ref[...], preferred_element_type=F32)
    g1 = mod_ref[2:3, :]
    sh2 = mod_ref[3:4, :]
    sc2 = mod_ref[4:5, :]
    x1 = x_ref[...] + g1 * y
    x1_ref[...] = x1
    h2 = x1 * _rms_scale(x1, D_MODEL) * fg_ref[...]
    h2 = h2 * (1.0 + sc2) + sh2
    logits_t = lax.dot_general(wr_ref[...], h2, (((1,), (1,)), ((), ())),
                               precision=lax.Precision.HIGHEST, preferred_element_type=F32)
    comb, sel = _route(logits_t, rb_ref[...])
    before = (lax.broadcasted_iota(jnp.int32, (TT, TT), 0)
              < lax.broadcasted_iota(jnp.int32, (TT, TT), 1))
    self = jnp.where(sel, 1.0, 0.0)
    before_b = jnp.where(before, 1.0, 0.0).astype(BF16)
    rank = jnp.concatenate(
        [jnp.dot(self[:, s * TT:(s + 1) * TT].astype(BF16), before_b, preferred_element_type=F32)
         for s in range(TM // TT)], axis=1)
    rk = jnp.where(sel, rank, -1.0)
    rkt_ref[...] = rk.astype(BF16)
    rktok_ref[...] = rk.T.astype(BF16)
    for s in range(TM // TT):
        cnt_ref[s] = jnp.broadcast_to(
            jnp.sum(self[:, s * TT:(s + 1) * TT], axis=1, keepdims=True), (N_EXPERTS, LANES))
    comb_tok = comb.T
    hi = comb_tok.astype(BF16)
    lo = (comb_tok - hi.astype(F32)).astype(BF16)
    h2_ref[:, :D_MODEL] = h2.astype(BF16)
    h2_ref[:, D_MODEL:] = jnp.concatenate([hi, lo], axis=-1)


def _mix_out(x, gm, mla, mod_l, p):
    tiles_per_seq = SEQ // TM
    full = lambda shape: pl.BlockSpec(shape, lambda i: (0,) * len(shape))
    tok = lambda width: pl.BlockSpec((TM, width), lambda i: (i, 0))
    return pl.pallas_call(
        _mix_out_kernel,
        out_shape=(jax.ShapeDtypeStruct((N_TOK, D_MODEL), F32),
                   jax.ShapeDtypeStruct((N_TOK, H_AUG), BF16),
                   jax.ShapeDtypeStruct((N_EXPERTS, N_TOK), BF16),
                   jax.ShapeDtypeStruct((N_TOK, N_EXPERTS), BF16),
                   jax.ShapeDtypeStruct((N_TILES, N_EXPERTS, LANES), F32)),
        grid=(N_TOK // TM,),
        in_specs=[tok(D_MODEL), tok(GM_WIDTH), tok(MLA_WIDTH),
                  pl.BlockSpec((None, 6, D_MODEL), lambda i: (i // tiles_per_seq, 0, 0)),
                  full((1, MLA_WIDTH)), full((D_MODEL, D_MODEL)), full((1, D_MODEL)),
                  full((N_EXPERTS, D_MODEL)), full((N_EXPERTS, 1))],
        out_specs=(tok(D_MODEL), tok(H_AUG), pl.BlockSpec((N_EXPERTS, TM), lambda i: (0, i)),
                   tok(N_EXPERTS), pl.BlockSpec((TM // TT, N_EXPERTS, LANES), lambda i: (i, 0, 0))),
        compiler_params=_cparams(("arbitrary",)),
        name="mix_out_route",
    )(x, gm, mla, mod_l, p["out_norm_mla"], p["w_out"], p["norm_ffn_g"], p["w_router_t"],
      p["router_bias"])


def _moe_plan(cnt):
    i32 = jnp.int32
    cnt = cnt.astype(i32)
    padlen = (cnt + SEG_ALIGN - 1) // SEG_ALIGN * SEG_ALIGN
    lend = jnp.cumsum(padlen, axis=1)
    lstart = lend - padlen
    ltot = lend[:, -1]
    region = jnp.sum(padlen, axis=0)
    region_pad = (region + TM_FFN - 1) // TM_FFN * TM_FFN
    base = jnp.cumsum(region_pad) - region_pad
    pos = base[None, :] + jnp.cumsum(padlen, axis=0) - padlen
    gidx = jnp.arange(MAX_GROUPS, dtype=i32)
    grow = gidx * SEG_ALIGN
    in_seg = (grow[None, :, None] >= lstart[:, None, :]) & (grow[None, :, None] < lend[:, None, :])
    shift = jnp.sum(jnp.where(in_seg, (pos - lstart)[:, None, :], 0), axis=-1)
    n_groups = ltot // SEG_ALIGN
    valid_g = gidx[None, :] < n_groups[:, None]
    real_dst = (shift + grow[None, :]) // SEG_ALIGN
    trash = SORTED_ROWS // SEG_ALIGN + (jnp.arange(N_TILES, dtype=i32)[:, None] % 2) * GROUPS_PER_CHUNK \
        + gidx[None, :] % GROUPS_PER_CHUNK
    n_chunks = (n_groups + GROUPS_PER_CHUNK - 1) // GROUPS_PER_CHUNK
    tile_end = jnp.cumsum(region_pad) // TM_FFN
    n_used = tile_end[-1]
    tj = jnp.arange(FFN_TILES, dtype=i32)
    in_exp = (tj[:, None] * TM_FFN >= base[None, :]) & (tj[:, None] * TM_FFN < (base + region_pad)[None, :])
    t_exp = jnp.sum(jnp.where(in_exp, jnp.arange(N_EXPERTS, dtype=i32)[None, :], 0), axis=-1)
    t_valid = jnp.sum(jnp.where(in_exp, jnp.clip((base + region)[None, :] - tj[:, None] * TM_FFN, 0, TM_FFN), 0),
                      axis=-1)
    lstart_f = lstart.astype(F32)
    lend_f = lend.astype(F32)
    return {
        "n_chunks": n_chunks.astype(i32),
        "group_dst": jnp.where(valid_g, real_dst, trash).reshape(-1).astype(i32),
        "group_src": jnp.where(valid_g, real_dst, 0).reshape(-1).astype(i32),
        "seg_row": jnp.stack([lstart_f, lend_f], axis=1),
        "seg_col": jnp.stack([lstart_f, lend_f], axis=2),
        "n_used": n_used.reshape(1).astype(i32),
        "t_exp": t_exp.astype(i32),
        "t_valid": t_valid.astype(i32),
    }


def _dispatch_kernel(nc_ref, gd_ref, h_ref, rkt_ref, srow_ref, scol_ref, xs_ref, loc_ref, sem):
    i = pl.program_id(0)
    slot = i % 2

    def wait_chunks(n, slot_):
        def body(c, carry):
            pltpu.make_async_copy(loc_ref.at[slot_, pl.ds(0, CHUNK_ROWS), :],
                                  xs_ref.at[pl.ds(0, CHUNK_ROWS), :], sem.at[slot_]).wait()
            return carry
        lax.fori_loop(0, n, body, 0)

    @pl.when(i >= 2)
    def _():
        wait_chunks(nc_ref[i - 2], slot)

    ls_row = srow_ref[0:1, :]
    le_row = srow_ref[1:2, :]
    ls_rep = jnp.broadcast_to(scol_ref[:, 0:1], (N_EXPERTS, LANES)).astype(BF16)
    rk_ls = jnp.concatenate([rkt_ref[...], ls_rep], axis=1)
    h = h_ref[...]
    n_c = nc_ref[i]
    rio_e0 = lax.broadcasted_iota(jnp.int32, (CHUNK_ROWS, N_EXPERTS), 0).astype(F32)
    rio_t0 = lax.broadcasted_iota(jnp.int32, (CHUNK_ROWS, TT), 0).astype(F32)

    def select(r0f):
        rio_e = rio_e0 + r0f
        g = jnp.where((rio_e >= ls_row) & (rio_e < le_row), 1.0, 0.0).astype(BF16)
        both = jnp.dot(g, rk_ls, preferred_element_type=F32)
        want = (rio_t0 + r0f) - jnp.concatenate([both[:, TT:]] * (TT // LANES), axis=1)
        return jnp.where(both[:, :TT] == want, 1.0, 0.0).astype(BF16)

    def emit(r0, psel):
        loc_ref[slot, pl.ds(r0, CHUNK_ROWS), :] = jnp.dot(
            psel, h, preferred_element_type=F32).astype(BF16)

    psels = [select(float(c * CHUNK_ROWS)) for c in range(STATIC_CHUNKS)]
    for c in range(STATIC_CHUNKS):
        emit(c * CHUNK_ROWS, psels[c])

    def chunk(c, carry):
        emit(pl.multiple_of(c * CHUNK_ROWS, CHUNK_ROWS), select((c * CHUNK_ROWS).astype(F32)))
        return carry

    lax.fori_loop(STATIC_CHUNKS, n_c, chunk, 0)

    def issue(c, carry):
        for k in range(GROUPS_PER_CHUNK):
            src = pl.multiple_of(c * CHUNK_ROWS + k * SEG_ALIGN, SEG_ALIGN)
            dst = pl.multiple_of(gd_ref[i * MAX_GROUPS + c * GROUPS_PER_CHUNK + k] * SEG_ALIGN, SEG_ALIGN)
            pltpu.make_async_copy(loc_ref.at[slot, pl.ds(src, SEG_ALIGN), :],
                                  xs_ref.at[pl.ds(dst, SEG_ALIGN), :], sem.at[slot]).start()
        return carry

    lax.fori_loop(0, n_c, issue, 0)

    @pl.when(i == N_TILES - 1)
    def _():
        wait_chunks(nc_ref[i - 1], 1 - slot)
        wait_chunks(n_c, slot)


def _dispatch(plan, h2aug, rkt):
    grid_spec = pltpu.PrefetchScalarGridSpec(
        num_scalar_prefetch=2,
        grid=(N_TILES,),
        in_specs=[pl.BlockSpec((TT, H_AUG), lambda i, *_: (i, 0)),
                  pl.BlockSpec((N_EXPERTS, TT), lambda i, *_: (0, i)),
                  pl.BlockSpec((None, 2, N_EXPERTS), lambda i, *_: (i, 0, 0)),
                  pl.BlockSpec((None, N_EXPERTS, 2), lambda i, *_: (i, 0, 0))],
        out_specs=pl.BlockSpec(memory_space=pl.ANY),
        scratch_shapes=[pltpu.VMEM((2, LOCAL_ROWS, H_AUG), BF16), pltpu.SemaphoreType.DMA((2,))],
    )
    return pl.pallas_call(
        _dispatch_kernel,
        out_shape=jax.ShapeDtypeStruct((SORTED_ROWS + TRASH_ROWS, H_AUG), BF16),
        grid_spec=grid_spec,
        compiler_params=_cparams(("arbitrary",)),
        name="moe_dispatch",
    )(plan["n_chunks"], plan["group_dst"], h2aug, rkt, plan["seg_row"], plan["seg_col"])


def _ffn_kernel(nu_ref, te_ref, tv_ref, xs_ref, wg_ref, wu_ref, wdn_ref, y_ref,
                xbuf_ref, wgu_ref, wd_ref, sem):
    j = pl.program_id(0)
    n_used = nu_ref[0]

    def fetch(t):
        slot = t % FFN_RING
        return pltpu.make_async_copy(xs_ref.at[pl.ds(pl.multiple_of(t * TM_FFN, TM_FFN), TM_FFN), :],
                                     xbuf_ref.at[slot], sem.at[slot])

    @pl.when(j == 0)
    def _():
        fetch(0).start()

        @pl.when(n_used > 1)
        def _():
            fetch(1).start()

    @pl.when(j + 2 < n_used)
    def _():
        fetch(j + 2).start()

    @pl.when(j < n_used)
    def _():
        e = te_ref[j]
        fetch(j).wait()
        x_ref = xbuf_ref.at[j % FFN_RING]

        @pl.when((j == 0) | (e != te_ref[jnp.maximum(j - 1, 0)]))
        def _():
            wgu_ref[:, :EXPERT_FF] = wg_ref[...].astype(BF16)
            wgu_ref[:, EXPERT_FF:] = wu_ref[...].astype(BF16)
            wd_ref[...] = wdn_ref[...].astype(BF16)

        rows = lax.broadcasted_iota(jnp.int32, (TM_FFN, 1), 0)
        valid = rows < tv_ref[j]
        x = jnp.where(valid, x_ref[:, :D_MODEL], jnp.zeros((), BF16))
        lane = lax.broadcasted_iota(jnp.int32, (TM_FFN, LANES), 1)
        mine = ((lane == e) | (lane == e + N_EXPERTS)) & valid
        w = jnp.sum(jnp.where(mine, x_ref[:, D_MODEL:].astype(F32), 0.0), axis=-1, keepdims=True)
        hgu = jnp.dot(x, wgu_ref[...], preferred_element_type=F32)
        g = hgu[:, :EXPERT_FF]
        a = (g * jax.nn.sigmoid(g)) * hgu[:, EXPERT_FF:] * w
        y_ref[...] = jnp.dot(a.astype(BF16), wd_ref[...], preferred_element_type=F32).astype(BF16)


def _expert_ffn(plan, xs, p):
    layer = p["layer"]

    def row_tile(j, nu, te, tv):
        return (jnp.minimum(j, nu[0] - 1), 0)

    grid_spec = pltpu.PrefetchScalarGridSpec(
        num_scalar_prefetch=3,
        grid=(FFN_TILES,),
        in_specs=[pl.BlockSpec(memory_space=pl.ANY),
                  pl.BlockSpec((None, None, D_MODEL, EXPERT_FF), lambda j, nu, te, tv: (layer, te[j], 0, 0)),
                  pl.BlockSpec((None, None, D_MODEL, EXPERT_FF), lambda j, nu, te, tv: (layer, te[j], 0, 0)),
                  pl.BlockSpec((None, None, EXPERT_FF, D_MODEL), lambda j, nu, te, tv: (layer, te[j], 0, 0))],
        out_specs=pl.BlockSpec((TM_FFN, D_MODEL), row_tile),
        scratch_shapes=[pltpu.VMEM((FFN_RING, TM_FFN, H_AUG), BF16),
                        pltpu.VMEM((D_MODEL, 2 * EXPERT_FF), BF16),
                        pltpu.VMEM((EXPERT_FF, D_MODEL), BF16),
                        pltpu.SemaphoreType.DMA((FFN_RING,))],
    )
    return pl.pallas_call(
        _ffn_kernel,
        out_shape=jax.ShapeDtypeStruct((SORTED_ROWS, D_MODEL), BF16),
        grid_spec=grid_spec,
        compiler_params=_cparams(("arbitrary",)),
        name="moe_expert_ffn",
    )(plan["n_used"], plan["t_exp"], plan["t_valid"], xs, p["w_exp_gate"], p["w_exp_up"],
      p["w_exp_down"])


def _combine_kernel(nc_ref, gs_ref, y_ref, rktok_ref, srow_ref, scol_ref, h_ref, x1_ref, mod_ref,
                    sgu_ref, sd_ref, o_ref, yl_ref, acc_ref, sem):
    i = pl.program_id(0)
    slot = i % 2

    def gather(step, slot_):
        def body(c, carry):
            for k in range(GROUPS_PER_CHUNK):
                src = pl.multiple_of(
                    gs_ref[step * MAX_GROUPS + c * GROUPS_PER_CHUNK + k] * SEG_ALIGN, SEG_ALIGN)
                dst = pl.multiple_of(c * CHUNK_ROWS + k * SEG_ALIGN, SEG_ALIGN)
                pltpu.make_async_copy(y_ref.at[pl.ds(src, SEG_ALIGN), :],
                                      yl_ref.at[slot_, pl.ds(dst, SEG_ALIGN), :], sem.at[slot_]).start()
            return carry
        lax.fori_loop(0, nc_ref[step], body, 0)

    @pl.when(i == 0)
    def _():
        yl_ref[...] = jnp.zeros(yl_ref.shape, BF16)
        gather(0, 0)

    @pl.when(i + 1 < N_TILES)
    def _():
        gather(i + 1, 1 - slot)

    h = h_ref[...]
    hs = jnp.dot(h, sgu_ref[...], preferred_element_type=F32)
    gs = hs[:, :EXPERT_FF]
    a = (gs * jax.nn.sigmoid(gs)) * hs[:, EXPERT_FF:]
    shared = jnp.dot(a.astype(BF16), sd_ref[...], preferred_element_type=F32)

    n_c = nc_ref[i]

    def wait_body(c, carry):
        pltpu.make_async_copy(y_ref.at[pl.ds(0, CHUNK_ROWS), :],
                              yl_ref.at[slot, pl.ds(0, CHUNK_ROWS), :], sem.at[slot]).wait()
        return carry

    lax.fori_loop(0, n_c, wait_body, 0)

    ls_col = scol_ref[:, 0:1]
    le_col = scol_ref[:, 1:2]
    ls_row8 = jnp.broadcast_to(srow_ref[0:1, :], (8, N_EXPERTS)).astype(BF16)
    rk_ls = jnp.concatenate([rktok_ref[...], ls_row8], axis=0)
    rio_e0 = lax.broadcasted_iota(jnp.int32, (N_EXPERTS, CHUNK_ROWS), 1).astype(F32)
    rio_r0 = lax.broadcasted_iota(jnp.int32, (1, CHUNK_ROWS), 1).astype(F32)

    def select_t(r0f):
        rio_e = rio_e0 + r0f
        gt = jnp.where((rio_e >= ls_col) & (rio_e < le_col), 1.0, 0.0).astype(BF16)
        both = jnp.dot(rk_ls, gt, preferred_element_type=F32)
        want = (rio_r0 + r0f) - both[TT:TT + 1, :]
        return jnp.where(both[:TT, :] == want, 1.0, 0.0).astype(BF16)

    static_rows = STATIC_CHUNKS * CHUNK_ROWS
    pt_all = jnp.concatenate([select_t(float(c * CHUNK_ROWS)) for c in range(STATIC_CHUNKS)], axis=1)
    acc_ref[...] = shared + jnp.dot(pt_all, yl_ref[slot, pl.ds(0, static_rows), :],
                                    preferred_element_type=F32)

    def chunk(c, carry):
        r0 = pl.multiple_of(c * CHUNK_ROWS, CHUNK_ROWS)
        acc_ref[...] += jnp.dot(select_t((c * CHUNK_ROWS).astype(F32)),
                                yl_ref[slot, pl.ds(r0, CHUNK_ROWS), :], preferred_element_type=F32)
        return carry

    lax.fori_loop(STATIC_CHUNKS, n_c, chunk, 0)
    o_ref[...] = x1_ref[...] + mod_ref[5:6, :] * acc_ref[...]


def _combine(plan, y, rktok, h2aug, x1, mod_l, p):
    tiles_per_seq = SEQ // TT
    grid_spec = pltpu.PrefetchScalarGridSpec(
        num_scalar_prefetch=2,
        grid=(N_TILES,),
        in_specs=[pl.BlockSpec(memory_space=pl.ANY),
                  pl.BlockSpec((TT, N_EXPERTS), lambda i, *_: (i, 0)),
                  pl.BlockSpec((None, 2, N_EXPERTS), lambda i, *_: (i, 0, 0)),
                  pl.BlockSpec((None, N_EXPERTS, 2), lambda i, *_: (i, 0, 0)),
                  pl.BlockSpec((TT, D_MODEL), lambda i, *_: (i, 0)),
                  pl.BlockSpec((TT, D_MODEL), lambda i, *_: (i, 0)),
                  pl.BlockSpec((None, 6, D_MODEL), lambda i, *_: (i // tiles_per_seq, 0, 0)),
                  pl.BlockSpec((D_MODEL, 2 * EXPERT_FF), lambda i, *_: (0, 0)),
                  pl.BlockSpec((EXPERT_FF, D_MODEL), lambda i, *_: (0, 0))],
        out_specs=pl.BlockSpec((TT, D_MODEL), lambda i, *_: (i, 0)),
        scratch_shapes=[pltpu.VMEM((2, LOCAL_ROWS, D_MODEL), BF16),
                        pltpu.VMEM((TT, D_MODEL), F32),
                        pltpu.SemaphoreType.DMA((2,))],
    )
    return pl.pallas_call(
        _combine_kernel,
        out_shape=jax.ShapeDtypeStruct((N_TOK, D_MODEL), F32),
        grid_spec=grid_spec,
        compiler_params=_cparams(("arbitrary",)),
        name="moe_combine",
    )(plan["n_chunks"], plan["group_src"], y, rktok, plan["seg_row"], plan["seg_col"], h2aug, x1,
      mod_l, p["w_sh_gu"], p["w_sh_down"])


def _moe(h2aug, rkt, rktok, cnt, x1, mod_l, p):
    plan = _moe_plan(cnt[:, :, 0])
    xs = _dispatch(plan, h2aug, rkt)
    y = _expert_ffn(plan, xs, p)
    return _combine(plan, y, rktok, h2aug, x1, mod_l, p)


def _pad_heads(w, per_head):
    k = w.shape[0]
    w = w.reshape(k, MLA_HEADS, per_head)
    w = jnp.pad(w, ((0, 0), (0, 0), (0, HEAD_PAD - per_head)))
    return w.reshape(k, MLA_HEADS * HEAD_PAD)


def _layer_params(l, a):
    w_in = a["w_in"][l]
    pe_cols = jnp.zeros((D_MODEL, LANES), F32).at[:, QK_NOPE:QK_DIM].set(w_in[:, O_PE:])
    w_in_p = jnp.concatenate([w_in[:, :O_PE], pe_cols], axis=1).astype(BF16)
    w_kv = a["w_kv_up"][l].reshape(KV_LORA, MLA_HEADS, QK_NOPE + V_DIM)
    w_k = _pad_heads(w_kv[:, :, :QK_NOPE].reshape(KV_LORA, MLA_HEADS * QK_NOPE), QK_NOPE)
    w_v = w_kv[:, :, QK_NOPE:].reshape(KV_LORA, MLA_HEADS * V_DIM)
    blocks = TT // GM_BLOCK
    eye = jnp.eye(blocks, dtype=F32)
    ws_big = jnp.einsum("ab,gts->gatbs", eye, a["gm_ws"][l]).reshape(GM_GROUPS, TT, TT)
    bs_tile = jnp.tile(jnp.repeat(a["gm_bs"][l].T, GM_CH, axis=1), (TM // GM_BLOCK, 1))
    pad96 = lambda g: jnp.pad(g, (0, LANES - QK_DIM)).reshape(1, LANES)
    return {
        "norm_mix_g": a["norm_mix_g"][l].reshape(1, D_MODEL),
        "norm_ffn_g": a["norm_ffn_g"][l].reshape(1, D_MODEL),
        "w_in": w_in_p,
        "gm_ln_g": a["gm_ln_g"][l].reshape(1, GM_WIDTH),
        "gm_ln_b": a["gm_ln_b"][l].reshape(1, GM_WIDTH),
        "gm_ws": ws_big.astype(BF16),
        "gm_bs": bs_tile,
        "q_a_g": a["q_a_g"][l].reshape(1, Q_LORA),
        "w_q_up": _pad_heads(a["w_q_up"][l], QK_DIM).astype(BF16),
        "kv_a_g": a["kv_a_g"][l].reshape(1, KV_LORA),
        "w_k_up": w_k.astype(BF16),
        "w_v_t": jnp.pad(w_v.T.reshape(MLA_HEADS, V_DIM, KV_LORA), ((0, 0), (0, V_AUG - V_DIM), (0, 0))
                         ).reshape(MLA_HEADS * V_AUG, KV_LORA).astype(BF16),
        "v_ones": jnp.tile((jnp.arange(V_AUG) == V_DIM).astype(F32), MLA_HEADS).reshape(-1, 1),
        "q_norm_g": pad96(a["q_norm_g"][l]),
        "k_norm_g": pad96(a["k_norm_g"][l]),
        "out_norm_gm": a["out_norm_g"][l, :GM_WIDTH].reshape(1, GM_WIDTH),
        "out_norm_mla": a["out_norm_g"][l, GM_WIDTH:].reshape(1, MLA_WIDTH),
        "w_out": a["w_out"][l].astype(BF16),
        "w_router_t": a["w_router"][l].T,
        "router_bias": a["router_bias"][l].reshape(N_EXPERTS, 1),
        "layer": l,
        "w_exp_gate": a["w_exp_gate"],
        "w_exp_up": a["w_exp_up"],
        "w_exp_down": a["w_exp_down"],
        "w_sh_gu": jnp.concatenate([a["w_sh_gate"][l], a["w_sh_up"][l]], axis=-1).astype(BF16),
        "w_sh_down": a["w_sh_down"][l].astype(BF16),
    }


def kernel(x, c, positions, norm_mix_g, norm_ffn_g, w_ada, b_ada, w_in, gm_ln_g, gm_ln_b, gm_ws, gm_bs, q_a_g, w_q_up, kv_a_g, w_kv_up, q_norm_g, k_norm_g, out_norm_g, w_out, w_router, router_bias, w_exp_gate, w_exp_up, w_exp_down, w_sh_gate, w_sh_up, w_sh_down):
    a = dict(norm_mix_g=norm_mix_g, norm_ffn_g=norm_ffn_g, w_in=w_in, gm_ln_g=gm_ln_g,
             gm_ln_b=gm_ln_b, gm_ws=gm_ws, gm_bs=gm_bs, q_a_g=q_a_g, w_q_up=w_q_up,
             kv_a_g=kv_a_g, w_kv_up=w_kv_up, q_norm_g=q_norm_g, k_norm_g=k_norm_g,
             out_norm_g=out_norm_g, w_out=w_out, w_router=w_router, router_bias=router_bias,
             w_exp_gate=w_exp_gate, w_exp_up=w_exp_up, w_exp_down=w_exp_down,
             w_sh_gate=w_sh_gate, w_sh_up=w_sh_up, w_sh_down=w_sh_down)
    cos_t, sin_t = _rope_tables(positions)
    mod = _ada_mod(c, w_ada, b_ada).reshape(DEPTH, BATCH, 6, D_MODEL)
    xt = x.reshape(N_TOK, D_MODEL)
    for l in range(DEPTH):
        p = _layer_params(l, a)
        gm, q, k, v = _mix_in(xt, mod[l], p, cos_t, sin_t)
        mla = _attention(q, k, v).reshape(N_TOK, MLA_WIDTH)
        x1, h2aug, rkt, rktok, cnt = _mix_out(xt, gm, mla, mod[l], p)
        xt = _moe(h2aug, rkt, rktok, cnt, x1, mod[l], p)
    return xt.reshape(BATCH, SEQ, D_MODEL)
```

```python
import functools

import jax
import jax.numpy as jnp
from jax import lax
from jax.experimental import pallas as pl
from jax.experimental.pallas import tpu as pltpu

D_MODEL = 1024
BATCH = 8
SEQ = 4096
DEPTH = 2
N_TOK = BATCH * SEQ
CHUNK = 64
GM_WIDTH = 512
GM_BLOCK = 128
GM_GROUPS = 4
GM_CH = 128
MLA_HEADS = 8
QK_NOPE = 64
QK_ROPE = 32
QK_DIM = 96
V_DIM = 64
MLA_WIDTH = 512
Q_LORA = 384
KV_LORA = 256
ROPE_THETA = 10000.0
N_EXPERTS = 64
TOP_K = 8
N_GROUPS = 8
GROUP_SIZE = N_EXPERTS // N_GROUPS
TOPK_GROUPS = 4
EXPERT_FF = 256
ROUTED_SCALE = 2.5
EPS = 1e-6

LANES = 128
HEAD_PAD = LANES
IN_COLS_PAD = 2 * GM_WIDTH + Q_LORA + KV_LORA + LANES
O_Q = 2 * GM_WIDTH
O_KV = O_Q + Q_LORA
O_PE = O_KV + KV_LORA

TT = 256
TM = 512
TQ = 512
TK = 256
ATT_HEADS = 8
V_AUG = 80
N_TILES = N_TOK // TT
H_AUG = D_MODEL + LANES
SEG_ALIGN = 16
CHUNK_ROWS = 256
LOCAL_ROWS = TT * TOP_K + N_EXPERTS * SEG_ALIGN
MAX_GROUPS = LOCAL_ROWS // SEG_ALIGN
TM_FFN = 512
FFN_RING = 3
SORTED_ROWS = -(-(N_TOK * TOP_K + N_TILES * N_EXPERTS * (SEG_ALIGN - 1)
                  + N_EXPERTS * (TM_FFN - 1)) // TM_FFN) * TM_FFN
FFN_TILES = SORTED_ROWS // TM_FFN
GROUPS_PER_CHUNK = CHUNK_ROWS // SEG_ALIGN
TRASH_ROWS = 2 * CHUNK_ROWS
STATIC_CHUNKS = 10
VMEM_LIMIT = 56 * 1024 * 1024

F32 = jnp.float32
BF16 = jnp.bfloat16


def _cparams(sem, flags=None):
    return pltpu.CompilerParams(dimension_semantics=sem, vmem_limit_bytes=VMEM_LIMIT, flags=flags)


def _rms_scale(v, width):
    return lax.rsqrt(jnp.sum(v * v, axis=-1, keepdims=True) * (1.0 / width) + EPS)


def _rope_kernel(pos_ref, inv_ref, cos_ref, sin_ref):
    ang = pos_ref[...].astype(F32) * inv_ref[...]
    lane = lax.broadcasted_iota(jnp.int32, ang.shape, 1)
    c = jnp.cos(ang)
    s = jnp.sin(ang)
    cos_ref[...] = jnp.where(lane < QK_NOPE, 1.0, jnp.where(lane < QK_DIM, c, 0.0))
    sin_ref[...] = jnp.where((lane >= QK_NOPE) & (lane < QK_NOPE + 16), -s,
                             jnp.where((lane >= QK_NOPE + 16) & (lane < QK_DIM), s, 0.0))


def _rope_tables(positions):
    inv_freq = 1.0 / (ROPE_THETA ** (jnp.arange(0, QK_ROPE, 2, dtype=F32) / QK_ROPE))
    inv = jnp.zeros((1, LANES), F32)
    inv = inv.at[0, QK_NOPE:QK_NOPE + 16].set(inv_freq)
    inv = inv.at[0, QK_NOPE + 16:QK_DIM].set(inv_freq)
    pos = positions.reshape(N_TOK, 1)
    tile = 2048
    return pl.pallas_call(
        _rope_kernel,
        out_shape=(jax.ShapeDtypeStruct((N_TOK, LANES), F32),
                   jax.ShapeDtypeStruct((N_TOK, LANES), F32)),
        grid=(N_TOK // tile,),
        in_specs=[pl.BlockSpec((tile, 1), lambda i: (i, 0)),
                  pl.BlockSpec((1, LANES), lambda i: (0, 0))],
        out_specs=(pl.BlockSpec((tile, LANES), lambda i: (i, 0)),
                   pl.BlockSpec((tile, LANES), lambda i: (i, 0))),
        compiler_params=_cparams(("arbitrary",)),
        name="rope_tables",
    )(pos, inv)


def _ada_kernel(c_ref, w_ref, b_ref, o_ref):
    c = c_ref[...]
    ca = c * jax.nn.sigmoid(c)
    o_ref[...] = jnp.dot(ca, w_ref[...], precision=lax.Precision.HIGHEST,
                         preferred_element_type=F32) + b_ref[...]


def _ada_mod(c, w_ada, b_ada):
    return pl.pallas_call(
        _ada_kernel,
        out_shape=jax.ShapeDtypeStruct((DEPTH, BATCH, 6 * D_MODEL), F32),
        grid=(DEPTH, 6),
        in_specs=[pl.BlockSpec((BATCH, D_MODEL), lambda l, j: (0, 0)),
                  pl.BlockSpec((None, D_MODEL, D_MODEL), lambda l, j: (l, 0, j)),
                  pl.BlockSpec((None, 1, D_MODEL), lambda l, j: (l, 0, j))],
        out_specs=pl.BlockSpec((None, BATCH, D_MODEL), lambda l, j: (l, 0, j)),
        compiler_params=_cparams(("arbitrary", "arbitrary")),
        name="ada_mod",
    )(c, w_ada, b_ada.reshape(DEPTH, 1, 6 * D_MODEL))


def _rope_rot(t, cos_t, sin_t, lane):
    swapped = jnp.where(lane < QK_NOPE + 16, pltpu.roll(t, LANES - 16, 1), pltpu.roll(t, 16, 1))
    return t * cos_t + swapped * sin_t


def _mix_in_kernel(x_ref, mod_ref, ng_ref, win_ref, lng_ref, lnb_ref, ws_ref, bs_ref,
                   qag_ref, wq_ref, kvg_ref, wkv_ref, wvt_ref, vone_ref, qng_ref, kng_ref, og_ref,
                   cos_ref, sin_ref,
                   gm_ref, q_ref, k_ref, vt_ref):
    x = x_ref[...]
    sh1 = mod_ref[0:1, :]
    sc1 = mod_ref[1:2, :]
    h = x * _rms_scale(x, D_MODEL) * ng_ref[...]
    h = h * (1.0 + sc1) + sh1
    proj = jnp.dot(h.astype(BF16), win_ref[...], preferred_element_type=F32)

    uv_pre = proj[:, :2 * GM_WIDTH]
    uv = 0.5 * uv_pre * (1.0 + lax.erf(uv_pre * (2.0 ** -0.5)))
    row = lax.broadcasted_iota(jnp.int32, (TT, TT), 0)
    col = lax.broadcasted_iota(jnp.int32, (TT, TT), 1)
    causal = (col <= row) & ((col // GM_BLOCK) == (row // GM_BLOCK))
    groups = range(GM_GROUPS)
    gsl = [slice(g * GM_CH, (g + 1) * GM_CH) for g in groups]
    vg = [uv[:, GM_WIDTH + g * GM_CH:GM_WIDTH + (g + 1) * GM_CH] for g in groups]
    mu = [jnp.mean(vg[g], axis=-1, keepdims=True) for g in groups]
    xc = [vg[g] - mu[g] for g in groups]
    var = [jnp.mean(xc[g] * xc[g], axis=-1, keepdims=True) for g in groups]
    vln = [(xc[g] * lax.rsqrt(var[g] + EPS) * lng_ref[:, gsl[g]] + lnb_ref[:, gsl[g]]).astype(BF16)
           for g in groups]
    ws = [jnp.where(causal, ws_ref[g], jnp.zeros((), BF16)) for g in groups]
    gate = [jnp.concatenate([jnp.dot(ws[g], vln[g][s * TT:(s + 1) * TT], preferred_element_type=F32)
                             for s in range(TM // TT)], axis=0) for g in groups]
    outs = [uv[:, gsl[g]] * (gate[g] + bs_ref[:, gsl[g]]) for g in groups]
    sq = [jnp.sum(outs[g] * outs[g], axis=-1, keepdims=True) for g in groups]
    rs = lax.rsqrt((sq[0] + sq[1] + sq[2] + sq[3]) * (1.0 / GM_WIDTH) + EPS)
    for g in groups:
        gm_ref[:, gsl[g]] = (outs[g] * rs * og_ref[:, gsl[g]]).astype(BF16)

    cos_t = cos_ref[...]
    sin_t = sin_ref[...]
    lane = lax.broadcasted_iota(jnp.int32, (TM, LANES), 1)
    ql = proj[:, O_Q:O_KV]
    qn = ql * _rms_scale(ql, Q_LORA) * qag_ref[...]
    qa = jnp.dot(qn.astype(BF16), wq_ref[...], preferred_element_type=F32)
    kl = proj[:, O_KV:O_PE]
    kn = kl * _rms_scale(kl, KV_LORA) * kvg_ref[...]
    kva = jnp.dot(kn.astype(BF16), wkv_ref[...], preferred_element_type=F32)
    pe = proj[:, O_PE:O_PE + LANES]
    pe_ssq = jnp.sum(pe * pe, axis=-1, keepdims=True)
    pe_rot = _rope_rot(pe * kng_ref[...], cos_t, sin_t, lane)
    qscale = (QK_DIM ** -0.5) * 1.4426950408889634
    v_t = lax.dot_general(wvt_ref[...], kn.astype(BF16), (((1,), (1,)), ((), ())),
                          preferred_element_type=F32) + vone_ref[...]
    vt_ref[...] = v_t.reshape(MLA_HEADS, V_AUG, TM).astype(BF16)
    heads = range(MLA_HEADS)
    qh = [qa[:, hd * HEAD_PAD:(hd + 1) * HEAD_PAD] for hd in heads]
    kh = [kva[:, hd * HEAD_PAD:(hd + 1) * HEAD_PAD] for hd in heads]
    q_ssq = [jnp.sum(qh[hd] * qh[hd], axis=-1, keepdims=True) for hd in heads]
    k_ssq = [jnp.sum(kh[hd] * kh[hd], axis=-1, keepdims=True) for hd in heads]
    qg = [qh[hd] * lax.rsqrt(q_ssq[hd] * (1.0 / QK_DIM) + EPS) * qng_ref[...] for hd in heads]
    q_lo = [pltpu.roll(qg[hd], LANES - 16, 1) for hd in heads]
    q_hi = [pltpu.roll(qg[hd], 16, 1) for hd in heads]
    first_half = lane < QK_NOPE + 16
    for hd in heads:
        rot = qg[hd] * cos_t + jnp.where(first_half, q_lo[hd], q_hi[hd]) * sin_t
        q_ref[hd] = (rot * qscale).astype(BF16)
    for hd in heads:
        krs = lax.rsqrt((k_ssq[hd] + pe_ssq) * (1.0 / QK_DIM) + EPS)
        k_ref[hd] = ((kh[hd] * kng_ref[...] + pe_rot) * krs).astype(BF16)


def _mix_in(x, mod_l, p, cos_t, sin_t):
    tiles_per_seq = SEQ // TM
    full = lambda shape: pl.BlockSpec(shape, lambda i: (0,) * len(shape))
    head_spec = pl.BlockSpec((None, MLA_HEADS, TM, HEAD_PAD),
                             lambda i: (i // tiles_per_seq, 0, i % tiles_per_seq, 0))
    head_shape = jax.ShapeDtypeStruct((BATCH, MLA_HEADS, SEQ, HEAD_PAD), BF16)
    return pl.pallas_call(
        _mix_in_kernel,
        out_shape=(jax.ShapeDtypeStruct((N_TOK, GM_WIDTH), BF16), head_shape, head_shape,
                   jax.ShapeDtypeStruct((BATCH, MLA_HEADS, V_AUG, SEQ), BF16)),
        grid=(N_TOK // TM,),
        in_specs=[pl.BlockSpec((TM, D_MODEL), lambda i: (i, 0)),
                  pl.BlockSpec((None, 6, D_MODEL), lambda i: (i // tiles_per_seq, 0, 0)),
                  full((1, D_MODEL)),
                  full((D_MODEL, IN_COLS_PAD)),
                  full((1, GM_WIDTH)), full((1, GM_WIDTH)),
                  full((GM_GROUPS, TT, TT)), full((TM, GM_WIDTH)),
                  full((1, Q_LORA)), full((Q_LORA, MLA_HEADS * HEAD_PAD)),
                  full((1, KV_LORA)), full((KV_LORA, MLA_HEADS * HEAD_PAD)),
                  full((MLA_HEADS * V_AUG, KV_LORA)), full((MLA_HEADS * V_AUG, 1)),
                  full((1, LANES)), full((1, LANES)), full((1, GM_WIDTH)),
                  pl.BlockSpec((TM, LANES), lambda i: (i, 0)),
                  pl.BlockSpec((TM, LANES), lambda i: (i, 0))],
        out_specs=(pl.BlockSpec((TM, GM_WIDTH), lambda i: (i, 0)), head_spec, head_spec,
                   pl.BlockSpec((None, MLA_HEADS, V_AUG, TM),
                                lambda i: (i // tiles_per_seq, 0, 0, i % tiles_per_seq))),
        compiler_params=_cparams(("arbitrary",)),
        name="mix_in",
    )(x, mod_l, p["norm_mix_g"], p["w_in"], p["gm_ln_g"], p["gm_ln_b"], p["gm_ws"], p["gm_bs"],
      p["q_a_g"], p["w_q_up"], p["kv_a_g"], p["w_k_up"], p["w_v_t"], p["v_ones"], p["q_norm_g"],
      p["k_norm_g"],
      p["out_norm_gm"], cos_t, sin_t)


def _attn_kernel(q_ref, k_ref, vt_ref, o_ref):
    qi = pl.program_id(2)
    krow = lax.broadcasted_iota(jnp.int32, (TK, TQ), 0)
    qcol = lax.broadcasted_iota(jnp.int32, (TK, TQ), 1)
    nt = (((1,), (1,)), ((), ()))
    per_q = TQ // TK

    def tile(j, carry, diag):
        start = pl.multiple_of(j * TK, TK)
        heads = range(ATT_HEADS)
        s = [lax.dot_general(k_ref[hh, pl.ds(start, TK), :], q_ref[hh], nt,
                             preferred_element_type=F32) for hh in heads]
        if diag is not None:
            allowed = ((krow + diag * TK) // CHUNK) <= (qcol // CHUNK)
            s = [jnp.where(allowed, sh, -jnp.inf) for sh in s]
        m_new = [jnp.maximum(carry[hh][0], jnp.max(s[hh], axis=0, keepdims=True)) for hh in heads]
        p = [jnp.exp2(s[hh] - m_new[hh]) for hh in heads]
        alpha = [jnp.exp2(carry[hh][0] - m_new[hh]) for hh in heads]
        pv = [jnp.dot(vt_ref[hh, :, pl.ds(start, TK)], p[hh].astype(BF16),
                      preferred_element_type=F32) for hh in heads]
        return tuple((m_new[hh], alpha[hh] * carry[hh][1] + pv[hh]) for hh in heads)

    init = tuple((jnp.full((1, TQ), -jnp.inf, F32), jnp.zeros((V_AUG, TQ), F32))
                 for _ in range(ATT_HEADS))
    carry = lax.fori_loop(0, qi * per_q, functools.partial(tile, diag=None), init)
    for t in range(per_q):
        carry = tile(qi * per_q + t, carry, t)
    out_t = jnp.concatenate([acc[:V_DIM] / acc[V_DIM:V_DIM + 1] for (_, acc) in carry], axis=0)
    o_ref[...] = out_t.T


def _attention(q, k, vt):
    groups = MLA_HEADS // ATT_HEADS
    qspec = pl.BlockSpec((None, ATT_HEADS, TQ, HEAD_PAD), lambda b, hp, i: (b, hp, i, 0))
    kspec = pl.BlockSpec((None, ATT_HEADS, SEQ, HEAD_PAD), lambda b, hp, i: (b, hp, 0, 0))
    vspec = pl.BlockSpec((None, ATT_HEADS, V_AUG, SEQ), lambda b, hp, i: (b, hp, 0, 0))
    return pl.pallas_call(
        _attn_kernel,
        out_shape=jax.ShapeDtypeStruct((BATCH, SEQ, MLA_WIDTH), F32),
        grid=(BATCH, groups, SEQ // TQ),
        in_specs=[qspec, kspec, vspec],
        out_specs=pl.BlockSpec((None, TQ, ATT_HEADS * V_DIM), lambda b, hp, i: (b, i, hp)),
        compiler_params=_cparams(("arbitrary", "arbitrary", "arbitrary")),
        name="mla_attention",
    )(q, k, vt)


def _route(logits_t, bias_col):
    t = logits_t.shape[1]
    scores = jax.nn.sigmoid(logits_t)
    biased = scores + bias_col
    neg = -jnp.inf
    b3 = biased.reshape(N_GROUPS, GROUP_SIZE, t)
    m1 = jnp.max(b3, axis=1, keepdims=True)
    n_max = jnp.sum((b3 == m1).astype(F32), axis=1, keepdims=True)
    m2 = jnp.max(jnp.where(b3 < m1, b3, neg), axis=1, keepdims=True)
    gs = (m1 + jnp.where(n_max >= 2.0, m1, m2)).reshape(N_GROUPS, t)
    gidx = lax.broadcasted_iota(jnp.int32, (N_GROUPS, t), 0)
    grank = jnp.zeros((N_GROUPS, t), F32)
    for g in range(N_GROUPS):
        other = gs[g:g + 1, :]
        ahead = (other > gs) | ((other == gs) & (gidx > g))
        grank = grank + ahead.astype(F32)
    gsel = grank < float(TOPK_GROUPS)
    emask = jnp.broadcast_to(gsel.reshape(N_GROUPS, 1, t), (N_GROUPS, GROUP_SIZE, t)).reshape(N_EXPERTS, t)
    ms = jnp.where(emask, biased, neg)
    eidx = lax.broadcasted_iota(jnp.int32, (N_EXPERTS, t), 0).astype(F32)
    picked = jnp.zeros((N_EXPERTS, t), F32)
    for _ in range(TOP_K):
        top = jnp.max(ms, axis=0, keepdims=True)
        first = jnp.min(jnp.where(ms == top, eidx, float(N_EXPERTS)), axis=0, keepdims=True)
        hit = eidx == first
        picked = jnp.where(hit, 1.0, picked)
        ms = jnp.where(hit, neg, ms)
    sel = picked > 0.5
    w = jnp.where(sel, scores, 0.0)
    denom = jnp.sum(w, axis=0, keepdims=True)
    return (w / denom) * ROUTED_SCALE, sel


def _mix_out_kernel(x_ref, gm_ref, mla_ref, mod_ref, og_ref, wout_ref, fg_ref, wr_ref, rb_ref,
                    x1_ref, h2_ref, rkt_ref, rktok_ref, cnt_ref):
    mla = mla_ref[...]
    mla_n = mla * _rms_scale(mla, MLA_WIDTH) * og_ref[...]
    mixed = jnp.concatenate([gm_ref[...], mla_n.astype(BF16)], axis=-1)
    y = jnp.dot(mixed, wout_ref[...], preferred_element_type=F32)
    g1 = mod_ref[2:3, :]
    sh2 = mod_ref[3:4, :]
    sc2 = mod_ref[4:5, :]
    x1 = x_ref[...] + g1 * y
    x1_ref[...] = x1
    h2 = x1 * _rms_scale(x1, D_MODEL) * fg_ref[...]
    h2 = h2 * (1.0 + sc2) + sh2
    logits_t = lax.dot_general(wr_ref[...], h2, (((1,), (1,)), ((), ())),
                               precision=lax.Precision.HIGHEST, preferred_element_type=F32)
    comb, sel = _route(logits_t, rb_ref[...])
    before = (lax.broadcasted_iota(jnp.int32, (TT, TT), 0)
              < lax.broadcasted_iota(jnp.int32, (TT, TT), 1))
    self = jnp.where(sel, 1.0, 0.0)
    before_b = jnp.where(before, 1.0, 0.0).astype(BF16)
    rank = jnp.concatenate(
        [jnp.dot(self[:, s * TT:(s + 1) * TT].astype(BF16), before_b, preferred_element_type=F32)
         for s in range(TM // TT)], axis=1)
    rk = jnp.where(sel, rank, -1.0)
    rkt_ref[...] = rk.astype(BF16)
    rktok_ref[...] = rk.T.astype(BF16)
    for s in range(TM // TT):
        cnt_ref[s] = jnp.broadcast_to(
            jnp.sum(self[:, s * TT:(s + 1) * TT], axis=1, keepdims=True), (N_EXPERTS, LANES))
    comb_tok = comb.T
    hi = comb_tok.astype(BF16)
    lo = (comb_tok - hi.astype(F32)).astype(BF16)
    h2_ref[:, :D_MODEL] = h2.astype(BF16)
    h2_ref[:, D_MODEL:] = jnp.concatenate([hi, lo], axis=-1)


def _mix_out(x, gm, mla, mod_l, p):
    tiles_per_seq = SEQ // TM
    full = lambda shape: pl.BlockSpec(shape, lambda i: (0,) * len(shape))
    tok = lambda width: pl.BlockSpec((TM, width), lambda i: (i, 0))
    return pl.pallas_call(
        _mix_out_kernel,
        out_shape=(jax.ShapeDtypeStruct((N_TOK, D_MODEL), F32),
                   jax.ShapeDtypeStruct((N_TOK, H_AUG), BF16),
                   jax.ShapeDtypeStruct((N_EXPERTS, N_TOK), BF16),
                   jax.ShapeDtypeStruct((N_TOK, N_EXPERTS), BF16),
                   jax.ShapeDtypeStruct((N_TILES, N_EXPERTS, LANES), F32)),
        grid=(N_TOK // TM,),
        in_specs=[tok(D_MODEL), tok(GM_WIDTH), tok(MLA_WIDTH),
                  pl.BlockSpec((None, 6, D_MODEL), lambda i: (i // tiles_per_seq, 0, 0)),
                  full((1, MLA_WIDTH)), full((D_MODEL, D_MODEL)), full((1, D_MODEL)),
                  full((N_EXPERTS, D_MODEL)), full((N_EXPERTS, 1))],
        out_specs=(tok(D_MODEL), tok(H_AUG), pl.BlockSpec((N_EXPERTS, TM), lambda i: (0, i)),
                   tok(N_EXPERTS), pl.BlockSpec((TM // TT, N_EXPERTS, LANES), lambda i: (i, 0, 0))),
        compiler_params=_cparams(("arbitrary",)),
        name="mix_out_route",
    )(x, gm, mla, mod_l, p["out_norm_mla"], p["w_out"], p["norm_ffn_g"], p["w_router_t"],
      p["router_bias"])


def _moe_plan(cnt):
    i32 = jnp.int32
    cnt = cnt.astype(i32)
    padlen = (cnt + SEG_ALIGN - 1) // SEG_ALIGN * SEG_ALIGN
    lend = jnp.cumsum(padlen, axis=1)
    lstart = lend - padlen
    ltot = lend[:, -1]
    region = jnp.sum(padlen, axis=0)
    region_pad = (region + TM_FFN - 1) // TM_FFN * TM_FFN
    base = jnp.cumsum(region_pad) - region_pad
    pos = base[None, :] + jnp.cumsum(padlen, axis=0) - padlen
    gidx = jnp.arange(MAX_GROUPS, dtype=i32)
    grow = gidx * SEG_ALIGN
    in_seg = (grow[None, :, None] >= lstart[:, None, :]) & (grow[None, :, None] < lend[:, None, :])
    shift = jnp.sum(jnp.where(in_seg, (pos - lstart)[:, None, :], 0), axis=-1)
    n_groups = ltot // SEG_ALIGN
    valid_g = gidx[None, :] < n_groups[:, None]
    real_dst = (shift + grow[None, :]) // SEG_ALIGN
    trash = SORTED_ROWS // SEG_ALIGN + (jnp.arange(N_TILES, dtype=i32)[:, None] % 2) * GROUPS_PER_CHUNK \
        + gidx[None, :] % GROUPS_PER_CHUNK
    n_chunks = (n_groups + GROUPS_PER_CHUNK - 1) // GROUPS_PER_CHUNK
    tile_end = jnp.cumsum(region_pad) // TM_FFN
    n_used = tile_end[-1]
    tj = jnp.arange(FFN_TILES, dtype=i32)
    in_exp = (tj[:, None] * TM_FFN >= base[None, :]) & (tj[:, None] * TM_FFN < (base + region_pad)[None, :])
    t_exp = jnp.sum(jnp.where(in_exp, jnp.arange(N_EXPERTS, dtype=i32)[None, :], 0), axis=-1)
    t_valid = jnp.sum(jnp.where(in_exp, jnp.clip((base + region)[None, :] - tj[:, None] * TM_FFN, 0, TM_FFN), 0),
                      axis=-1)
    has_rows = region_pad > 0
    run_of_exp = jnp.cumsum(has_rows.astype(i32)) - 1
    eidx = jnp.arange(N_EXPERTS, dtype=i32)
    t_run = jnp.sum(jnp.where(in_exp, run_of_exp[None, :], 0), axis=-1)
    run_exp = jnp.sum(jnp.where(has_rows[None, :] & (run_of_exp[None, :] == eidx[:, None]), eidx[None, :], 0),
                      axis=-1)
    lstart_f = lstart.astype(F32)
    lend_f = lend.astype(F32)
    return {
        "t_run": t_run.astype(i32),
        "run_exp": run_exp.astype(i32),
        "n_runs": jnp.sum(has_rows.astype(i32)).reshape(1),
        "n_chunks": n_chunks.astype(i32),
        "group_dst": jnp.where(valid_g, real_dst, trash).reshape(-1).astype(i32),
        "group_src": jnp.where(valid_g, real_dst, 0).reshape(-1).astype(i32),
        "seg_row": jnp.stack([lstart_f, lend_f], axis=1),
        "seg_col": jnp.stack([lstart_f, lend_f], axis=2),
        "n_used": n_used.reshape(1).astype(i32),
        "t_exp": t_exp.astype(i32),
        "t_valid": t_valid.astype(i32),
    }


def _dispatch_kernel(nc_ref, gd_ref, h_ref, rkt_ref, srow_ref, scol_ref, xs_ref, loc_ref, sem):
    i = pl.program_id(0)
    slot = i % 2

    def wait_chunks(n, slot_):
        def body(c, carry):
            pltpu.make_async_copy(loc_ref.at[slot_, pl.ds(0, CHUNK_ROWS), :],
                                  xs_ref.at[pl.ds(0, CHUNK_ROWS), :], sem.at[slot_]).wait()
            return carry
        lax.fori_loop(0, n, body, 0)

    @pl.when(i >= 2)
    def _():
        wait_chunks(nc_ref[i - 2], slot)

    ls_row = srow_ref[0:1, :]
    le_row = srow_ref[1:2, :]
    ls_rep = jnp.broadcast_to(scol_ref[:, 0:1], (N_EXPERTS, LANES)).astype(BF16)
    rk_ls = jnp.concatenate([rkt_ref[...], ls_rep], axis=1)
    h = h_ref[...]
    n_c = nc_ref[i]
    rio_e0 = lax.broadcasted_iota(jnp.int32, (CHUNK_ROWS, N_EXPERTS), 0).astype(F32)
    rio_t0 = lax.broadcasted_iota(jnp.int32, (CHUNK_ROWS, TT), 0).astype(F32)

    def select(r0f):
        rio_e = rio_e0 + r0f
        g = jnp.where((rio_e >= ls_row) & (rio_e < le_row), 1.0, 0.0).astype(BF16)
        both = jnp.dot(g, rk_ls, preferred_element_type=F32)
        want = (rio_t0 + r0f) - jnp.concatenate([both[:, TT:]] * (TT // LANES), axis=1)
        return jnp.where(both[:, :TT] == want, 1.0, 0.0).astype(BF16)

    def emit(r0, psel):
        loc_ref[slot, pl.ds(r0, CHUNK_ROWS), :] = jnp.dot(
            psel, h, preferred_element_type=F32).astype(BF16)

    psels = [select(float(c * CHUNK_ROWS)) for c in range(STATIC_CHUNKS)]
    for c in range(STATIC_CHUNKS):
        emit(c * CHUNK_ROWS, psels[c])

    def chunk(c, carry):
        emit(pl.multiple_of(c * CHUNK_ROWS, CHUNK_ROWS), select((c * CHUNK_ROWS).astype(F32)))
        return carry

    lax.fori_loop(STATIC_CHUNKS, n_c, chunk, 0)

    def issue(c, carry):
        for k in range(GROUPS_PER_CHUNK):
            src = pl.multiple_of(c * CHUNK_ROWS + k * SEG_ALIGN, SEG_ALIGN)
            dst = pl.multiple_of(gd_ref[i * MAX_GROUPS + c * GROUPS_PER_CHUNK + k] * SEG_ALIGN, SEG_ALIGN)
            pltpu.make_async_copy(loc_ref.at[slot, pl.ds(src, SEG_ALIGN), :],
                                  xs_ref.at[pl.ds(dst, SEG_ALIGN), :], sem.at[slot]).start()
        return carry

    lax.fori_loop(0, n_c, issue, 0)

    @pl.when(i == N_TILES - 1)
    def _():
        wait_chunks(nc_ref[i - 1], 1 - slot)
        wait_chunks(n_c, slot)


def _dispatch(plan, h2aug, rkt):
    grid_spec = pltpu.PrefetchScalarGridSpec(
        num_scalar_prefetch=2,
        grid=(N_TILES,),
        in_specs=[pl.BlockSpec((TT, H_AUG), lambda i, *_: (i, 0)),
                  pl.BlockSpec((N_EXPERTS, TT), lambda i, *_: (0, i)),
                  pl.BlockSpec((None, 2, N_EXPERTS), lambda i, *_: (i, 0, 0)),
                  pl.BlockSpec((None, N_EXPERTS, 2), lambda i, *_: (i, 0, 0))],
        out_specs=pl.BlockSpec(memory_space=pl.ANY),
        scratch_shapes=[pltpu.VMEM((2, LOCAL_ROWS, H_AUG), BF16), pltpu.SemaphoreType.DMA((2,))],
    )
    return pl.pallas_call(
        _dispatch_kernel,
        out_shape=jax.ShapeDtypeStruct((SORTED_ROWS + TRASH_ROWS, H_AUG), BF16),
        grid_spec=grid_spec,
        compiler_params=_cparams(("arbitrary",)),
        name="moe_dispatch",
    )(plan["n_chunks"], plan["group_dst"], h2aug, rkt, plan["seg_row"], plan["seg_col"])


def _ffn_kernel(nu_ref, te_ref, tv_ref, tr_ref, re_ref, nr_ref,
                xs_ref, wg_hbm, wu_hbm, wd_hbm, y_hbm,
                xbuf_ref, ybuf_ref, wg_raw, wu_raw, wd_raw, wgu_ref, wd_ref, xsem, ysem, wsem,
                *, layer):
    n_used = nu_ref[0]
    n_runs = nr_ref[0]

    def rows_of(t):
        return pl.ds(pl.multiple_of(t * TM_FFN, TM_FFN), TM_FFN)

    def x_copy(t):
        slot = t % FFN_RING
        return pltpu.make_async_copy(xs_ref.at[rows_of(t), :], xbuf_ref.at[slot], xsem.at[slot])

    def y_copy(t):
        slot = t % 2
        return pltpu.make_async_copy(ybuf_ref.at[slot], y_hbm.at[rows_of(t), :], ysem.at[slot])

    def w_copies(r):
        slot = r % 2
        e = re_ref[r]
        return (pltpu.make_async_copy(wg_hbm.at[layer, e], wg_raw.at[slot], wsem.at[slot]),
                pltpu.make_async_copy(wu_hbm.at[layer, e], wu_raw.at[slot], wsem.at[slot]),
                pltpu.make_async_copy(wd_hbm.at[layer, e], wd_raw.at[slot], wsem.at[slot]))

    x_copy(0).start()

    @pl.when(n_used > 1)
    def _():
        x_copy(1).start()

    for cp in w_copies(0):
        cp.start()

    def tile(j, carry):
        e = te_ref[j]
        r = tr_ref[j]

        @pl.when(j + 2 < n_used)
        def _():
            x_copy(j + 2).start()

        @pl.when((j == 0) | (r != tr_ref[jnp.maximum(j - 1, 0)]))
        def _():
            for cp in w_copies(r):
                cp.wait()
            slot = r % 2
            wgu_ref[:, :EXPERT_FF] = wg_raw[slot].astype(BF16)
            wgu_ref[:, EXPERT_FF:] = wu_raw[slot].astype(BF16)
            wd_ref[...] = wd_raw[slot].astype(BF16)

            @pl.when(r + 1 < n_runs)
            def _():
                for cp in w_copies(r + 1):
                    cp.start()

        x_copy(j).wait()

        @pl.when(j >= 2)
        def _():
            y_copy(j - 2).wait()

        x_ref = xbuf_ref.at[j % FFN_RING]
        rows = lax.broadcasted_iota(jnp.int32, (TM_FFN, 1), 0)
        valid = rows < tv_ref[j]
        x = jnp.where(valid, x_ref[:, :D_MODEL], jnp.zeros((), BF16))
        lane = lax.broadcasted_iota(jnp.int32, (TM_FFN, LANES), 1)
        mine = ((lane == e) | (lane == e + N_EXPERTS)) & valid
        w = jnp.sum(jnp.where(mine, x_ref[:, D_MODEL:].astype(F32), 0.0), axis=-1, keepdims=True)
        hgu = jnp.dot(x, wgu_ref[...], preferred_element_type=F32)
        g = hgu[:, :EXPERT_FF]
        a = (g * jax.nn.sigmoid(g)) * hgu[:, EXPERT_FF:] * w
        ybuf_ref[j % 2] = jnp.dot(a.astype(BF16), wd_ref[...], preferred_element_type=F32).astype(BF16)
        y_copy(j).start()
        return carry

    lax.fori_loop(0, n_used, tile, 0)

    @pl.when(n_used >= 2)
    def _():
        y_copy(n_used - 2).wait()

    y_copy(n_used - 1).wait()


def _expert_ffn(plan, xs, p):
    any_spec = pl.BlockSpec(memory_space=pl.ANY)
    grid_spec = pltpu.PrefetchScalarGridSpec(
        num_scalar_prefetch=6,
        grid=(1,),
        in_specs=[any_spec, any_spec, any_spec, any_spec],
        out_specs=any_spec,
        scratch_shapes=[pltpu.VMEM((FFN_RING, TM_FFN, H_AUG), BF16),
                        pltpu.VMEM((2, TM_FFN, D_MODEL), BF16),
                        pltpu.VMEM((2, D_MODEL, EXPERT_FF), F32),
                        pltpu.VMEM((2, D_MODEL, EXPERT_FF), F32),
                        pltpu.VMEM((2, EXPERT_FF, D_MODEL), F32),
                        pltpu.VMEM((D_MODEL, 2 * EXPERT_FF), BF16),
                        pltpu.VMEM((EXPERT_FF, D_MODEL), BF16),
                        pltpu.SemaphoreType.DMA((FFN_RING,)),
                        pltpu.SemaphoreType.DMA((2,)),
                        pltpu.SemaphoreType.DMA((2,))],
    )
    return pl.pallas_call(
        functools.partial(_ffn_kernel, layer=p["layer"]),
        out_shape=jax.ShapeDtypeStruct((SORTED_ROWS, D_MODEL), BF16),
        grid_spec=grid_spec,
        compiler_params=_cparams(("arbitrary",)),
        name="moe_expert_ffn",
    )(plan["n_used"], plan["t_exp"], plan["t_valid"], plan["t_run"], plan["run_exp"], plan["n_runs"],
      xs, p["w_exp_gate"], p["w_exp_up"], p["w_exp_down"])


def _combine_kernel(nc_ref, gs_ref, y_ref, rktok_ref, srow_ref, scol_ref, h_ref, x1_ref, mod_ref,
                    sgu_ref, sd_ref, o_ref, yl_ref, acc_ref, sem):
    i = pl.program_id(0)
    slot = i % 2

    def gather(step, slot_):
        def body(c, carry):
            for k in range(GROUPS_PER_CHUNK):
                src = pl.multiple_of(
                    gs_ref[step * MAX_GROUPS + c * GROUPS_PER_CHUNK + k] * SEG_ALIGN, SEG_ALIGN)
                dst = pl.multiple_of(c * CHUNK_ROWS + k * SEG_ALIGN, SEG_ALIGN)
                pltpu.make_async_copy(y_ref.at[pl.ds(src, SEG_ALIGN), :],
                                      yl_ref.at[slot_, pl.ds(dst, SEG_ALIGN), :], sem.at[slot_]).start()
            return carry
        lax.fori_loop(0, nc_ref[step], body, 0)

    @pl.when(i == 0)
    def _():
        yl_ref[...] = jnp.zeros(yl_ref.shape, BF16)
        gather(0, 0)

    @pl.when(i + 1 < N_TILES)
    def _():
        gather(i + 1, 1 - slot)

    h = h_ref[...]
    hs = jnp.dot(h, sgu_ref[...], preferred_element_type=F32)
    gs = hs[:, :EXPERT_FF]
    a = (gs * jax.nn.sigmoid(gs)) * hs[:, EXPERT_FF:]
    shared = jnp.dot(a.astype(BF16), sd_ref[...], preferred_element_type=F32)

    n_c = nc_ref[i]

    def wait_body(c, carry):
        pltpu.make_async_copy(y_ref.at[pl.ds(0, CHUNK_ROWS), :],
                              yl_ref.at[slot, pl.ds(0, CHUNK_ROWS), :], sem.at[slot]).wait()
        return carry

    lax.fori_loop(0, n_c, wait_body, 0)

    ls_col = scol_ref[:, 0:1]
    le_col = scol_ref[:, 1:2]
    ls_row8 = jnp.broadcast_to(srow_ref[0:1, :], (8, N_EXPERTS)).astype(BF16)
    rk_ls = jnp.concatenate([rktok_ref[...], ls_row8], axis=0)
    rio_e0 = lax.broadcasted_iota(jnp.int32, (N_EXPERTS, CHUNK_ROWS), 1).astype(F32)
    rio_r0 = lax.broadcasted_iota(jnp.int32, (1, CHUNK_ROWS), 1).astype(F32)

    def select_t(r0f):
        rio_e = rio_e0 + r0f
        gt = jnp.where((rio_e >= ls_col) & (rio_e < le_col), 1.0, 0.0).astype(BF16)
        both = jnp.dot(rk_ls, gt, preferred_element_type=F32)
        want = (rio_r0 + r0f) - both[TT:TT + 1, :]
        return jnp.where(both[:TT, :] == want, 1.0, 0.0).astype(BF16)

    static_rows = STATIC_CHUNKS * CHUNK_ROWS
    pt_all = jnp.concatenate([select_t(float(c * CHUNK_ROWS)) for c in range(STATIC_CHUNKS)], axis=1)
    acc_ref[...] = shared + jnp.dot(pt_all, yl_ref[slot, pl.ds(0, static_rows), :],
                                    preferred_element_type=F32)

    def chunk(c, carry):
        r0 = pl.multiple_of(c * CHUNK_ROWS, CHUNK_ROWS)
        acc_ref[...] += jnp.dot(select_t((c * CHUNK_ROWS).astype(F32)),
                                yl_ref[slot, pl.ds(r0, CHUNK_ROWS), :], preferred_element_type=F32)
        return carry

    lax.fori_loop(STATIC_CHUNKS, n_c, chunk, 0)
    o_ref[...] = x1_ref[...] + mod_ref[5:6, :] * acc_ref[...]


def _combine(plan, y, rktok, h2aug, x1, mod_l, p):
    tiles_per_seq = SEQ // TT
    grid_spec = pltpu.PrefetchScalarGridSpec(
        num_scalar_prefetch=2,
        grid=(N_TILES,),
        in_specs=[pl.BlockSpec(memory_space=pl.ANY),
                  pl.BlockSpec((TT, N_EXPERTS), lambda i, *_: (i, 0)),
                  pl.BlockSpec((None, 2, N_EXPERTS), lambda i, *_: (i, 0, 0)),
                  pl.BlockSpec((None, N_EXPERTS, 2), lambda i, *_: (i, 0, 0)),
                  pl.BlockSpec((TT, D_MODEL), lambda i, *_: (i, 0)),
                  pl.BlockSpec((TT, D_MODEL), lambda i, *_: (i, 0)),
                  pl.BlockSpec((None, 6, D_MODEL), lambda i, *_: (i // tiles_per_seq, 0, 0)),
                  pl.BlockSpec((D_MODEL, 2 * EXPERT_FF), lambda i, *_: (0, 0)),
                  pl.BlockSpec((EXPERT_FF, D_MODEL), lambda i, *_: (0, 0))],
        out_specs=pl.BlockSpec((TT, D_MODEL), lambda i, *_: (i, 0)),
        scratch_shapes=[pltpu.VMEM((2, LOCAL_ROWS, D_MODEL), BF16),
                        pltpu.VMEM((TT, D_MODEL), F32),
                        pltpu.SemaphoreType.DMA((2,))],
    )
    return pl.pallas_call(
        _combine_kernel,
        out_shape=jax.ShapeDtypeStruct((N_TOK, D_MODEL), F32),
        grid_spec=grid_spec,
        compiler_params=_cparams(("arbitrary",)),
        name="moe_combine",
    )(plan["n_chunks"], plan["group_src"], y, rktok, plan["seg_row"], plan["seg_col"], h2aug, x1,
      mod_l, p["w_sh_gu"], p["w_sh_down"])


def _moe(h2aug, rkt, rktok, cnt, x1, mod_l, p):
    plan = _moe_plan(cnt[:, :, 0])
    xs = _dispatch(plan, h2aug, rkt)
    y = _expert_ffn(plan, xs, p)
    return _combine(plan, y, rktok, h2aug, x1, mod_l, p)


def _pad_heads(w, per_head):
    k = w.shape[0]
    w = w.reshape(k, MLA_HEADS, per_head)
    w = jnp.pad(w, ((0, 0), (0, 0), (0, HEAD_PAD - per_head)))
    return w.reshape(k, MLA_HEADS * HEAD_PAD)


def _layer_params(l, a):
    w_in = a["w_in"][l]
    pe_cols = jnp.zeros((D_MODEL, LANES), F32).at[:, QK_NOPE:QK_DIM].set(w_in[:, O_PE:])
    w_in_p = jnp.concatenate([w_in[:, :O_PE], pe_cols], axis=1).astype(BF16)
    w_kv = a["w_kv_up"][l].reshape(KV_LORA, MLA_HEADS, QK_NOPE + V_DIM)
    w_k = _pad_heads(w_kv[:, :, :QK_NOPE].reshape(KV_LORA, MLA_HEADS * QK_NOPE), QK_NOPE)
    w_v = w_kv[:, :, QK_NOPE:].reshape(KV_LORA, MLA_HEADS * V_DIM)
    blocks = TT // GM_BLOCK
    eye = jnp.eye(blocks, dtype=F32)
    ws_big = jnp.einsum("ab,gts->gatbs", eye, a["gm_ws"][l]).reshape(GM_GROUPS, TT, TT)
    bs_tile = jnp.tile(jnp.repeat(a["gm_bs"][l].T, GM_CH, axis=1), (TM // GM_BLOCK, 1))
    pad96 = lambda g: jnp.pad(g, (0, LANES - QK_DIM)).reshape(1, LANES)
    return {
        "norm_mix_g": a["norm_mix_g"][l].reshape(1, D_MODEL),
        "norm_ffn_g": a["norm_ffn_g"][l].reshape(1, D_MODEL),
        "w_in": w_in_p,
        "gm_ln_g": a["gm_ln_g"][l].reshape(1, GM_WIDTH),
        "gm_ln_b": a["gm_ln_b"][l].reshape(1, GM_WIDTH),
        "gm_ws": ws_big.astype(BF16),
        "gm_bs": bs_tile,
        "q_a_g": a["q_a_g"][l].reshape(1, Q_LORA),
        "w_q_up": _pad_heads(a["w_q_up"][l], QK_DIM).astype(BF16),
        "kv_a_g": a["kv_a_g"][l].reshape(1, KV_LORA),
        "w_k_up": w_k.astype(BF16),
        "w_v_t": jnp.pad(w_v.T.reshape(MLA_HEADS, V_DIM, KV_LORA), ((0, 0), (0, V_AUG - V_DIM), (0, 0))
                         ).reshape(MLA_HEADS * V_AUG, KV_LORA).astype(BF16),
        "v_ones": jnp.tile((jnp.arange(V_AUG) == V_DIM).astype(F32), MLA_HEADS).reshape(-1, 1),
        "q_norm_g": pad96(a["q_norm_g"][l]),
        "k_norm_g": pad96(a["k_norm_g"][l]),
        "out_norm_gm": a["out_norm_g"][l, :GM_WIDTH].reshape(1, GM_WIDTH),
        "out_norm_mla": a["out_norm_g"][l, GM_WIDTH:].reshape(1, MLA_WIDTH),
        "w_out": a["w_out"][l].astype(BF16),
        "w_router_t": a["w_router"][l].T,
        "router_bias": a["router_bias"][l].reshape(N_EXPERTS, 1),
        "layer": l,
        "w_exp_gate": a["w_exp_gate"],
        "w_exp_up": a["w_exp_up"],
        "w_exp_down": a["w_exp_down"],
        "w_sh_gu": jnp.concatenate([a["w_sh_gate"][l], a["w_sh_up"][l]], axis=-1).astype(BF16),
        "w_sh_down": a["w_sh_down"][l].astype(BF16),
    }


def kernel(x, c, positions, norm_mix_g, norm_ffn_g, w_ada, b_ada, w_in, gm_ln_g, gm_ln_b, gm_ws, gm_bs, q_a_g, w_q_up, kv_a_g, w_kv_up, q_norm_g, k_norm_g, out_norm_g, w_out, w_router, router_bias, w_exp_gate, w_exp_up, w_exp_down, w_sh_gate, w_sh_up, w_sh_down):
    a = dict(norm_mix_g=norm_mix_g, norm_ffn_g=norm_ffn_g, w_in=w_in, gm_ln_g=gm_ln_g,
             gm_ln_b=gm_ln_b, gm_ws=gm_ws, gm_bs=gm_bs, q_a_g=q_a_g, w_q_up=w_q_up,
             kv_a_g=kv_a_g, w_kv_up=w_kv_up, q_norm_g=q_norm_g, k_norm_g=k_norm_g,
             out_norm_g=out_norm_g, w_out=w_out, w_router=w_router, router_bias=router_bias,
             w_exp_gate=w_exp_gate, w_exp_up=w_exp_up, w_exp_down=w_exp_down,
             w_sh_gate=w_sh_gate, w_sh_up=w_sh_up, w_sh_down=w_sh_down)
    cos_t, sin_t = _rope_tables(positions)
    mod = _ada_mod(c, w_ada, b_ada).reshape(DEPTH, BATCH, 6, D_MODEL)
    xt = x.reshape(N_TOK, D_MODEL)
    for l in range(DEPTH):
        p = _layer_params(l, a)
        gm, q, k, v = _mix_in(xt, mod[l], p, cos_t, sin_t)
        mla = _attention(q, k, v).reshape(N_TOK, MLA_WIDTH)
        x1, h2aug, rkt, rktok, cnt = _mix_out(xt, gm, mla, mod[l], p)
        xt = _moe(h2aug, rkt, rktok, cnt, x1, mod[l], p)
    return xt.reshape(BATCH, SEQ, D_MODEL)
```

```python
import functools

import jax
import jax.numpy as jnp
from jax import lax
from jax.experimental import pallas as pl
from jax.experimental.pallas import tpu as pltpu

D_MODEL = 1024
BATCH = 8
SEQ = 4096
DEPTH = 2
N_TOK = BATCH * SEQ
CHUNK = 64
GM_WIDTH = 512
GM_BLOCK = 128
GM_GROUPS = 4
GM_CH = 128
MLA_HEADS = 8
QK_NOPE = 64
QK_ROPE = 32
QK_DIM = 96
V_DIM = 64
MLA_WIDTH = 512
Q_LORA = 384
KV_LORA = 256
ROPE_THETA = 10000.0
N_EXPERTS = 64
TOP_K = 8
N_GROUPS = 8
GROUP_SIZE = N_EXPERTS // N_GROUPS
TOPK_GROUPS = 4
EXPERT_FF = 256
ROUTED_SCALE = 2.5
EPS = 1e-6

LANES = 128
HEAD_PAD = LANES
IN_COLS_PAD = 2 * GM_WIDTH + Q_LORA + KV_LORA + LANES
O_Q = 2 * GM_WIDTH
O_KV = O_Q + Q_LORA
O_PE = O_KV + KV_LORA

TT = 256
TM = 512
TQ = 512
TK = 256
ATT_HEADS = 8
V_AUG = 80
N_TILES = N_TOK // TT
H_AUG = D_MODEL + LANES
SEG_ALIGN = 16
CHUNK_ROWS = 256
LOCAL_ROWS = TT * TOP_K + N_EXPERTS * SEG_ALIGN
MAX_GROUPS = LOCAL_ROWS // SEG_ALIGN
TM_FFN = 512
FFN_RING = 3
SORTED_ROWS = -(-(N_TOK * TOP_K + N_TILES * N_EXPERTS * (SEG_ALIGN - 1)
                  + N_EXPERTS * (TM_FFN - 1)) // TM_FFN) * TM_FFN
FFN_TILES = SORTED_ROWS // TM_FFN
GROUPS_PER_CHUNK = CHUNK_ROWS // SEG_ALIGN
TRASH_ROWS = 2 * CHUNK_ROWS
STATIC_CHUNKS = 10
VMEM_LIMIT = 56 * 1024 * 1024

F32 = jnp.float32
BF16 = jnp.bfloat16


def _cparams(sem, flags=None):
    return pltpu.CompilerParams(dimension_semantics=sem, vmem_limit_bytes=VMEM_LIMIT, flags=flags)


def _rms_scale(v, width):
    return lax.rsqrt(jnp.sum(v * v, axis=-1, keepdims=True) * (1.0 / width) + EPS)


def _rope_kernel(pos_ref, inv_ref, cos_ref, sin_ref):
    ang = pos_ref[...].astype(F32) * inv_ref[...]
    lane = lax.broadcasted_iota(jnp.int32, ang.shape, 1)
    c = jnp.cos(ang)
    s = jnp.sin(ang)
    cos_ref[...] = jnp.where(lane < QK_NOPE, 1.0, jnp.where(lane < QK_DIM, c, 0.0))
    sin_ref[...] = jnp.where((lane >= QK_NOPE) & (lane < QK_NOPE + 16), -s,
                             jnp.where((lane >= QK_NOPE + 16) & (lane < QK_DIM), s, 0.0))


def _rope_tables(positions):
    inv_freq = 1.0 / (ROPE_THETA ** (jnp.arange(0, QK_ROPE, 2, dtype=F32) / QK_ROPE))
    inv = jnp.zeros((1, LANES), F32)
    inv = inv.at[0, QK_NOPE:QK_NOPE + 16].set(inv_freq)
    inv = inv.at[0, QK_NOPE + 16:QK_DIM].set(inv_freq)
    pos = positions.reshape(N_TOK, 1)
    tile = 2048
    return pl.pallas_call(
        _rope_kernel,
        out_shape=(jax.ShapeDtypeStruct((N_TOK, LANES), F32),
                   jax.ShapeDtypeStruct((N_TOK, LANES), F32)),
        grid=(N_TOK // tile,),
        in_specs=[pl.BlockSpec((tile, 1), lambda i: (i, 0)),
                  pl.BlockSpec((1, LANES), lambda i: (0, 0))],
        out_specs=(pl.BlockSpec((tile, LANES), lambda i: (i, 0)),
                   pl.BlockSpec((tile, LANES), lambda i: (i, 0))),
        compiler_params=_cparams(("arbitrary",)),
        name="rope_tables",
    )(pos, inv)


def _ada_kernel(c_ref, w_ref, b_ref, o_ref):
    c = c_ref[...]
    ca = c * jax.nn.sigmoid(c)
    o_ref[...] = jnp.dot(ca, w_ref[...], precision=lax.Precision.HIGHEST,
                         preferred_element_type=F32) + b_ref[...]


def _ada_mod(c, w_ada, b_ada):
    return pl.pallas_call(
        _ada_kernel,
        out_shape=jax.ShapeDtypeStruct((DEPTH, BATCH, 6 * D_MODEL), F32),
        grid=(DEPTH, 6),
        in_specs=[pl.BlockSpec((BATCH, D_MODEL), lambda l, j: (0, 0)),
                  pl.BlockSpec((None, D_MODEL, D_MODEL), lambda l, j: (l, 0, j)),
                  pl.BlockSpec((None, 1, D_MODEL), lambda l, j: (l, 0, j))],
        out_specs=pl.BlockSpec((None, BATCH, D_MODEL), lambda l, j: (l, 0, j)),
        compiler_params=_cparams(("arbitrary", "arbitrary")),
        name="ada_mod",
    )(c, w_ada, b_ada.reshape(DEPTH, 1, 6 * D_MODEL))


def _rope_rot(t, cos_t, sin_t):
    return t * cos_t + pltpu.roll(t, LANES - 16, 1) * sin_t


def _mix_in_kernel(x_ref, mod_ref, ng_ref, win_ref, lng_ref, lnb_ref, ws_ref, bs_ref,
                   qag_ref, wq_ref, kvg_ref, wkv_ref, wvt_ref, vone_ref, qng_ref, kng_ref, og_ref,
                   cos_ref, sin_ref,
                   gm_ref, q_ref, k_ref, vt_ref):
    x = x_ref[...]
    sh1 = mod_ref[0:1, :]
    sc1 = mod_ref[1:2, :]
    h = x * _rms_scale(x, D_MODEL) * ng_ref[...]
    h = h * (1.0 + sc1) + sh1
    proj = jnp.dot(h.astype(BF16), win_ref[...], preferred_element_type=F32)

    uv_pre = proj[:, :2 * GM_WIDTH]
    uv = 0.5 * uv_pre * (1.0 + lax.erf(uv_pre * (2.0 ** -0.5)))
    row = lax.broadcasted_iota(jnp.int32, (TT, TT), 0)
    col = lax.broadcasted_iota(jnp.int32, (TT, TT), 1)
    causal = (col <= row) & ((col // GM_BLOCK) == (row // GM_BLOCK))
    groups = range(GM_GROUPS)
    gsl = [slice(g * GM_CH, (g + 1) * GM_CH) for g in groups]
    vg = [uv[:, GM_WIDTH + g * GM_CH:GM_WIDTH + (g + 1) * GM_CH] for g in groups]
    mu = [jnp.mean(vg[g], axis=-1, keepdims=True) for g in groups]
    xc = [vg[g] - mu[g] for g in groups]
    var = [jnp.mean(xc[g] * xc[g], axis=-1, keepdims=True) for g in groups]
    vln = [(xc[g] * lax.rsqrt(var[g] + EPS) * lng_ref[:, gsl[g]] + lnb_ref[:, gsl[g]]).astype(BF16)
           for g in groups]
    ws = [jnp.where(causal, ws_ref[g], jnp.zeros((), BF16)) for g in groups]
    gate = [jnp.concatenate([jnp.dot(ws[g], vln[g][s * TT:(s + 1) * TT], preferred_element_type=F32)
                             for s in range(TM // TT)], axis=0) for g in groups]
    outs = [uv[:, gsl[g]] * (gate[g] + bs_ref[:, gsl[g]]) for g in groups]
    sq = [jnp.sum(outs[g] * outs[g], axis=-1, keepdims=True) for g in groups]
    rs = lax.rsqrt((sq[0] + sq[1] + sq[2] + sq[3]) * (1.0 / GM_WIDTH) + EPS)
    for g in groups:
        gm_ref[:, gsl[g]] = (outs[g] * rs * og_ref[:, gsl[g]]).astype(BF16)

    cos_t = cos_ref[...]
    sin_t = sin_ref[...]
    lane = lax.broadcasted_iota(jnp.int32, (TM, LANES), 1)
    ql = proj[:, O_Q:O_KV]
    qn = ql * _rms_scale(ql, Q_LORA) * qag_ref[...]
    qa = jnp.dot(qn.astype(BF16), wq_ref[...], preferred_element_type=F32)
    kl = proj[:, O_KV:O_PE]
    kn = kl * _rms_scale(kl, KV_LORA) * kvg_ref[...]
    kva = jnp.dot(kn.astype(BF16), wkv_ref[...], preferred_element_type=F32)
    pe = proj[:, O_PE:O_PE + LANES]
    real = lane < QK_DIM
    pe_ssq = jnp.sum(jnp.where(real, pe * pe, 0.0), axis=-1, keepdims=True)
    pe_rot = _rope_rot(pe * kng_ref[...], cos_t, sin_t)
    qscale = (QK_DIM ** -0.5) * 1.4426950408889634
    v_t = lax.dot_general(wvt_ref[...], kn.astype(BF16), (((1,), (1,)), ((), ())),
                          preferred_element_type=F32) + vone_ref[...]
    vt_ref[...] = v_t.reshape(MLA_HEADS, V_AUG, TM).astype(BF16)
    heads = range(MLA_HEADS)
    qh = [qa[:, hd * HEAD_PAD:(hd + 1) * HEAD_PAD] for hd in heads]
    kh = [kva[:, hd * HEAD_PAD:(hd + 1) * HEAD_PAD] for hd in heads]
    q_ssq = [jnp.sum(jnp.where(real, qh[hd] * qh[hd], 0.0), axis=-1, keepdims=True) for hd in heads]
    k_ssq = [jnp.sum(kh[hd] * kh[hd], axis=-1, keepdims=True) for hd in heads]
    qg = [qh[hd] * lax.rsqrt(q_ssq[hd] * (1.0 / QK_DIM) + EPS) * qng_ref[...] for hd in heads]
    q_sw = [pltpu.roll(qg[hd], LANES - 16, 1) for hd in heads]
    for hd in heads:
        q_ref[hd] = ((qg[hd] * cos_t + q_sw[hd] * sin_t) * qscale).astype(BF16)
    for hd in heads:
        krs = lax.rsqrt((k_ssq[hd] + pe_ssq) * (1.0 / QK_DIM) + EPS)
        k_ref[hd] = ((kh[hd] * kng_ref[...] + pe_rot) * krs).astype(BF16)


def _mix_in(x, mod_l, p, cos_t, sin_t):
    tiles_per_seq = SEQ // TM
    full = lambda shape: pl.BlockSpec(shape, lambda i: (0,) * len(shape))
    head_spec = pl.BlockSpec((None, MLA_HEADS, TM, HEAD_PAD),
                             lambda i: (i // tiles_per_seq, 0, i % tiles_per_seq, 0))
    head_shape = jax.ShapeDtypeStruct((BATCH, MLA_HEADS, SEQ, HEAD_PAD), BF16)
    return pl.pallas_call(
        _mix_in_kernel,
        out_shape=(jax.ShapeDtypeStruct((N_TOK, GM_WIDTH), BF16), head_shape, head_shape,
                   jax.ShapeDtypeStruct((BATCH, MLA_HEADS, V_AUG, SEQ), BF16)),
        grid=(N_TOK // TM,),
        in_specs=[pl.BlockSpec((TM, D_MODEL), lambda i: (i, 0)),
                  pl.BlockSpec((None, 6, D_MODEL), lambda i: (i // tiles_per_seq, 0, 0)),
                  full((1, D_MODEL)),
                  full((D_MODEL, IN_COLS_PAD)),
                  full((1, GM_WIDTH)), full((1, GM_WIDTH)),
                  full((GM_GROUPS, TT, TT)), full((TM, GM_WIDTH)),
                  full((1, Q_LORA)), full((Q_LORA, MLA_HEADS * HEAD_PAD)),
                  full((1, KV_LORA)), full((KV_LORA, MLA_HEADS * HEAD_PAD)),
                  full((MLA_HEADS * V_AUG, KV_LORA)), full((MLA_HEADS * V_AUG, 1)),
                  full((1, LANES)), full((1, LANES)), full((1, GM_WIDTH)),
                  pl.BlockSpec((TM, LANES), lambda i: (i, 0)),
                  pl.BlockSpec((TM, LANES), lambda i: (i, 0))],
        out_specs=(pl.BlockSpec((TM, GM_WIDTH), lambda i: (i, 0)), head_spec, head_spec,
                   pl.BlockSpec((None, MLA_HEADS, V_AUG, TM),
                                lambda i: (i // tiles_per_seq, 0, 0, i % tiles_per_seq))),
        compiler_params=_cparams(("arbitrary",)),
        name="mix_in",
    )(x, mod_l, p["norm_mix_g"], p["w_in"], p["gm_ln_g"], p["gm_ln_b"], p["gm_ws"], p["gm_bs"],
      p["q_a_g"], p["w_q_up"], p["kv_a_g"], p["w_k_up"], p["w_v_t"], p["v_ones"], p["q_norm_g"],
      p["k_norm_g"],
      p["out_norm_gm"], cos_t, sin_t)


def _attn_kernel(q_ref, k_ref, vt_ref, o_ref):
    qi = pl.program_id(2)
    krow = lax.broadcasted_iota(jnp.int32, (TK, TQ), 0)
    qcol = lax.broadcasted_iota(jnp.int32, (TK, TQ), 1)
    nt = (((1,), (1,)), ((), ()))
    per_q = TQ // TK

    def tile(j, carry, diag):
        start = pl.multiple_of(j * TK, TK)
        heads = range(ATT_HEADS)
        s = [lax.dot_general(k_ref[hh, pl.ds(start, TK), :], q_ref[hh], nt,
                             preferred_element_type=F32) for hh in heads]
        if diag is not None:
            allowed = ((krow + diag * TK) // CHUNK) <= (qcol // CHUNK)
            s = [jnp.where(allowed, sh, -jnp.inf) for sh in s]
        m_new = [jnp.maximum(carry[hh][0], jnp.max(s[hh], axis=0, keepdims=True)) for hh in heads]
        p = [jnp.exp2(s[hh] - m_new[hh]) for hh in heads]
        alpha = [jnp.exp2(carry[hh][0] - m_new[hh]) for hh in heads]
        pv = [jnp.dot(vt_ref[hh, :, pl.ds(start, TK)], p[hh].astype(BF16),
                      preferred_element_type=F32) for hh in heads]
        return tuple((m_new[hh], alpha[hh] * carry[hh][1] + pv[hh]) for hh in heads)

    init = tuple((jnp.full((1, TQ), -jnp.inf, F32), jnp.zeros((V_AUG, TQ), F32))
                 for _ in range(ATT_HEADS))
    carry = lax.fori_loop(0, qi * per_q, functools.partial(tile, diag=None), init)
    for t in range(per_q):
        carry = tile(qi * per_q + t, carry, t)
    out_t = jnp.concatenate([acc[:V_DIM] / acc[V_DIM:V_DIM + 1] for (_, acc) in carry], axis=0)
    o_ref[...] = out_t.T


def _attention(q, k, vt):
    groups = MLA_HEADS // ATT_HEADS
    qspec = pl.BlockSpec((None, ATT_HEADS, TQ, HEAD_PAD), lambda b, hp, i: (b, hp, i, 0))
    kspec = pl.BlockSpec((None, ATT_HEADS, SEQ, HEAD_PAD), lambda b, hp, i: (b, hp, 0, 0))
    vspec = pl.BlockSpec((None, ATT_HEADS, V_AUG, SEQ), lambda b, hp, i: (b, hp, 0, 0))
    return pl.pallas_call(
        _attn_kernel,
        out_shape=jax.ShapeDtypeStruct((BATCH, SEQ, MLA_WIDTH), F32),
        grid=(BATCH, groups, SEQ // TQ),
        in_specs=[qspec, kspec, vspec],
        out_specs=pl.BlockSpec((None, TQ, ATT_HEADS * V_DIM), lambda b, hp, i: (b, i, hp)),
        compiler_params=_cparams(("arbitrary", "arbitrary", "arbitrary")),
        name="mla_attention",
    )(q, k, vt)


def _route(logits_t, bias_col):
    t = logits_t.shape[1]
    scores = jax.nn.sigmoid(logits_t)
    biased = scores + bias_col
    neg = -jnp.inf
    b3 = biased.reshape(N_GROUPS, GROUP_SIZE, t)
    m1 = jnp.max(b3, axis=1, keepdims=True)
    n_max = jnp.sum((b3 == m1).astype(F32), axis=1, keepdims=True)
    m2 = jnp.max(jnp.where(b3 < m1, b3, neg), axis=1, keepdims=True)
    gs = (m1 + jnp.where(n_max >= 2.0, m1, m2)).reshape(N_GROUPS, t)
    gidx = lax.broadcasted_iota(jnp.int32, (N_GROUPS, t), 0)
    grank = jnp.zeros((N_GROUPS, t), F32)
    for g in range(N_GROUPS):
        other = gs[g:g + 1, :]
        ahead = (other > gs) | ((other == gs) & (gidx > g))
        grank = grank + ahead.astype(F32)
    gsel = grank < float(TOPK_GROUPS)
    emask = jnp.broadcast_to(gsel.reshape(N_GROUPS, 1, t), (N_GROUPS, GROUP_SIZE, t)).reshape(N_EXPERTS, t)
    ms = jnp.where(emask, biased, neg)
    eidx = lax.broadcasted_iota(jnp.int32, (N_EXPERTS, t), 0).astype(F32)
    picked = jnp.zeros((N_EXPERTS, t), F32)
    for _ in range(TOP_K):
        top = jnp.max(ms, axis=0, keepdims=True)
        first = jnp.min(jnp.where(ms == top, eidx, float(N_EXPERTS)), axis=0, keepdims=True)
        hit = eidx == first
        picked = jnp.where(hit, 1.0, picked)
        ms = jnp.where(hit, neg, ms)
    sel = picked > 0.5
    w = jnp.where(sel, scores, 0.0)
    denom = jnp.sum(w, axis=0, keepdims=True)
    return (w / denom) * ROUTED_SCALE, sel


def _mix_out_kernel(x_ref, gm_ref, mla_ref, mod_ref, og_ref, wout_ref, fg_ref, wr_ref, rb_ref,
                    x1_ref, h2_ref, rkt_ref, rktok_ref, cnt_ref):
    mla = mla_ref[...]
    mla_n = mla * _rms_scale(mla, MLA_WIDTH) * og_ref[...]
    mixed = jnp.concatenate([gm_ref[...], mla_n.astype(BF16)], axis=-1)
    y = jnp.dot(mixed, wout_ref[...], preferred_element_type=F32)
    g1 = mod_ref[2:3, :]
    sh2 = mod_ref[3:4, :]
    sc2 = mod_ref[4:5, :]
    x1 = x_ref[...] + g1 * y
    x1_ref[...] = x1
    h2 = x1 * _rms_scale(x1, D_MODEL) * fg_ref[...]
    h2 = h2 * (1.0 + sc2) + sh2
    logits_t = lax.dot_general(wr_ref[...], h2, (((1,), (1,)), ((), ())),
                               precision=lax.Precision.HIGHEST, preferred_element_type=F32)
    comb, sel = _route(logits_t, rb_ref[...])
    before = (lax.broadcasted_iota(jnp.int32, (TT, TT), 0)
              < lax.broadcasted_iota(jnp.int32, (TT, TT), 1))
    self = jnp.where(sel, 1.0, 0.0)
    before_b = jnp.where(before, 1.0, 0.0).astype(BF16)
    rank = jnp.concatenate(
        [jnp.dot(self[:, s * TT:(s + 1) * TT].astype(BF16), before_b, preferred_element_type=F32)
         for s in range(TM // TT)], axis=1)
    rk = jnp.where(sel, rank, -1.0)
    rkt_ref[...] = rk.astype(BF16)
    rktok_ref[...] = rk.T.astype(BF16)
    for s in range(TM // TT):
        cnt_ref[s] = jnp.broadcast_to(
            jnp.sum(self[:, s * TT:(s + 1) * TT], axis=1, keepdims=True), (N_EXPERTS, LANES))
    comb_tok = comb.T
    hi = comb_tok.astype(BF16)
    lo = (comb_tok - hi.astype(F32)).astype(BF16)
    h2_ref[:, :D_MODEL] = h2.astype(BF16)
    h2_ref[:, D_MODEL:] = jnp.concatenate([hi, lo], axis=-1)


def _mix_out(x, gm, mla, mod_l, p):
    tiles_per_seq = SEQ // TM
    full = lambda shape: pl.BlockSpec(shape, lambda i: (0,) * len(shape))
    tok = lambda width: pl.BlockSpec((TM, width), lambda i: (i, 0))
    return pl.pallas_call(
        _mix_out_kernel,
        out_shape=(jax.ShapeDtypeStruct((N_TOK, D_MODEL), F32),
                   jax.ShapeDtypeStruct((N_TOK, H_AUG), BF16),
                   jax.ShapeDtypeStruct((N_EXPERTS, N_TOK), BF16),
                   jax.ShapeDtypeStruct((N_TOK, N_EXPERTS), BF16),
                   jax.ShapeDtypeStruct((N_TILES, N_EXPERTS, LANES), F32)),
        grid=(N_TOK // TM,),
        in_specs=[tok(D_MODEL), tok(GM_WIDTH), tok(MLA_WIDTH),
                  pl.BlockSpec((None, 6, D_MODEL), lambda i: (i // tiles_per_seq, 0, 0)),
                  full((1, MLA_WIDTH)), full((D_MODEL, D_MODEL)), full((1, D_MODEL)),
                  full((N_EXPERTS, D_MODEL)), full((N_EXPERTS, 1))],
        out_specs=(tok(D_MODEL), tok(H_AUG), pl.BlockSpec((N_EXPERTS, TM), lambda i: (0, i)),
                   tok(N_EXPERTS), pl.BlockSpec((TM // TT, N_EXPERTS, LANES), lambda i: (i, 0, 0))),
        compiler_params=_cparams(("arbitrary",)),
        name="mix_out_route",
    )(x, gm, mla, mod_l, p["out_norm_mla"], p["w_out"], p["norm_ffn_g"], p["w_router_t"],
      p["router_bias"])


def _moe_plan(cnt):
    i32 = jnp.int32
    cnt = cnt.astype(i32)
    padlen = (cnt + SEG_ALIGN - 1) // SEG_ALIGN * SEG_ALIGN
    lend = jnp.cumsum(padlen, axis=1)
    lstart = lend - padlen
    ltot = lend[:, -1]
    region = jnp.sum(padlen, axis=0)
    region_pad = (region + TM_FFN - 1) // TM_FFN * TM_FFN
    base = jnp.cumsum(region_pad) - region_pad
    pos = base[None, :] + jnp.cumsum(padlen, axis=0) - padlen
    gidx = jnp.arange(MAX_GROUPS, dtype=i32)
    grow = gidx * SEG_ALIGN
    in_seg = (grow[None, :, None] >= lstart[:, None, :]) & (grow[None, :, None] < lend[:, None, :])
    shift = jnp.sum(jnp.where(in_seg, (pos - lstart)[:, None, :], 0), axis=-1)
    n_groups = ltot // SEG_ALIGN
    valid_g = gidx[None, :] < n_groups[:, None]
    real_dst = (shift + grow[None, :]) // SEG_ALIGN
    trash = SORTED_ROWS // SEG_ALIGN + (jnp.arange(N_TILES, dtype=i32)[:, None] % 2) * GROUPS_PER_CHUNK \
        + gidx[None, :] % GROUPS_PER_CHUNK
    n_chunks = (n_groups + GROUPS_PER_CHUNK - 1) // GROUPS_PER_CHUNK
    tile_end = jnp.cumsum(region_pad) // TM_FFN
    n_used = tile_end[-1]
    tj = jnp.arange(FFN_TILES, dtype=i32)
    in_exp = (tj[:, None] * TM_FFN >= base[None, :]) & (tj[:, None] * TM_FFN < (base + region_pad)[None, :])
    t_exp = jnp.sum(jnp.where(in_exp, jnp.arange(N_EXPERTS, dtype=i32)[None, :], 0), axis=-1)
    t_valid = jnp.sum(jnp.where(in_exp, jnp.clip((base + region)[None, :] - tj[:, None] * TM_FFN, 0, TM_FFN), 0),
                      axis=-1)
    has_rows = region_pad > 0
    run_of_exp = jnp.cumsum(has_rows.astype(i32)) - 1
    eidx = jnp.arange(N_EXPERTS, dtype=i32)
    t_run = jnp.sum(jnp.where(in_exp, run_of_exp[None, :], 0), axis=-1)
    run_exp = jnp.sum(jnp.where(has_rows[None, :] & (run_of_exp[None, :] == eidx[:, None]), eidx[None, :], 0),
                      axis=-1)
    lstart_f = lstart.astype(F32)
    lend_f = lend.astype(F32)
    return {
        "t_run": t_run.astype(i32),
        "run_exp": run_exp.astype(i32),
        "n_runs": jnp.sum(has_rows.astype(i32)).reshape(1),
        "n_chunks": n_chunks.astype(i32),
        "group_dst": jnp.where(valid_g, real_dst, trash).reshape(-1).astype(i32),
        "group_src": jnp.where(valid_g, real_dst, 0).reshape(-1).astype(i32),
        "seg_row": jnp.stack([lstart_f, lend_f], axis=1),
        "seg_col": jnp.stack([lstart_f, lend_f], axis=2),
        "n_used": n_used.reshape(1).astype(i32),
        "t_exp": t_exp.astype(i32),
        "t_valid": t_valid.astype(i32),
    }


def _dispatch_kernel(nc_ref, gd_ref, h_ref, rkt_ref, srow_ref, scol_ref, xs_ref, loc_ref, sem):
    i = pl.program_id(0)
    slot = i % 2

    def wait_chunks(n, slot_):
        def body(c, carry):
            pltpu.make_async_copy(loc_ref.at[slot_, pl.ds(0, GROUPS_PER_CHUNK)],
                                  xs_ref.at[pl.ds(0, GROUPS_PER_CHUNK)], sem.at[slot_]).wait()
            return carry
        lax.fori_loop(0, n, body, 0)

    @pl.when(i >= 2)
    def _():
        wait_chunks(nc_ref[i - 2], slot)

    ls_row = srow_ref[0:1, :]
    le_row = srow_ref[1:2, :]
    ls_rep = jnp.broadcast_to(scol_ref[:, 0:1], (N_EXPERTS, LANES)).astype(BF16)
    rk_ls = jnp.concatenate([rkt_ref[...], ls_rep], axis=1)
    h = h_ref[...]
    n_c = nc_ref[i]
    rio_e0 = lax.broadcasted_iota(jnp.int32, (CHUNK_ROWS, N_EXPERTS), 0).astype(F32)
    rio_t0 = lax.broadcasted_iota(jnp.int32, (CHUNK_ROWS, TT), 0).astype(F32)

    def select(r0f):
        rio_e = rio_e0 + r0f
        g = jnp.where((rio_e >= ls_row) & (rio_e < le_row), 1.0, 0.0).astype(BF16)
        both = jnp.dot(g, rk_ls, preferred_element_type=F32)
        want = (rio_t0 + r0f) - jnp.concatenate([both[:, TT:]] * (TT // LANES), axis=1)
        return jnp.where(both[:, :TT] == want, 1.0, 0.0).astype(BF16)

    def emit(g0, psel):
        loc_ref[slot, pl.ds(g0, GROUPS_PER_CHUNK)] = jnp.dot(
            psel, h, preferred_element_type=F32).astype(BF16).reshape(GROUPS_PER_CHUNK, SEG_ALIGN, H_AUG)

    psels = [select(float(c * CHUNK_ROWS)) for c in range(STATIC_CHUNKS)]
    for c in range(STATIC_CHUNKS):
        emit(c * GROUPS_PER_CHUNK, psels[c])

    def chunk(c, carry):
        emit(pl.multiple_of(c * GROUPS_PER_CHUNK, GROUPS_PER_CHUNK), select((c * CHUNK_ROWS).astype(F32)))
        return carry

    lax.fori_loop(STATIC_CHUNKS, n_c, chunk, 0)

    def issue(c, carry):
        for k in range(GROUPS_PER_CHUNK):
            g = c * GROUPS_PER_CHUNK + k
            pltpu.make_async_copy(loc_ref.at[slot, g], xs_ref.at[gd_ref[i * MAX_GROUPS + g]],
                                  sem.at[slot]).start()
        return carry

    lax.fori_loop(0, n_c, issue, 0)

    @pl.when(i == N_TILES - 1)
    def _():
        wait_chunks(nc_ref[i - 1], 1 - slot)
        wait_chunks(n_c, slot)


def _dispatch(plan, h2aug, rkt):
    grid_spec = pltpu.PrefetchScalarGridSpec(
        num_scalar_prefetch=2,
        grid=(N_TILES,),
        in_specs=[pl.BlockSpec((TT, H_AUG), lambda i, *_: (i, 0)),
                  pl.BlockSpec((N_EXPERTS, TT), lambda i, *_: (0, i)),
                  pl.BlockSpec((None, 2, N_EXPERTS), lambda i, *_: (i, 0, 0)),
                  pl.BlockSpec((None, N_EXPERTS, 2), lambda i, *_: (i, 0, 0))],
        out_specs=pl.BlockSpec(memory_space=pl.ANY),
        scratch_shapes=[pltpu.VMEM((2, MAX_GROUPS, SEG_ALIGN, H_AUG), BF16),
                        pltpu.SemaphoreType.DMA((2,))],
    )
    return pl.pallas_call(
        _dispatch_kernel,
        out_shape=jax.ShapeDtypeStruct(((SORTED_ROWS + TRASH_ROWS) // SEG_ALIGN, SEG_ALIGN, H_AUG), BF16),
        grid_spec=grid_spec,
        compiler_params=_cparams(("arbitrary",)),
        name="moe_dispatch",
    )(plan["n_chunks"], plan["group_dst"], h2aug, rkt, plan["seg_row"], plan["seg_col"])


def _ffn_kernel(nu_ref, te_ref, tv_ref, tr_ref, re_ref, nr_ref,
                xs_ref, wg_hbm, wu_hbm, wd_hbm, y_hbm,
                xbuf_ref, ybuf_ref, wg_raw, wu_raw, wd_raw, wgu_ref, wd_ref, xsem, ysem, wsem,
                *, layer):
    n_used = nu_ref[0]
    n_runs = nr_ref[0]

    tile_groups = TM_FFN // SEG_ALIGN

    def groups_of(t):
        return pl.ds(pl.multiple_of(t * tile_groups, tile_groups), tile_groups)

    def x_copy(t):
        slot = t % FFN_RING
        return pltpu.make_async_copy(xs_ref.at[groups_of(t)], xbuf_ref.at[slot], xsem.at[slot])

    def y_copy(t):
        slot = t % 2
        return pltpu.make_async_copy(ybuf_ref.at[slot], y_hbm.at[groups_of(t)], ysem.at[slot])

    def w_copies(r):
        slot = r % 2
        e = re_ref[r]
        return (pltpu.make_async_copy(wg_hbm.at[layer, e], wg_raw.at[slot], wsem.at[slot]),
                pltpu.make_async_copy(wu_hbm.at[layer, e], wu_raw.at[slot], wsem.at[slot]),
                pltpu.make_async_copy(wd_hbm.at[layer, e], wd_raw.at[slot], wsem.at[slot]))

    x_copy(0).start()

    @pl.when(n_used > 1)
    def _():
        x_copy(1).start()

    for cp in w_copies(0):
        cp.start()

    def tile(j, carry):
        e = te_ref[j]
        r = tr_ref[j]

        @pl.when(j + 2 < n_used)
        def _():
            x_copy(j + 2).start()

        @pl.when((j == 0) | (r != tr_ref[jnp.maximum(j - 1, 0)]))
        def _():
            for cp in w_copies(r):
                cp.wait()
            slot = r % 2
            wgu_ref[:, :EXPERT_FF] = wg_raw[slot].astype(BF16)
            wgu_ref[:, EXPERT_FF:] = wu_raw[slot].astype(BF16)
            wd_ref[...] = wd_raw[slot].astype(BF16)

            @pl.when(r + 1 < n_runs)
            def _():
                for cp in w_copies(r + 1):
                    cp.start()

        x_copy(j).wait()

        @pl.when(j >= 2)
        def _():
            y_copy(j - 2).wait()

        xt = xbuf_ref[j % FFN_RING].reshape(TM_FFN, H_AUG)
        rows = lax.broadcasted_iota(jnp.int32, (TM_FFN, 1), 0)
        valid = rows < tv_ref[j]
        x = jnp.where(valid, xt[:, :D_MODEL], jnp.zeros((), BF16))
        lane = lax.broadcasted_iota(jnp.int32, (TM_FFN, LANES), 1)
        mine = ((lane == e) | (lane == e + N_EXPERTS)) & valid
        w = jnp.sum(jnp.where(mine, xt[:, D_MODEL:].astype(F32), 0.0), axis=-1, keepdims=True)
        hgu = jnp.dot(x, wgu_ref[...], preferred_element_type=F32)
        g = hgu[:, :EXPERT_FF]
        a = (g * jax.nn.sigmoid(g)) * hgu[:, EXPERT_FF:] * w
        ybuf_ref[j % 2] = jnp.dot(a.astype(BF16), wd_ref[...], preferred_element_type=F32).astype(
            BF16).reshape(tile_groups, SEG_ALIGN, D_MODEL)
        y_copy(j).start()
        return carry

    lax.fori_loop(0, n_used, tile, 0)

    @pl.when(n_used >= 2)
    def _():
        y_copy(n_used - 2).wait()

    y_copy(n_used - 1).wait()


def _expert_ffn(plan, xs, p):
    any_spec = pl.BlockSpec(memory_space=pl.ANY)
    grid_spec = pltpu.PrefetchScalarGridSpec(
        num_scalar_prefetch=6,
        grid=(1,),
        in_specs=[any_spec, any_spec, any_spec, any_spec],
        out_specs=any_spec,
        scratch_shapes=[pltpu.VMEM((FFN_RING, TM_FFN // SEG_ALIGN, SEG_ALIGN, H_AUG), BF16),
                        pltpu.VMEM((2, TM_FFN // SEG_ALIGN, SEG_ALIGN, D_MODEL), BF16),
                        pltpu.VMEM((2, D_MODEL, EXPERT_FF), F32),
                        pltpu.VMEM((2, D_MODEL, EXPERT_FF), F32),
                        pltpu.VMEM((2, EXPERT_FF, D_MODEL), F32),
                        pltpu.VMEM((D_MODEL, 2 * EXPERT_FF), BF16),
                        pltpu.VMEM((EXPERT_FF, D_MODEL), BF16),
                        pltpu.SemaphoreType.DMA((FFN_RING,)),
                        pltpu.SemaphoreType.DMA((2,)),
                        pltpu.SemaphoreType.DMA((2,))],
    )
    return pl.pallas_call(
        functools.partial(_ffn_kernel, layer=p["layer"]),
        out_shape=jax.ShapeDtypeStruct((SORTED_ROWS // SEG_ALIGN, SEG_ALIGN, D_MODEL), BF16),
        grid_spec=grid_spec,
        compiler_params=_cparams(("arbitrary",)),
        name="moe_expert_ffn",
    )(plan["n_used"], plan["t_exp"], plan["t_valid"], plan["t_run"], plan["run_exp"], plan["n_runs"],
      xs, p["w_exp_gate"], p["w_exp_up"], p["w_exp_down"])


def _combine_kernel(nc_ref, gs_ref, y_ref, rktok_ref, srow_ref, scol_ref, h_ref, x1_ref, mod_ref,
                    sgu_ref, sd_ref, o_ref, yl_ref, acc_ref, sem):
    i = pl.program_id(0)
    slot = i % 2

    def gather(step, slot_):
        def body(c, carry):
            for k in range(GROUPS_PER_CHUNK):
                g = c * GROUPS_PER_CHUNK + k
                pltpu.make_async_copy(y_ref.at[gs_ref[step * MAX_GROUPS + g]], yl_ref.at[slot_, g],
                                      sem.at[slot_]).start()
            return carry
        lax.fori_loop(0, nc_ref[step], body, 0)

    @pl.when(i == 0)
    def _():
        yl_ref[...] = jnp.zeros(yl_ref.shape, BF16)
        gather(0, 0)

    @pl.when(i + 1 < N_TILES)
    def _():
        gather(i + 1, 1 - slot)

    h = h_ref[...]
    hs = jnp.dot(h, sgu_ref[...], preferred_element_type=F32)
    gs = hs[:, :EXPERT_FF]
    a = (gs * jax.nn.sigmoid(gs)) * hs[:, EXPERT_FF:]
    shared = jnp.dot(a.astype(BF16), sd_ref[...], preferred_element_type=F32)

    n_c = nc_ref[i]

    def wait_body(c, carry):
        pltpu.make_async_copy(y_ref.at[pl.ds(0, GROUPS_PER_CHUNK)],
                              yl_ref.at[slot, pl.ds(0, GROUPS_PER_CHUNK)], sem.at[slot]).wait()
        return carry

    lax.fori_loop(0, n_c, wait_body, 0)

    ls_col = scol_ref[:, 0:1]
    le_col = scol_ref[:, 1:2]
    ls_row8 = jnp.broadcast_to(srow_ref[0:1, :], (8, N_EXPERTS)).astype(BF16)
    rk_ls = jnp.concatenate([rktok_ref[...], ls_row8], axis=0)
    rio_e0 = lax.broadcasted_iota(jnp.int32, (N_EXPERTS, CHUNK_ROWS), 1).astype(F32)
    rio_r0 = lax.broadcasted_iota(jnp.int32, (1, CHUNK_ROWS), 1).astype(F32)

    def select_t(r0f):
        rio_e = rio_e0 + r0f
        gt = jnp.where((rio_e >= ls_col) & (rio_e < le_col), 1.0, 0.0).astype(BF16)
        both = jnp.dot(rk_ls, gt, preferred_element_type=F32)
        want = (rio_r0 + r0f) - both[TT:TT + 1, :]
        return jnp.where(both[:TT, :] == want, 1.0, 0.0).astype(BF16)

    static_rows = STATIC_CHUNKS * CHUNK_ROWS
    pt_all = jnp.concatenate([select_t(float(c * CHUNK_ROWS)) for c in range(STATIC_CHUNKS)], axis=1)
    y_static = yl_ref[slot, pl.ds(0, STATIC_CHUNKS * GROUPS_PER_CHUNK)].reshape(static_rows, D_MODEL)
    acc_ref[...] = shared + jnp.dot(pt_all, y_static, preferred_element_type=F32)

    def chunk(c, carry):
        g0 = pl.multiple_of(c * GROUPS_PER_CHUNK, GROUPS_PER_CHUNK)
        y_c = yl_ref[slot, pl.ds(g0, GROUPS_PER_CHUNK)].reshape(CHUNK_ROWS, D_MODEL)
        acc_ref[...] += jnp.dot(select_t((c * CHUNK_ROWS).astype(F32)), y_c, preferred_element_type=F32)
        return carry

    lax.fori_loop(STATIC_CHUNKS, n_c, chunk, 0)
    o_ref[...] = x1_ref[...] + mod_ref[5:6, :] * acc_ref[...]


def _combine(plan, y, rktok, h2aug, x1, mod_l, p):
    tiles_per_seq = SEQ // TT
    grid_spec = pltpu.PrefetchScalarGridSpec(
        num_scalar_prefetch=2,
        grid=(N_TILES,),
        in_specs=[pl.BlockSpec(memory_space=pl.ANY),
                  pl.BlockSpec((TT, N_EXPERTS), lambda i, *_: (i, 0)),
                  pl.BlockSpec((None, 2, N_EXPERTS), lambda i, *_: (i, 0, 0)),
                  pl.BlockSpec((None, N_EXPERTS, 2), lambda i, *_: (i, 0, 0)),
                  pl.BlockSpec((TT, D_MODEL), lambda i, *_: (i, 0)),
                  pl.BlockSpec((TT, D_MODEL), lambda i, *_: (i, 0)),
                  pl.BlockSpec((None, 6, D_MODEL), lambda i, *_: (i // tiles_per_seq, 0, 0)),
                  pl.BlockSpec((D_MODEL, 2 * EXPERT_FF), lambda i, *_: (0, 0)),
                  pl.BlockSpec((EXPERT_FF, D_MODEL), lambda i, *_: (0, 0))],
        out_specs=pl.BlockSpec((TT, D_MODEL), lambda i, *_: (i, 0)),
        scratch_shapes=[pltpu.VMEM((2, MAX_GROUPS, SEG_ALIGN, D_MODEL), BF16),
                        pltpu.VMEM((TT, D_MODEL), F32),
                        pltpu.SemaphoreType.DMA((2,))],
    )
    return pl.pallas_call(
        _combine_kernel,
        out_shape=jax.ShapeDtypeStruct((N_TOK, D_MODEL), F32),
        grid_spec=grid_spec,
        compiler_params=_cparams(("arbitrary",)),
        name="moe_combine",
    )(plan["n_chunks"], plan["group_src"], y, rktok, plan["seg_row"], plan["seg_col"], h2aug, x1,
      mod_l, p["w_sh_gu"], p["w_sh_down"])


def _moe(h2aug, rkt, rktok, cnt, x1, mod_l, p):
    plan = _moe_plan(cnt[:, :, 0])
    xs = _dispatch(plan, h2aug, rkt)
    y = _expert_ffn(plan, xs, p)
    return _combine(plan, y, rktok, h2aug, x1, mod_l, p)


def _pad_heads(w, per_head):
    k = w.shape[0]
    w = w.reshape(k, MLA_HEADS, per_head)
    w = jnp.pad(w, ((0, 0), (0, 0), (0, HEAD_PAD - per_head)))
    return w.reshape(k, MLA_HEADS * HEAD_PAD)


def _layer_params(l, a):
    w_in = a["w_in"][l]
    half = QK_ROPE // 2
    pe_cols = jnp.zeros((D_MODEL, LANES), F32).at[:, QK_NOPE:QK_DIM].set(w_in[:, O_PE:])
    pe_cols = pe_cols.at[:, QK_DIM:QK_DIM + half].set(w_in[:, O_PE:O_PE + half])
    w_in_p = jnp.concatenate([w_in[:, :O_PE], pe_cols], axis=1).astype(BF16)
    w_q = a["w_q_up"][l].reshape(Q_LORA, MLA_HEADS, QK_DIM)
    w_q = jnp.concatenate([w_q, w_q[:, :, QK_NOPE:QK_NOPE + half],
                           jnp.zeros((Q_LORA, MLA_HEADS, HEAD_PAD - QK_DIM - half), F32)], axis=-1)
    w_kv = a["w_kv_up"][l].reshape(KV_LORA, MLA_HEADS, QK_NOPE + V_DIM)
    w_k = _pad_heads(w_kv[:, :, :QK_NOPE].reshape(KV_LORA, MLA_HEADS * QK_NOPE), QK_NOPE)
    w_v = w_kv[:, :, QK_NOPE:].reshape(KV_LORA, MLA_HEADS * V_DIM)
    blocks = TT // GM_BLOCK
    eye = jnp.eye(blocks, dtype=F32)
    ws_big = jnp.einsum("ab,gts->gatbs", eye, a["gm_ws"][l]).reshape(GM_GROUPS, TT, TT)
    bs_tile = jnp.tile(jnp.repeat(a["gm_bs"][l].T, GM_CH, axis=1), (TM // GM_BLOCK, 1))
    pad96 = lambda g: jnp.concatenate(
        [g, g[QK_NOPE:QK_NOPE + half], jnp.zeros((LANES - QK_DIM - half,), F32)]).reshape(1, LANES)
    return {
        "norm_mix_g": a["norm_mix_g"][l].reshape(1, D_MODEL),
        "norm_ffn_g": a["norm_ffn_g"][l].reshape(1, D_MODEL),
        "w_in": w_in_p,
        "gm_ln_g": a["gm_ln_g"][l].reshape(1, GM_WIDTH),
        "gm_ln_b": a["gm_ln_b"][l].reshape(1, GM_WIDTH),
        "gm_ws": ws_big.astype(BF16),
        "gm_bs": bs_tile,
        "q_a_g": a["q_a_g"][l].reshape(1, Q_LORA),
        "w_q_up": w_q.reshape(Q_LORA, MLA_HEADS * HEAD_PAD).astype(BF16),
        "kv_a_g": a["kv_a_g"][l].reshape(1, KV_LORA),
        "w_k_up": w_k.astype(BF16),
        "w_v_t": jnp.pad(w_v.T.reshape(MLA_HEADS, V_DIM, KV_LORA), ((0, 0), (0, V_AUG - V_DIM), (0, 0))
                         ).reshape(MLA_HEADS * V_AUG, KV_LORA).astype(BF16),
        "v_ones": jnp.tile((jnp.arange(V_AUG) == V_DIM).astype(F32), MLA_HEADS).reshape(-1, 1),
        "q_norm_g": pad96(a["q_norm_g"][l]),
        "k_norm_g": pad96(a["k_norm_g"][l]),
        "out_norm_gm": a["out_norm_g"][l, :GM_WIDTH].reshape(1, GM_WIDTH),
        "out_norm_mla": a["out_norm_g"][l, GM_WIDTH:].reshape(1, MLA_WIDTH),
        "w_out": a["w_out"][l].astype(BF16),
        "w_router_t": a["w_router"][l].T,
        "router_bias": a["router_bias"][l].reshape(N_EXPERTS, 1),
        "layer": l,
        "w_exp_gate": a["w_exp_gate"],
        "w_exp_up": a["w_exp_up"],
        "w_exp_down": a["w_exp_down"],
        "w_sh_gu": jnp.concatenate([a["w_sh_gate"][l], a["w_sh_up"][l]], axis=-1).astype(BF16),
        "w_sh_down": a["w_sh_down"][l].astype(BF16),
    }


def kernel(x, c, positions, norm_mix_g, norm_ffn_g, w_ada, b_ada, w_in, gm_ln_g, gm_ln_b, gm_ws, gm_bs, q_a_g, w_q_up, kv_a_g, w_kv_up, q_norm_g, k_norm_g, out_norm_g, w_out, w_router, router_bias, w_exp_gate, w_exp_up, w_exp_down, w_sh_gate, w_sh_up, w_sh_down):
    a = dict(norm_mix_g=norm_mix_g, norm_ffn_g=norm_ffn_g, w_in=w_in, gm_ln_g=gm_ln_g,
             gm_ln_b=gm_ln_b, gm_ws=gm_ws, gm_bs=gm_bs, q_a_g=q_a_g, w_q_up=w_q_up,
             kv_a_g=kv_a_g, w_kv_up=w_kv_up, q_norm_g=q_norm_g, k_norm_g=k_norm_g,
             out_norm_g=out_norm_g, w_out=w_out, w_router=w_router, router_bias=router_bias,
             w_exp_gate=w_exp_gate, w_exp_up=w_exp_up, w_exp_down=w_exp_down,
             w_sh_gate=w_sh_gate, w_sh_up=w_sh_up, w_sh_down=w_sh_down)
    cos_t, sin_t = _rope_tables(positions)
    mod = _ada_mod(c, w_ada, b_ada).reshape(DEPTH, BATCH, 6, D_MODEL)
    xt = x.reshape(N_TOK, D_MODEL)
    for l in range(DEPTH):
        p = _layer_params(l, a)
        gm, q, k, v = _mix_in(xt, mod[l], p, cos_t, sin_t)
        mla = _attention(q, k, v).reshape(N_TOK, MLA_WIDTH)
        x1, h2aug, rkt, rktok, cnt = _mix_out(xt, gm, mla, mod[l], p)
        xt = _moe(h2aug, rkt, rktok, cnt, x1, mod[l], p)
    return xt.reshape(BATCH, SEQ, D_MODEL)
```

```python
import functools

import jax
import jax.numpy as jnp
from jax import lax
from jax.experimental import pallas as pl
from jax.experimental.pallas import tpu as pltpu

D_MODEL = 1024
BATCH = 8
SEQ = 4096
DEPTH = 2
N_TOK = BATCH * SEQ
CHUNK = 64
GM_WIDTH = 512
GM_BLOCK = 128
GM_GROUPS = 4
GM_CH = 128
MLA_HEADS = 8
QK_NOPE = 64
QK_ROPE = 32
QK_DIM = 96
V_DIM = 64
MLA_WIDTH = 512
Q_LORA = 384
KV_LORA = 256
ROPE_THETA = 10000.0
N_EXPERTS = 64
TOP_K = 8
N_GROUPS = 8
GROUP_SIZE = N_EXPERTS // N_GROUPS
TOPK_GROUPS = 4
EXPERT_FF = 256
ROUTED_SCALE = 2.5
EPS = 1e-6

LANES = 128
HEAD_PAD = LANES
IN_COLS_PAD = 2 * GM_WIDTH + Q_LORA + KV_LORA + LANES
O_Q = 2 * GM_WIDTH
O_KV = O_Q + Q_LORA
O_PE = O_KV + KV_LORA

TT = 256
TM = 512
TQ = 512
TK = 256
ATT_HEADS = 8
V_AUG = 80
N_TILES = N_TOK // TT
H_AUG = D_MODEL + LANES
SEG_ALIGN = 16
CHUNK_ROWS = 256
LOCAL_ROWS = TT * TOP_K + N_EXPERTS * SEG_ALIGN
MAX_GROUPS = LOCAL_ROWS // SEG_ALIGN
TM_FFN = 512
FFN_RING = 3
SORTED_ROWS = -(-(N_TOK * TOP_K + N_TILES * N_EXPERTS * (SEG_ALIGN - 1)
                  + N_EXPERTS * (TM_FFN - 1)) // TM_FFN) * TM_FFN
FFN_TILES = SORTED_ROWS // TM_FFN
GROUPS_PER_CHUNK = CHUNK_ROWS // SEG_ALIGN
TRASH_ROWS = 2 * LOCAL_ROWS
STATIC_CHUNKS = 10
VMEM_LIMIT = 56 * 1024 * 1024

F32 = jnp.float32
BF16 = jnp.bfloat16


def _cparams(sem, flags=None):
    return pltpu.CompilerParams(dimension_semantics=sem, vmem_limit_bytes=VMEM_LIMIT, flags=flags)


def _rms_scale(v, width):
    return lax.rsqrt(jnp.sum(v * v, axis=-1, keepdims=True) * (1.0 / width) + EPS)


def _rope_kernel(pos_ref, inv_ref, cos_ref, sin_ref):
    ang = pos_ref[...].astype(F32) * inv_ref[...]
    lane = lax.broadcasted_iota(jnp.int32, ang.shape, 1)
    c = jnp.cos(ang)
    s = jnp.sin(ang)
    cos_ref[...] = jnp.where(lane < QK_NOPE, 1.0, jnp.where(lane < QK_DIM, c, 0.0))
    sin_ref[...] = jnp.where((lane >= QK_NOPE) & (lane < QK_NOPE + 16), -s,
                             jnp.where((lane >= QK_NOPE + 16) & (lane < QK_DIM), s, 0.0))


def _rope_tables(positions):
    inv_freq = 1.0 / (ROPE_THETA ** (jnp.arange(0, QK_ROPE, 2, dtype=F32) / QK_ROPE))
    inv = jnp.zeros((1, LANES), F32)
    inv = inv.at[0, QK_NOPE:QK_NOPE + 16].set(inv_freq)
    inv = inv.at[0, QK_NOPE + 16:QK_DIM].set(inv_freq)
    pos = positions.reshape(N_TOK, 1)
    tile = 2048
    return pl.pallas_call(
        _rope_kernel,
        out_shape=(jax.ShapeDtypeStruct((N_TOK, LANES), F32),
                   jax.ShapeDtypeStruct((N_TOK, LANES), F32)),
        grid=(N_TOK // tile,),
        in_specs=[pl.BlockSpec((tile, 1), lambda i: (i, 0)),
                  pl.BlockSpec((1, LANES), lambda i: (0, 0))],
        out_specs=(pl.BlockSpec((tile, LANES), lambda i: (i, 0)),
                   pl.BlockSpec((tile, LANES), lambda i: (i, 0))),
        compiler_params=_cparams(("arbitrary",)),
        name="rope_tables",
    )(pos, inv)


def _ada_kernel(c_ref, w_ref, b_ref, o_ref):
    c = c_ref[...]
    ca = c * jax.nn.sigmoid(c)
    o_ref[...] = jnp.dot(ca, w_ref[...], precision=lax.Precision.HIGHEST,
                         preferred_element_type=F32) + b_ref[...]


def _ada_mod(c, w_ada, b_ada):
    return pl.pallas_call(
        _ada_kernel,
        out_shape=jax.ShapeDtypeStruct((DEPTH, BATCH, 6 * D_MODEL), F32),
        grid=(DEPTH, 6),
        in_specs=[pl.BlockSpec((BATCH, D_MODEL), lambda l, j: (0, 0)),
                  pl.BlockSpec((None, D_MODEL, D_MODEL), lambda l, j: (l, 0, j)),
                  pl.BlockSpec((None, 1, D_MODEL), lambda l, j: (l, 0, j))],
        out_specs=pl.BlockSpec((None, BATCH, D_MODEL), lambda l, j: (l, 0, j)),
        compiler_params=_cparams(("arbitrary", "arbitrary")),
        name="ada_mod",
    )(c, w_ada, b_ada.reshape(DEPTH, 1, 6 * D_MODEL))


def _rope_rot(t, cos_t, sin_t):
    return t * cos_t + pltpu.roll(t, LANES - 16, 1) * sin_t


def _mix_in_kernel(x_ref, mod_ref, ng_ref, win_ref, lng_ref, lnb_ref, ws_ref, bs_ref,
                   qag_ref, wq_ref, kvg_ref, wkv_ref, wvt_ref, vone_ref, qng_ref, kng_ref, og_ref,
                   cos_ref, sin_ref,
                   gm_ref, q_ref, k_ref, vt_ref):
    x = x_ref[...]
    sh1 = mod_ref[0:1, :]
    sc1 = mod_ref[1:2, :]
    h = x * _rms_scale(x, D_MODEL) * ng_ref[...]
    h = h * (1.0 + sc1) + sh1
    proj = jnp.dot(h.astype(BF16), win_ref[...], preferred_element_type=F32)

    uv_pre = proj[:, :2 * GM_WIDTH]
    uv = 0.5 * uv_pre * (1.0 + lax.erf(uv_pre * (2.0 ** -0.5)))
    row = lax.broadcasted_iota(jnp.int32, (TT, TT), 0)
    col = lax.broadcasted_iota(jnp.int32, (TT, TT), 1)
    causal = (col <= row) & ((col // GM_BLOCK) == (row // GM_BLOCK))
    groups = range(GM_GROUPS)
    gsl = [slice(g * GM_CH, (g + 1) * GM_CH) for g in groups]
    vg = [uv[:, GM_WIDTH + g * GM_CH:GM_WIDTH + (g + 1) * GM_CH] for g in groups]
    mu = [jnp.mean(vg[g], axis=-1, keepdims=True) for g in groups]
    xc = [vg[g] - mu[g] for g in groups]
    var = [jnp.mean(xc[g] * xc[g], axis=-1, keepdims=True) for g in groups]
    vln = [(xc[g] * lax.rsqrt(var[g] + EPS) * lng_ref[:, gsl[g]] + lnb_ref[:, gsl[g]]).astype(BF16)
           for g in groups]
    ws = [jnp.where(causal, ws_ref[g], jnp.zeros((), BF16)) for g in groups]
    gate = [jnp.concatenate([jnp.dot(ws[g], vln[g][s * TT:(s + 1) * TT], preferred_element_type=F32)
                             for s in range(TM // TT)], axis=0) for g in groups]
    outs = [uv[:, gsl[g]] * (gate[g] + bs_ref[:, gsl[g]]) for g in groups]
    sq = [jnp.sum(outs[g] * outs[g], axis=-1, keepdims=True) for g in groups]
    rs = lax.rsqrt((sq[0] + sq[1] + sq[2] + sq[3]) * (1.0 / GM_WIDTH) + EPS)
    for g in groups:
        gm_ref[:, gsl[g]] = (outs[g] * rs * og_ref[:, gsl[g]]).astype(BF16)

    cos_t = cos_ref[...]
    sin_t = sin_ref[...]
    lane = lax.broadcasted_iota(jnp.int32, (TM, LANES), 1)
    ql = proj[:, O_Q:O_KV]
    qn = ql * _rms_scale(ql, Q_LORA) * qag_ref[...]
    qa = jnp.dot(qn.astype(BF16), wq_ref[...], preferred_element_type=F32)
    kl = proj[:, O_KV:O_PE]
    kn = kl * _rms_scale(kl, KV_LORA) * kvg_ref[...]
    kva = jnp.dot(kn.astype(BF16), wkv_ref[...], preferred_element_type=F32)
    pe = proj[:, O_PE:O_PE + LANES]
    real = lane < QK_DIM
    pe_ssq = jnp.sum(jnp.where(real, pe * pe, 0.0), axis=-1, keepdims=True)
    pe_rot = _rope_rot(pe * kng_ref[...], cos_t, sin_t)
    qscale = (QK_DIM ** -0.5) * 1.4426950408889634
    v_t = lax.dot_general(wvt_ref[...], kn.astype(BF16), (((1,), (1,)), ((), ())),
                          preferred_element_type=F32) + vone_ref[...]
    vt_ref[...] = v_t.reshape(MLA_HEADS, V_AUG, TM).astype(BF16)
    heads = range(MLA_HEADS)
    qh = [qa[:, hd * HEAD_PAD:(hd + 1) * HEAD_PAD] for hd in heads]
    kh = [kva[:, hd * HEAD_PAD:(hd + 1) * HEAD_PAD] for hd in heads]
    q_ssq = [jnp.sum(jnp.where(real, qh[hd] * qh[hd], 0.0), axis=-1, keepdims=True) for hd in heads]
    k_ssq = [jnp.sum(kh[hd] * kh[hd], axis=-1, keepdims=True) for hd in heads]
    qg = [qh[hd] * lax.rsqrt(q_ssq[hd] * (1.0 / QK_DIM) + EPS) * qng_ref[...] for hd in heads]
    q_sw = [pltpu.roll(qg[hd], LANES - 16, 1) for hd in heads]
    for hd in heads:
        q_ref[hd] = ((qg[hd] * cos_t + q_sw[hd] * sin_t) * qscale).astype(BF16)
    for hd in heads:
        krs = lax.rsqrt((k_ssq[hd] + pe_ssq) * (1.0 / QK_DIM) + EPS)
        k_ref[hd] = ((kh[hd] * kng_ref[...] + pe_rot) * krs).astype(BF16)


def _mix_in(x, mod_l, p, cos_t, sin_t):
    tiles_per_seq = SEQ // TM
    full = lambda shape: pl.BlockSpec(shape, lambda i: (0,) * len(shape))
    head_spec = pl.BlockSpec((None, MLA_HEADS, TM, HEAD_PAD),
                             lambda i: (i // tiles_per_seq, 0, i % tiles_per_seq, 0))
    head_shape = jax.ShapeDtypeStruct((BATCH, MLA_HEADS, SEQ, HEAD_PAD), BF16)
    return pl.pallas_call(
        _mix_in_kernel,
        out_shape=(jax.ShapeDtypeStruct((N_TOK, GM_WIDTH), BF16), head_shape, head_shape,
                   jax.ShapeDtypeStruct((BATCH, MLA_HEADS, V_AUG, SEQ), BF16)),
        grid=(N_TOK // TM,),
        in_specs=[pl.BlockSpec((TM, D_MODEL), lambda i: (i, 0)),
                  pl.BlockSpec((None, 6, D_MODEL), lambda i: (i // tiles_per_seq, 0, 0)),
                  full((1, D_MODEL)),
                  full((D_MODEL, IN_COLS_PAD)),
                  full((1, GM_WIDTH)), full((1, GM_WIDTH)),
                  full((GM_GROUPS, TT, TT)), full((TM, GM_WIDTH)),
                  full((1, Q_LORA)), full((Q_LORA, MLA_HEADS * HEAD_PAD)),
                  full((1, KV_LORA)), full((KV_LORA, MLA_HEADS * HEAD_PAD)),
                  full((MLA_HEADS * V_AUG, KV_LORA)), full((MLA_HEADS * V_AUG, 1)),
                  full((1, LANES)), full((1, LANES)), full((1, GM_WIDTH)),
                  pl.BlockSpec((TM, LANES), lambda i: (i, 0)),
                  pl.BlockSpec((TM, LANES), lambda i: (i, 0))],
        out_specs=(pl.BlockSpec((TM, GM_WIDTH), lambda i: (i, 0)), head_spec, head_spec,
                   pl.BlockSpec((None, MLA_HEADS, V_AUG, TM),
                                lambda i: (i // tiles_per_seq, 0, 0, i % tiles_per_seq))),
        compiler_params=_cparams(("arbitrary",)),
        name="mix_in",
    )(x, mod_l, p["norm_mix_g"], p["w_in"], p["gm_ln_g"], p["gm_ln_b"], p["gm_ws"], p["gm_bs"],
      p["q_a_g"], p["w_q_up"], p["kv_a_g"], p["w_k_up"], p["w_v_t"], p["v_ones"], p["q_norm_g"],
      p["k_norm_g"],
      p["out_norm_gm"], cos_t, sin_t)


def _attn_kernel(q_ref, k_ref, vt_ref, o_ref):
    qi = pl.program_id(2)
    krow = lax.broadcasted_iota(jnp.int32, (TK, TQ), 0)
    qcol = lax.broadcasted_iota(jnp.int32, (TK, TQ), 1)
    nt = (((1,), (1,)), ((), ()))
    per_q = TQ // TK

    def tile(j, carry, diag):
        start = pl.multiple_of(j * TK, TK)
        heads = range(ATT_HEADS)
        s = [lax.dot_general(k_ref[hh, pl.ds(start, TK), :], q_ref[hh], nt,
                             preferred_element_type=F32) for hh in heads]
        if diag is not None:
            allowed = ((krow + diag * TK) // CHUNK) <= (qcol // CHUNK)
            s = [jnp.where(allowed, sh, -jnp.inf) for sh in s]
        m_new = [jnp.maximum(carry[hh][0], jnp.max(s[hh], axis=0, keepdims=True)) for hh in heads]
        p = [jnp.exp2(s[hh] - m_new[hh]) for hh in heads]
        alpha = [jnp.exp2(carry[hh][0] - m_new[hh]) for hh in heads]
        pv = [jnp.dot(vt_ref[hh, :, pl.ds(start, TK)], p[hh].astype(BF16),
                      preferred_element_type=F32) for hh in heads]
        return tuple((m_new[hh], alpha[hh] * carry[hh][1] + pv[hh]) for hh in heads)

    init = tuple((jnp.full((1, TQ), -jnp.inf, F32), jnp.zeros((V_AUG, TQ), F32))
                 for _ in range(ATT_HEADS))
    carry = lax.fori_loop(0, qi * per_q, functools.partial(tile, diag=None), init)
    for t in range(per_q):
        carry = tile(qi * per_q + t, carry, t)
    out_t = jnp.concatenate([acc[:V_DIM] / acc[V_DIM:V_DIM + 1] for (_, acc) in carry], axis=0)
    o_ref[...] = out_t.T


def _attention(q, k, vt):
    groups = MLA_HEADS // ATT_HEADS
    qspec = pl.BlockSpec((None, ATT_HEADS, TQ, HEAD_PAD), lambda b, hp, i: (b, hp, i, 0))
    kspec = pl.BlockSpec((None, ATT_HEADS, SEQ, HEAD_PAD), lambda b, hp, i: (b, hp, 0, 0))
    vspec = pl.BlockSpec((None, ATT_HEADS, V_AUG, SEQ), lambda b, hp, i: (b, hp, 0, 0))
    return pl.pallas_call(
        _attn_kernel,
        out_shape=jax.ShapeDtypeStruct((BATCH, SEQ, MLA_WIDTH), F32),
        grid=(BATCH, groups, SEQ // TQ),
        in_specs=[qspec, kspec, vspec],
        out_specs=pl.BlockSpec((None, TQ, ATT_HEADS * V_DIM), lambda b, hp, i: (b, i, hp)),
        compiler_params=_cparams(("arbitrary", "arbitrary", "arbitrary")),
        name="mla_attention",
    )(q, k, vt)


def _route(logits_t, bias_col):
    t = logits_t.shape[1]
    scores = jax.nn.sigmoid(logits_t)
    biased = scores + bias_col
    neg = -jnp.inf
    b3 = biased.reshape(N_GROUPS, GROUP_SIZE, t)
    m1 = jnp.max(b3, axis=1, keepdims=True)
    n_max = jnp.sum((b3 == m1).astype(F32), axis=1, keepdims=True)
    m2 = jnp.max(jnp.where(b3 < m1, b3, neg), axis=1, keepdims=True)
    gs = (m1 + jnp.where(n_max >= 2.0, m1, m2)).reshape(N_GROUPS, t)
    gidx = lax.broadcasted_iota(jnp.int32, (N_GROUPS, t), 0)
    grank = jnp.zeros((N_GROUPS, t), F32)
    for g in range(N_GROUPS):
        other = gs[g:g + 1, :]
        ahead = (other > gs) | ((other == gs) & (gidx > g))
        grank = grank + ahead.astype(F32)
    gsel = grank < float(TOPK_GROUPS)
    emask = jnp.broadcast_to(gsel.reshape(N_GROUPS, 1, t), (N_GROUPS, GROUP_SIZE, t)).reshape(N_EXPERTS, t)
    ms = jnp.where(emask, biased, neg)
    eidx = lax.broadcasted_iota(jnp.int32, (N_EXPERTS, t), 0).astype(F32)
    picked = jnp.zeros((N_EXPERTS, t), F32)
    for _ in range(TOP_K):
        top = jnp.max(ms, axis=0, keepdims=True)
        first = jnp.min(jnp.where(ms == top, eidx, float(N_EXPERTS)), axis=0, keepdims=True)
        hit = eidx == first
        picked = jnp.where(hit, 1.0, picked)
        ms = jnp.where(hit, neg, ms)
    sel = picked > 0.5
    w = jnp.where(sel, scores, 0.0)
    denom = jnp.sum(w, axis=0, keepdims=True)
    return (w / denom) * ROUTED_SCALE, sel


def _mix_out_kernel(x_ref, gm_ref, mla_ref, mod_ref, og_ref, wout_ref, fg_ref, wr_ref, rb_ref,
                    x1_ref, h2_ref, rkt_ref, rktok_ref, cnt_ref):
    mla = mla_ref[...]
    mla_n = mla * _rms_scale(mla, MLA_WIDTH) * og_ref[...]
    mixed = jnp.concatenate([gm_ref[...], mla_n.astype(BF16)], axis=-1)
    y = jnp.dot(mixed, wout_ref[...], preferred_element_type=F32)
    g1 = mod_ref[2:3, :]
    sh2 = mod_ref[3:4, :]
    sc2 = mod_ref[4:5, :]
    x1 = x_ref[...] + g1 * y
    x1_ref[...] = x1
    h2 = x1 * _rms_scale(x1, D_MODEL) * fg_ref[...]
    h2 = h2 * (1.0 + sc2) + sh2
    logits_t = lax.dot_general(wr_ref[...], h2, (((1,), (1,)), ((), ())),
                               precision=lax.Precision.HIGHEST, preferred_element_type=F32)
    comb, sel = _route(logits_t, rb_ref[...])
    before = (lax.broadcasted_iota(jnp.int32, (TT, TT), 0)
              < lax.broadcasted_iota(jnp.int32, (TT, TT), 1))
    self = jnp.where(sel, 1.0, 0.0)
    before_b = jnp.where(before, 1.0, 0.0).astype(BF16)
    rank = jnp.concatenate(
        [jnp.dot(self[:, s * TT:(s + 1) * TT].astype(BF16), before_b, preferred_element_type=F32)
         for s in range(TM // TT)], axis=1)
    rk = jnp.where(sel, rank, -1.0)
    rkt_ref[...] = rk.astype(BF16)
    rktok_ref[...] = rk.T.astype(BF16)
    for s in range(TM // TT):
        cnt_ref[s] = jnp.broadcast_to(
            jnp.sum(self[:, s * TT:(s + 1) * TT], axis=1, keepdims=True), (N_EXPERTS, LANES))
    comb_tok = comb.T
    hi = comb_tok.astype(BF16)
    lo = (comb_tok - hi.astype(F32)).astype(BF16)
    h2_ref[:, :D_MODEL] = h2.astype(BF16)
    h2_ref[:, D_MODEL:] = jnp.concatenate([hi, lo], axis=-1)


def _mix_out(x, gm, mla, mod_l, p):
    tiles_per_seq = SEQ // TM
    full = lambda shape: pl.BlockSpec(shape, lambda i: (0,) * len(shape))
    tok = lambda width: pl.BlockSpec((TM, width), lambda i: (i, 0))
    return pl.pallas_call(
        _mix_out_kernel,
        out_shape=(jax.ShapeDtypeStruct((N_TOK, D_MODEL), F32),
                   jax.ShapeDtypeStruct((N_TOK, H_AUG), BF16),
                   jax.ShapeDtypeStruct((N_EXPERTS, N_TOK), BF16),
                   jax.ShapeDtypeStruct((N_TOK, N_EXPERTS), BF16),
                   jax.ShapeDtypeStruct((N_TILES, N_EXPERTS, LANES), F32)),
        grid=(N_TOK // TM,),
        in_specs=[tok(D_MODEL), tok(GM_WIDTH), tok(MLA_WIDTH),
                  pl.BlockSpec((None, 6, D_MODEL), lambda i: (i // tiles_per_seq, 0, 0)),
                  full((1, MLA_WIDTH)), full((D_MODEL, D_MODEL)), full((1, D_MODEL)),
                  full((N_EXPERTS, D_MODEL)), full((N_EXPERTS, 1))],
        out_specs=(tok(D_MODEL), tok(H_AUG), pl.BlockSpec((N_EXPERTS, TM), lambda i: (0, i)),
                   tok(N_EXPERTS), pl.BlockSpec((TM // TT, N_EXPERTS, LANES), lambda i: (i, 0, 0))),
        compiler_params=_cparams(("arbitrary",)),
        name="mix_out_route",
    )(x, gm, mla, mod_l, p["out_norm_mla"], p["w_out"], p["norm_ffn_g"], p["w_router_t"],
      p["router_bias"])


def _moe_plan(cnt):
    i32 = jnp.int32
    cnt = cnt.astype(i32)
    padlen = (cnt + SEG_ALIGN - 1) // SEG_ALIGN * SEG_ALIGN
    lend = jnp.cumsum(padlen, axis=1)
    lstart = lend - padlen
    ltot = lend[:, -1]
    region = jnp.sum(padlen, axis=0)
    region_pad = (region + TM_FFN - 1) // TM_FFN * TM_FFN
    base = jnp.cumsum(region_pad) - region_pad
    pos = base[None, :] + jnp.cumsum(padlen, axis=0) - padlen
    gidx = jnp.arange(MAX_GROUPS, dtype=i32)
    grow = gidx * SEG_ALIGN
    in_seg = (grow[None, :, None] >= lstart[:, None, :]) & (grow[None, :, None] < lend[:, None, :])
    shift = jnp.sum(jnp.where(in_seg, (pos - lstart)[:, None, :], 0), axis=-1)
    n_groups = ltot // SEG_ALIGN
    valid_g = gidx[None, :] < n_groups[:, None]
    real_dst = (shift + grow[None, :]) // SEG_ALIGN
    trash_base = SORTED_ROWS // SEG_ALIGN
    trash = trash_base + (jnp.arange(N_TILES, dtype=i32)[:, None] % 2) * MAX_GROUPS + gidx[None, :]
    n_chunks = (n_groups + GROUPS_PER_CHUNK - 1) // GROUPS_PER_CHUNK
    n_move = jnp.maximum(n_chunks, STATIC_CHUNKS)
    group_dst = jnp.where(valid_g, real_dst, trash)
    group_src = jnp.where(valid_g, real_dst, 0)
    lead = (trash_base + MAX_GROUPS + gidx)[None, :]
    static_c = jnp.full((1,), STATIC_CHUNKS, i32)
    tile_end = jnp.cumsum(region_pad) // TM_FFN
    n_used = tile_end[-1]
    tj = jnp.arange(FFN_TILES, dtype=i32)
    in_exp = (tj[:, None] * TM_FFN >= base[None, :]) & (tj[:, None] * TM_FFN < (base + region_pad)[None, :])
    t_exp = jnp.sum(jnp.where(in_exp, jnp.arange(N_EXPERTS, dtype=i32)[None, :], 0), axis=-1)
    t_valid = jnp.sum(jnp.where(in_exp, jnp.clip((base + region)[None, :] - tj[:, None] * TM_FFN, 0, TM_FFN), 0),
                      axis=-1)
    has_rows = region_pad > 0
    run_of_exp = jnp.cumsum(has_rows.astype(i32)) - 1
    eidx = jnp.arange(N_EXPERTS, dtype=i32)
    t_run = jnp.sum(jnp.where(in_exp, run_of_exp[None, :], 0), axis=-1)
    run_exp = jnp.sum(jnp.where(has_rows[None, :] & (run_of_exp[None, :] == eidx[:, None]), eidx[None, :], 0),
                      axis=-1)
    lstart_f = lstart.astype(F32)
    lend_f = lend.astype(F32)
    return {
        "t_run": t_run.astype(i32),
        "run_exp": run_exp.astype(i32),
        "n_runs": jnp.sum(has_rows.astype(i32)).reshape(1),
        "move_out": jnp.concatenate([static_c, n_move]).astype(i32),
        "move_in": jnp.concatenate([n_move, static_c]).astype(i32),
        "group_dst": jnp.concatenate([lead, group_dst], axis=0).reshape(-1).astype(i32),
        "group_src": jnp.concatenate([group_src, jnp.zeros((1, MAX_GROUPS), i32)], axis=0
                                     ).reshape(-1).astype(i32),
        "seg_row": jnp.stack([lstart_f, lend_f], axis=1),
        "seg_col": jnp.stack([lstart_f, lend_f], axis=2),
        "n_used": n_used.reshape(1).astype(i32),
        "t_exp": t_exp.astype(i32),
        "t_valid": t_valid.astype(i32),
    }


def _dispatch_kernel(nm_ref, gd_ref, h_ref, rkt_ref, srow_ref, scol_ref, xs_ref, loc_ref, sem):
    i = pl.program_id(0)
    slot = i % 2
    pslot = 1 - slot

    def wait_chunks(n, slot_):
        def body(c, carry):
            pltpu.make_async_copy(loc_ref.at[slot_, pl.ds(0, GROUPS_PER_CHUNK)],
                                  xs_ref.at[pl.ds(0, GROUPS_PER_CHUNK)], sem.at[slot_]).wait()
            return carry
        lax.fori_loop(0, n, body, 0)

    def issue_chunk(entry, slot_, c):
        for k in range(GROUPS_PER_CHUNK):
            g = c * GROUPS_PER_CHUNK + k
            pltpu.make_async_copy(loc_ref.at[slot_, g], xs_ref.at[gd_ref[entry * MAX_GROUPS + g]],
                                  sem.at[slot_]).start()

    @pl.when(i == 0)
    def _():
        loc_ref[1] = jnp.zeros(loc_ref.shape[1:], BF16)

    ls_row = srow_ref[0:1, :]
    le_row = srow_ref[1:2, :]
    ls_rep = jnp.broadcast_to(scol_ref[:, 0:1], (N_EXPERTS, LANES)).astype(BF16)
    rk_ls = jnp.concatenate([rkt_ref[...], ls_rep], axis=1)
    h = h_ref[...]
    n_c = nm_ref[i + 1]
    rio_e0 = lax.broadcasted_iota(jnp.int32, (CHUNK_ROWS, N_EXPERTS), 0).astype(F32)
    rio_t0 = lax.broadcasted_iota(jnp.int32, (CHUNK_ROWS, TT), 0).astype(F32)

    def select(r0f):
        rio_e = rio_e0 + r0f
        g = jnp.where((rio_e >= ls_row) & (rio_e < le_row), 1.0, 0.0).astype(BF16)
        both = jnp.dot(g, rk_ls, preferred_element_type=F32)
        want = (rio_t0 + r0f) - jnp.concatenate([both[:, TT:]] * (TT // LANES), axis=1)
        return jnp.where(both[:, :TT] == want, 1.0, 0.0).astype(BF16)

    def emit(g0, psel):
        loc_ref[slot, pl.ds(g0, GROUPS_PER_CHUNK)] = jnp.dot(
            psel, h, preferred_element_type=F32).astype(BF16).reshape(GROUPS_PER_CHUNK, SEG_ALIGN, H_AUG)

    psels = []
    for c in range(STATIC_CHUNKS):
        psels.append(select(float(c * CHUNK_ROWS)))
        issue_chunk(i, pslot, c)

    def issue_more(c, carry):
        issue_chunk(i, pslot, c)
        return carry

    lax.fori_loop(STATIC_CHUNKS, nm_ref[i], issue_more, 0)

    @pl.when(i >= 1)
    def _():
        wait_chunks(nm_ref[i - 1], slot)

    for c in range(STATIC_CHUNKS):
        emit(c * GROUPS_PER_CHUNK, psels[c])

    def chunk(c, carry):
        emit(pl.multiple_of(c * GROUPS_PER_CHUNK, GROUPS_PER_CHUNK), select((c * CHUNK_ROWS).astype(F32)))
        return carry

    lax.fori_loop(STATIC_CHUNKS, n_c, chunk, 0)

    @pl.when(i == N_TILES - 1)
    def _():
        def issue_last(c, carry):
            issue_chunk(i + 1, slot, c)
            return carry

        lax.fori_loop(0, n_c, issue_last, 0)
        wait_chunks(nm_ref[i], pslot)
        wait_chunks(n_c, slot)


def _dispatch(plan, h2aug, rkt):
    grid_spec = pltpu.PrefetchScalarGridSpec(
        num_scalar_prefetch=2,
        grid=(N_TILES,),
        in_specs=[pl.BlockSpec((TT, H_AUG), lambda i, *_: (i, 0)),
                  pl.BlockSpec((N_EXPERTS, TT), lambda i, *_: (0, i)),
                  pl.BlockSpec((None, 2, N_EXPERTS), lambda i, *_: (i, 0, 0)),
                  pl.BlockSpec((None, N_EXPERTS, 2), lambda i, *_: (i, 0, 0))],
        out_specs=pl.BlockSpec(memory_space=pl.ANY),
        scratch_shapes=[pltpu.VMEM((2, MAX_GROUPS, SEG_ALIGN, H_AUG), BF16),
                        pltpu.SemaphoreType.DMA((2,))],
    )
    return pl.pallas_call(
        _dispatch_kernel,
        out_shape=jax.ShapeDtypeStruct(((SORTED_ROWS + TRASH_ROWS) // SEG_ALIGN, SEG_ALIGN, H_AUG), BF16),
        grid_spec=grid_spec,
        compiler_params=_cparams(("arbitrary",)),
        name="moe_dispatch",
    )(plan["move_out"], plan["group_dst"], h2aug, rkt, plan["seg_row"], plan["seg_col"])


def _ffn_kernel(nu_ref, te_ref, tv_ref, tr_ref, re_ref, nr_ref,
                xs_ref, wg_hbm, wu_hbm, wd_hbm, y_hbm,
                xbuf_ref, ybuf_ref, wg_raw, wu_raw, wd_raw, wgu_ref, wd_ref, xsem, ysem, wsem,
                *, layer):
    n_used = nu_ref[0]
    n_runs = nr_ref[0]

    tile_groups = TM_FFN // SEG_ALIGN

    def groups_of(t):
        return pl.ds(pl.multiple_of(t * tile_groups, tile_groups), tile_groups)

    def x_copy(t):
        slot = t % FFN_RING
        return pltpu.make_async_copy(xs_ref.at[groups_of(t)], xbuf_ref.at[slot], xsem.at[slot])

    def y_copy(t):
        slot = t % 2
        return pltpu.make_async_copy(ybuf_ref.at[slot], y_hbm.at[groups_of(t)], ysem.at[slot])

    def w_copies(r):
        slot = r % 2
        e = re_ref[r]
        return (pltpu.make_async_copy(wg_hbm.at[layer, e], wg_raw.at[slot], wsem.at[slot]),
                pltpu.make_async_copy(wu_hbm.at[layer, e], wu_raw.at[slot], wsem.at[slot]),
                pltpu.make_async_copy(wd_hbm.at[layer, e], wd_raw.at[slot], wsem.at[slot]))

    x_copy(0).start()

    @pl.when(n_used > 1)
    def _():
        x_copy(1).start()

    for cp in w_copies(0):
        cp.start()

    def tile(j, carry):
        e = te_ref[j]
        r = tr_ref[j]

        @pl.when(j + 2 < n_used)
        def _():
            x_copy(j + 2).start()

        @pl.when((j == 0) | (r != tr_ref[jnp.maximum(j - 1, 0)]))
        def _():
            for cp in w_copies(r):
                cp.wait()
            slot = r % 2
            wgu_ref[:, :EXPERT_FF] = wg_raw[slot].astype(BF16)
            wgu_ref[:, EXPERT_FF:] = wu_raw[slot].astype(BF16)
            wd_ref[...] = wd_raw[slot].astype(BF16)

            @pl.when(r + 1 < n_runs)
            def _():
                for cp in w_copies(r + 1):
                    cp.start()

        x_copy(j).wait()

        @pl.when(j >= 2)
        def _():
            y_copy(j - 2).wait()

        xt = xbuf_ref[j % FFN_RING].reshape(TM_FFN, H_AUG)
        rows = lax.broadcasted_iota(jnp.int32, (TM_FFN, 1), 0)
        valid = rows < tv_ref[j]
        x = jnp.where(valid, xt[:, :D_MODEL], jnp.zeros((), BF16))
        lane = lax.broadcasted_iota(jnp.int32, (TM_FFN, LANES), 1)
        mine = ((lane == e) | (lane == e + N_EXPERTS)) & valid
        w = jnp.sum(jnp.where(mine, xt[:, D_MODEL:].astype(F32), 0.0), axis=-1, keepdims=True)
        hgu = jnp.dot(x, wgu_ref[...], preferred_element_type=F32)
        g = hgu[:, :EXPERT_FF]
        a = (g * jax.nn.sigmoid(g)) * hgu[:, EXPERT_FF:] * w
        ybuf_ref[j % 2] = jnp.dot(a.astype(BF16), wd_ref[...], preferred_element_type=F32).astype(
            BF16).reshape(tile_groups, SEG_ALIGN, D_MODEL)
        y_copy(j).start()
        return carry

    lax.fori_loop(0, n_used, tile, 0)

    @pl.when(n_used >= 2)
    def _():
        y_copy(n_used - 2).wait()

    y_copy(n_used - 1).wait()


def _expert_ffn(plan, xs, p):
    any_spec = pl.BlockSpec(memory_space=pl.ANY)
    grid_spec = pltpu.PrefetchScalarGridSpec(
        num_scalar_prefetch=6,
        grid=(1,),
        in_specs=[any_spec, any_spec, any_spec, any_spec],
        out_specs=any_spec,
        scratch_shapes=[pltpu.VMEM((FFN_RING, TM_FFN // SEG_ALIGN, SEG_ALIGN, H_AUG), BF16),
                        pltpu.VMEM((2, TM_FFN // SEG_ALIGN, SEG_ALIGN, D_MODEL), BF16),
                        pltpu.VMEM((2, D_MODEL, EXPERT_FF), F32),
                        pltpu.VMEM((2, D_MODEL, EXPERT_FF), F32),
                        pltpu.VMEM((2, EXPERT_FF, D_MODEL), F32),
                        pltpu.VMEM((D_MODEL, 2 * EXPERT_FF), BF16),
                        pltpu.VMEM((EXPERT_FF, D_MODEL), BF16),
                        pltpu.SemaphoreType.DMA((FFN_RING,)),
                        pltpu.SemaphoreType.DMA((2,)),
                        pltpu.SemaphoreType.DMA((2,))],
    )
    return pl.pallas_call(
        functools.partial(_ffn_kernel, layer=p["layer"]),
        out_shape=jax.ShapeDtypeStruct((SORTED_ROWS // SEG_ALIGN, SEG_ALIGN, D_MODEL), BF16),
        grid_spec=grid_spec,
        compiler_params=_cparams(("arbitrary",)),
        name="moe_expert_ffn",
    )(plan["n_used"], plan["t_exp"], plan["t_valid"], plan["t_run"], plan["run_exp"], plan["n_runs"],
      xs, p["w_exp_gate"], p["w_exp_up"], p["w_exp_down"])


def _combine_kernel(nc_ref, gs_ref, y_ref, rktok_ref, srow_ref, scol_ref, h_ref, x1_ref, mod_ref,
                    sgu_ref, sd_ref, o_ref, yl_ref, acc_ref, sem):
    i = pl.program_id(0)
    slot = i % 2
    pslot = 1 - slot

    def gather_chunk(tile, slot_, c):
        for k in range(GROUPS_PER_CHUNK):
            g = c * GROUPS_PER_CHUNK + k
            pltpu.make_async_copy(y_ref.at[gs_ref[tile * MAX_GROUPS + g]], yl_ref.at[slot_, g],
                                  sem.at[slot_]).start()

    def wait_chunks(n, slot_):
        def body(c, carry):
            pltpu.make_async_copy(y_ref.at[pl.ds(0, GROUPS_PER_CHUNK)],
                                  yl_ref.at[slot_, pl.ds(0, GROUPS_PER_CHUNK)], sem.at[slot_]).wait()
            return carry
        lax.fori_loop(0, n, body, 0)

    @pl.when(i == 0)
    def _():
        yl_ref[...] = jnp.zeros(yl_ref.shape, BF16)

        def first(c, carry):
            gather_chunk(0, 0, c)
            return carry

        lax.fori_loop(0, nc_ref[0], first, 0)

    h = h_ref[...]
    hs = jnp.dot(h, sgu_ref[...], preferred_element_type=F32)
    gs = hs[:, :EXPERT_FF]
    a = (gs * jax.nn.sigmoid(gs)) * hs[:, EXPERT_FF:]
    shared = jnp.dot(a.astype(BF16), sd_ref[...], preferred_element_type=F32)

    n_c = nc_ref[i]
    ls_col = scol_ref[:, 0:1]
    le_col = scol_ref[:, 1:2]
    ls_row8 = jnp.broadcast_to(srow_ref[0:1, :], (8, N_EXPERTS)).astype(BF16)
    rk_ls = jnp.concatenate([rktok_ref[...], ls_row8], axis=0)
    rio_e0 = lax.broadcasted_iota(jnp.int32, (N_EXPERTS, CHUNK_ROWS), 1).astype(F32)
    rio_r0 = lax.broadcasted_iota(jnp.int32, (1, CHUNK_ROWS), 1).astype(F32)

    def select_t(r0f):
        rio_e = rio_e0 + r0f
        gt = jnp.where((rio_e >= ls_col) & (rio_e < le_col), 1.0, 0.0).astype(BF16)
        both = jnp.dot(rk_ls, gt, preferred_element_type=F32)
        want = (rio_r0 + r0f) - both[TT:TT + 1, :]
        return jnp.where(both[:TT, :] == want, 1.0, 0.0).astype(BF16)

    static_rows = STATIC_CHUNKS * CHUNK_ROWS
    pts = []
    for c in range(STATIC_CHUNKS):
        pts.append(select_t(float(c * CHUNK_ROWS)))
        gather_chunk(i + 1, pslot, c)
    pt_all = jnp.concatenate(pts, axis=1)

    def gather_more(c, carry):
        gather_chunk(i + 1, pslot, c)
        return carry

    lax.fori_loop(STATIC_CHUNKS, nc_ref[i + 1], gather_more, 0)
    wait_chunks(n_c, slot)
    y_static = yl_ref[slot, pl.ds(0, STATIC_CHUNKS * GROUPS_PER_CHUNK)].reshape(static_rows, D_MODEL)
    acc_ref[...] = shared + jnp.dot(pt_all, y_static, preferred_element_type=F32)

    def chunk(c, carry):
        g0 = pl.multiple_of(c * GROUPS_PER_CHUNK, GROUPS_PER_CHUNK)
        y_c = yl_ref[slot, pl.ds(g0, GROUPS_PER_CHUNK)].reshape(CHUNK_ROWS, D_MODEL)
        acc_ref[...] += jnp.dot(select_t((c * CHUNK_ROWS).astype(F32)), y_c, preferred_element_type=F32)
        return carry

    lax.fori_loop(STATIC_CHUNKS, n_c, chunk, 0)
    o_ref[...] = x1_ref[...] + mod_ref[5:6, :] * acc_ref[...]

    @pl.when(i == N_TILES - 1)
    def _():
        wait_chunks(nc_ref[i + 1], pslot)


def _combine(plan, y, rktok, h2aug, x1, mod_l, p):
    tiles_per_seq = SEQ // TT
    grid_spec = pltpu.PrefetchScalarGridSpec(
        num_scalar_prefetch=2,
        grid=(N_TILES,),
        in_specs=[pl.BlockSpec(memory_space=pl.ANY),
                  pl.BlockSpec((TT, N_EXPERTS), lambda i, *_: (i, 0)),
                  pl.BlockSpec((None, 2, N_EXPERTS), lambda i, *_: (i, 0, 0)),
                  pl.BlockSpec((None, N_EXPERTS, 2), lambda i, *_: (i, 0, 0)),
                  pl.BlockSpec((TT, D_MODEL), lambda i, *_: (i, 0)),
                  pl.BlockSpec((TT, D_MODEL), lambda i, *_: (i, 0)),
                  pl.BlockSpec((None, 6, D_MODEL), lambda i, *_: (i // tiles_per_seq, 0, 0)),
                  pl.BlockSpec((D_MODEL, 2 * EXPERT_FF), lambda i, *_: (0, 0)),
                  pl.BlockSpec((EXPERT_FF, D_MODEL), lambda i, *_: (0, 0))],
        out_specs=pl.BlockSpec((TT, D_MODEL), lambda i, *_: (i, 0)),
        scratch_shapes=[pltpu.VMEM((2, MAX_GROUPS, SEG_ALIGN, D_MODEL), BF16),
                        pltpu.VMEM((TT, D_MODEL), F32),
                        pltpu.SemaphoreType.DMA((2,))],
    )
    return pl.pallas_call(
        _combine_kernel,
        out_shape=jax.ShapeDtypeStruct((N_TOK, D_MODEL), F32),
        grid_spec=grid_spec,
        compiler_params=_cparams(("arbitrary",)),
        name="moe_combine",
    )(plan["move_in"], plan["group_src"], y, rktok, plan["seg_row"], plan["seg_col"], h2aug, x1,
      mod_l, p["w_sh_gu"], p["w_sh_down"])


def _moe(h2aug, rkt, rktok, cnt, x1, mod_l, p):
    plan = _moe_plan(cnt[:, :, 0])
    xs = _dispatch(plan, h2aug, rkt)
    y = _expert_ffn(plan, xs, p)
    return _combine(plan, y, rktok, h2aug, x1, mod_l, p)


def _pad_heads(w, per_head):
    k = w.shape[0]
    w = w.reshape(k, MLA_HEADS, per_head)
    w = jnp.pad(w, ((0, 0), (0, 0), (0, HEAD_PAD - per_head)))
    return w.reshape(k, MLA_HEADS * HEAD_PAD)


def _layer_params(l, a):
    w_in = a["w_in"][l]
    half = QK_ROPE // 2
    pe_cols = jnp.zeros((D_MODEL, LANES), F32).at[:, QK_NOPE:QK_DIM].set(w_in[:, O_PE:])
    pe_cols = pe_cols.at[:, QK_DIM:QK_DIM + half].set(w_in[:, O_PE:O_PE + half])
    w_in_p = jnp.concatenate([w_in[:, :O_PE], pe_cols], axis=1).astype(BF16)
    w_q = a["w_q_up"][l].reshape(Q_LORA, MLA_HEADS, QK_DIM)
    w_q = jnp.concatenate([w_q, w_q[:, :, QK_NOPE:QK_NOPE + half],
                           jnp.zeros((Q_LORA, MLA_HEADS, HEAD_PAD - QK_DIM - half), F32)], axis=-1)
    w_kv = a["w_kv_up"][l].reshape(KV_LORA, MLA_HEADS, QK_NOPE + V_DIM)
    w_k = _pad_heads(w_kv[:, :, :QK_NOPE].reshape(KV_LORA, MLA_HEADS * QK_NOPE), QK_NOPE)
    w_v = w_kv[:, :, QK_NOPE:].reshape(KV_LORA, MLA_HEADS * V_DIM)
    blocks = TT // GM_BLOCK
    eye = jnp.eye(blocks, dtype=F32)
    ws_big = jnp.einsum("ab,gts->gatbs", eye, a["gm_ws"][l]).reshape(GM_GROUPS, TT, TT)
    bs_tile = jnp.tile(jnp.repeat(a["gm_bs"][l].T, GM_CH, axis=1), (TM // GM_BLOCK, 1))
    pad96 = lambda g: jnp.concatenate(
        [g, g[QK_NOPE:QK_NOPE + half], jnp.zeros((LANES - QK_DIM - half,), F32)]).reshape(1, LANES)
    return {
        "norm_mix_g": a["norm_mix_g"][l].reshape(1, D_MODEL),
        "norm_ffn_g": a["norm_ffn_g"][l].reshape(1, D_MODEL),
        "w_in": w_in_p,
        "gm_ln_g": a["gm_ln_g"][l].reshape(1, GM_WIDTH),
        "gm_ln_b": a["gm_ln_b"][l].reshape(1, GM_WIDTH),
        "gm_ws": ws_big.astype(BF16),
        "gm_bs": bs_tile,
        "q_a_g": a["q_a_g"][l].reshape(1, Q_LORA),
        "w_q_up": w_q.reshape(Q_LORA, MLA_HEADS * HEAD_PAD).astype(BF16),
        "kv_a_g": a["kv_a_g"][l].reshape(1, KV_LORA),
        "w_k_up": w_k.astype(BF16),
        "w_v_t": jnp.pad(w_v.T.reshape(MLA_HEADS, V_DIM, KV_LORA), ((0, 0), (0, V_AUG - V_DIM), (0, 0))
                         ).reshape(MLA_HEADS * V_AUG, KV_LORA).astype(BF16),
        "v_ones": jnp.tile((jnp.arange(V_AUG) == V_DIM).astype(F32), MLA_HEADS).reshape(-1, 1),
        "q_norm_g": pad96(a["q_norm_g"][l]),
        "k_norm_g": pad96(a["k_norm_g"][l]),
        "out_norm_gm": a["out_norm_g"][l, :GM_WIDTH].reshape(1, GM_WIDTH),
        "out_norm_mla": a["out_norm_g"][l, GM_WIDTH:].reshape(1, MLA_WIDTH),
        "w_out": a["w_out"][l].astype(BF16),
        "w_router_t": a["w_router"][l].T,
        "router_bias": a["router_bias"][l].reshape(N_EXPERTS, 1),
        "layer": l,
        "w_exp_gate": a["w_exp_gate"],
        "w_exp_up": a["w_exp_up"],
        "w_exp_down": a["w_exp_down"],
        "w_sh_gu": jnp.concatenate([a["w_sh_gate"][l], a["w_sh_up"][l]], axis=-1).astype(BF16),
        "w_sh_down": a["w_sh_down"][l].astype(BF16),
    }


def kernel(x, c, positions, norm_mix_g, norm_ffn_g, w_ada, b_ada, w_in, gm_ln_g, gm_ln_b, gm_ws, gm_bs, q_a_g, w_q_up, kv_a_g, w_kv_up, q_norm_g, k_norm_g, out_norm_g, w_out, w_router, router_bias, w_exp_gate, w_exp_up, w_exp_down, w_sh_gate, w_sh_up, w_sh_down):
    a = dict(norm_mix_g=norm_mix_g, norm_ffn_g=norm_ffn_g, w_in=w_in, gm_ln_g=gm_ln_g,
             gm_ln_b=gm_ln_b, gm_ws=gm_ws, gm_bs=gm_bs, q_a_g=q_a_g, w_q_up=w_q_up,
             kv_a_g=kv_a_g, w_kv_up=w_kv_up, q_norm_g=q_norm_g, k_norm_g=k_norm_g,
             out_norm_g=out_norm_g, w_out=w_out, w_router=w_router, router_bias=router_bias,
             w_exp_gate=w_exp_gate, w_exp_up=w_exp_up, w_exp_down=w_exp_down,
             w_sh_gate=w_sh_gate, w_sh_up=w_sh_up, w_sh_down=w_sh_down)
    cos_t, sin_t = _rope_tables(positions)
    mod = _ada_mod(c, w_ada, b_ada).reshape(DEPTH, BATCH, 6, D_MODEL)
    xt = x.reshape(N_TOK, D_MODEL)
    for l in range(DEPTH):
        p = _layer_params(l, a)
        gm, q, k, v = _mix_in(xt, mod[l], p, cos_t, sin_t)
        mla = _attention(q, k, v).reshape(N_TOK, MLA_WIDTH)
        x1, h2aug, rkt, rktok, cnt = _mix_out(xt, gm, mla, mod[l], p)
        xt = _moe(h2aug, rkt, rktok, cnt, x1, mod[l], p)
    return xt.reshape(BATCH, SEQ, D_MODEL)
```

```python
import functools

import jax
import jax.numpy as jnp
from jax import lax
from jax.experimental import pallas as pl
from jax.experimental.pallas import tpu as pltpu

D_MODEL = 1024
BATCH = 8
SEQ = 4096
DEPTH = 2
N_TOK = BATCH * SEQ
CHUNK = 64
GM_WIDTH = 512
GM_BLOCK = 128
GM_GROUPS = 4
GM_CH = 128
MLA_HEADS = 8
QK_NOPE = 64
QK_ROPE = 32
QK_DIM = 96
V_DIM = 64
MLA_WIDTH = 512
Q_LORA = 384
KV_LORA = 256
ROPE_THETA = 10000.0
N_EXPERTS = 64
TOP_K = 8
N_GROUPS = 8
GROUP_SIZE = N_EXPERTS // N_GROUPS
TOPK_GROUPS = 4
EXPERT_FF = 256
ROUTED_SCALE = 2.5
EPS = 1e-6

LANES = 128
HEAD_PAD = LANES
IN_COLS_PAD = 2 * GM_WIDTH + Q_LORA + KV_LORA + LANES
O_Q = 2 * GM_WIDTH
O_KV = O_Q + Q_LORA
O_PE = O_KV + KV_LORA

TT = 256
TM = 512
TQ = 512
TK = 256
ATT_HEADS = 8
V_AUG = 80
N_TILES = N_TOK // TT
H_AUG = D_MODEL + LANES
SEG_ALIGN = 16
CHUNK_ROWS = 256
LOCAL_ROWS = TT * TOP_K + N_EXPERTS * SEG_ALIGN
MAX_GROUPS = LOCAL_ROWS // SEG_ALIGN
TM_FFN = 512
FFN_RING = 3
SORTED_ROWS = -(-(N_TOK * TOP_K + N_TILES * N_EXPERTS * (SEG_ALIGN - 1)
                  + N_EXPERTS * (TM_FFN - 1)) // TM_FFN) * TM_FFN
FFN_TILES = SORTED_ROWS // TM_FFN
GROUPS_PER_CHUNK = CHUNK_ROWS // SEG_ALIGN
TRASH_ROWS = 2 * LOCAL_ROWS
STATIC_CHUNKS = 10
VMEM_LIMIT = 56 * 1024 * 1024

F32 = jnp.float32
BF16 = jnp.bfloat16


def _cparams(sem, flags=None):
    return pltpu.CompilerParams(dimension_semantics=sem, vmem_limit_bytes=VMEM_LIMIT, flags=flags)


def _rms_scale(v, width):
    return lax.rsqrt(jnp.sum(v * v, axis=-1, keepdims=True) * (1.0 / width) + EPS)


def _rope_kernel(pos_ref, inv_ref, cos_ref, sin_ref):
    ang = pos_ref[...].astype(F32) * inv_ref[...]
    lane = lax.broadcasted_iota(jnp.int32, ang.shape, 1)
    c = jnp.cos(ang)
    s = jnp.sin(ang)
    cos_ref[...] = jnp.where(lane < QK_NOPE, 1.0, jnp.where(lane < QK_DIM, c, 0.0))
    sin_ref[...] = jnp.where((lane >= QK_NOPE) & (lane < QK_NOPE + 16), -s,
                             jnp.where((lane >= QK_NOPE + 16) & (lane < QK_DIM), s, 0.0))


def _rope_tables(positions):
    inv_freq = 1.0 / (ROPE_THETA ** (jnp.arange(0, QK_ROPE, 2, dtype=F32) / QK_ROPE))
    inv = jnp.zeros((1, LANES), F32)
    inv = inv.at[0, QK_NOPE:QK_NOPE + 16].set(inv_freq)
    inv = inv.at[0, QK_NOPE + 16:QK_DIM].set(inv_freq)
    pos = positions.reshape(N_TOK, 1)
    tile = 2048
    return pl.pallas_call(
        _rope_kernel,
        out_shape=(jax.ShapeDtypeStruct((N_TOK, LANES), F32),
                   jax.ShapeDtypeStruct((N_TOK, LANES), F32)),
        grid=(N_TOK // tile,),
        in_specs=[pl.BlockSpec((tile, 1), lambda i: (i, 0)),
                  pl.BlockSpec((1, LANES), lambda i: (0, 0))],
        out_specs=(pl.BlockSpec((tile, LANES), lambda i: (i, 0)),
                   pl.BlockSpec((tile, LANES), lambda i: (i, 0))),
        compiler_params=_cparams(("arbitrary",)),
        name="rope_tables",
    )(pos, inv)


def _ada_kernel(c_ref, w_ref, b_ref, o_ref):
    c = c_ref[...]
    ca = c * jax.nn.sigmoid(c)
    o_ref[...] = jnp.dot(ca, w_ref[...], precision=lax.Precision.HIGHEST,
                         preferred_element_type=F32) + b_ref[...]


def _ada_mod(c, w_ada, b_ada):
    return pl.pallas_call(
        _ada_kernel,
        out_shape=jax.ShapeDtypeStruct((DEPTH, BATCH, 6 * D_MODEL), F32),
        grid=(DEPTH, 6),
        in_specs=[pl.BlockSpec((BATCH, D_MODEL), lambda l, j: (0, 0)),
                  pl.BlockSpec((None, D_MODEL, D_MODEL), lambda l, j: (l, 0, j)),
                  pl.BlockSpec((None, 1, D_MODEL), lambda l, j: (l, 0, j))],
        out_specs=pl.BlockSpec((None, BATCH, D_MODEL), lambda l, j: (l, 0, j)),
        compiler_params=_cparams(("arbitrary", "arbitrary")),
        name="ada_mod",
    )(c, w_ada, b_ada.reshape(DEPTH, 1, 6 * D_MODEL))


def _rope_rot(t, cos_t, sin_t):
    return t * cos_t + pltpu.roll(t, LANES - 16, 1) * sin_t


def _mix_in_kernel(x_ref, mod_ref, ng_ref, win_ref, lng_ref, lnb_ref, ws_ref, bs_ref,
                   qag_ref, wq_ref, kvg_ref, wkv_ref, wvt_ref, vone_ref, qng_ref, kng_ref, og_ref,
                   cos_ref, sin_ref,
                   gm_ref, q_ref, k_ref, vt_ref):
    x = x_ref[...]
    sh1 = mod_ref[0:1, :]
    sc1 = mod_ref[1:2, :]
    h = x * _rms_scale(x, D_MODEL) * ng_ref[...]
    h = h * (1.0 + sc1) + sh1
    proj = jnp.dot(h.astype(BF16), win_ref[...], preferred_element_type=F32)

    uv_pre = proj[:, :2 * GM_WIDTH]
    uv = 0.5 * uv_pre * (1.0 + lax.erf(uv_pre * (2.0 ** -0.5)))
    row = lax.broadcasted_iota(jnp.int32, (TT, TT), 0)
    col = lax.broadcasted_iota(jnp.int32, (TT, TT), 1)
    causal = (col <= row) & ((col // GM_BLOCK) == (row // GM_BLOCK))
    groups = range(GM_GROUPS)
    gsl = [slice(g * GM_CH, (g + 1) * GM_CH) for g in groups]
    vg = [uv[:, GM_WIDTH + g * GM_CH:GM_WIDTH + (g + 1) * GM_CH] for g in groups]
    mu = [jnp.mean(vg[g], axis=-1, keepdims=True) for g in groups]
    xc = [vg[g] - mu[g] for g in groups]
    var = [jnp.mean(xc[g] * xc[g], axis=-1, keepdims=True) for g in groups]
    vln = [(xc[g] * lax.rsqrt(var[g] + EPS) * lng_ref[:, gsl[g]] + lnb_ref[:, gsl[g]]).astype(BF16)
           for g in groups]
    ws = [jnp.where(causal, ws_ref[g], jnp.zeros((), BF16)) for g in groups]
    gate = [jnp.concatenate([jnp.dot(ws[g], vln[g][s * TT:(s + 1) * TT], preferred_element_type=F32)
                             for s in range(TM // TT)], axis=0) for g in groups]
    outs = [uv[:, gsl[g]] * (gate[g] + bs_ref[:, gsl[g]]) for g in groups]
    sq = [jnp.sum(outs[g] * outs[g], axis=-1, keepdims=True) for g in groups]
    rs = lax.rsqrt((sq[0] + sq[1] + sq[2] + sq[3]) * (1.0 / GM_WIDTH) + EPS)
    for g in groups:
        gm_ref[:, gsl[g]] = (outs[g] * rs * og_ref[:, gsl[g]]).astype(BF16)

    cos_t = cos_ref[...]
    sin_t = sin_ref[...]
    lane = lax.broadcasted_iota(jnp.int32, (TM, LANES), 1)
    ql = proj[:, O_Q:O_KV]
    qn = ql * _rms_scale(ql, Q_LORA) * qag_ref[...]
    qa = jnp.dot(qn.astype(BF16), wq_ref[...], preferred_element_type=F32)
    kl = proj[:, O_KV:O_PE]
    kn = kl * _rms_scale(kl, KV_LORA) * kvg_ref[...]
    kva = jnp.dot(kn.astype(BF16), wkv_ref[...], preferred_element_type=F32)
    pe = proj[:, O_PE:O_PE + LANES]
    real = lane < QK_DIM
    pe_ssq = jnp.sum(jnp.where(real, pe * pe, 0.0), axis=-1, keepdims=True)
    pe_rot = _rope_rot(pe * kng_ref[...], cos_t, sin_t)
    qscale = (QK_DIM ** -0.5) * 1.4426950408889634
    v_t = lax.dot_general(wvt_ref[...], kn.astype(BF16), (((1,), (1,)), ((), ())),
                          preferred_element_type=F32) + vone_ref[...]
    vt_ref[...] = v_t.reshape(MLA_HEADS, V_AUG, TM).astype(BF16)
    heads = range(MLA_HEADS)
    qh = [qa[:, hd * HEAD_PAD:(hd + 1) * HEAD_PAD] for hd in heads]
    kh = [kva[:, hd * HEAD_PAD:(hd + 1) * HEAD_PAD] for hd in heads]
    q_ssq = [jnp.sum(jnp.where(real, qh[hd] * qh[hd], 0.0), axis=-1, keepdims=True) for hd in heads]
    k_ssq = [jnp.sum(kh[hd] * kh[hd], axis=-1, keepdims=True) for hd in heads]
    qg = [qh[hd] * lax.rsqrt(q_ssq[hd] * (1.0 / QK_DIM) + EPS) * qng_ref[...] for hd in heads]
    q_sw = [pltpu.roll(qg[hd], LANES - 16, 1) for hd in heads]
    for hd in heads:
        q_ref[hd] = ((qg[hd] * cos_t + q_sw[hd] * sin_t) * qscale).astype(BF16)
    for hd in heads:
        krs = lax.rsqrt((k_ssq[hd] + pe_ssq) * (1.0 / QK_DIM) + EPS)
        k_ref[hd] = ((kh[hd] * kng_ref[...] + pe_rot) * krs).astype(BF16)


def _mix_in(x, mod_l, p, cos_t, sin_t):
    tiles_per_seq = SEQ // TM
    full = lambda shape: pl.BlockSpec(shape, lambda i: (0,) * len(shape))
    head_spec = pl.BlockSpec((None, MLA_HEADS, TM, HEAD_PAD),
                             lambda i: (i // tiles_per_seq, 0, i % tiles_per_seq, 0))
    head_shape = jax.ShapeDtypeStruct((BATCH, MLA_HEADS, SEQ, HEAD_PAD), BF16)
    return pl.pallas_call(
        _mix_in_kernel,
        out_shape=(jax.ShapeDtypeStruct((N_TOK, GM_WIDTH), BF16), head_shape, head_shape,
                   jax.ShapeDtypeStruct((BATCH, MLA_HEADS, V_AUG, SEQ), BF16)),
        grid=(N_TOK // TM,),
        in_specs=[pl.BlockSpec((TM, D_MODEL), lambda i: (i, 0)),
                  pl.BlockSpec((None, 6, D_MODEL), lambda i: (i // tiles_per_seq, 0, 0)),
                  full((1, D_MODEL)),
                  full((D_MODEL, IN_COLS_PAD)),
                  full((1, GM_WIDTH)), full((1, GM_WIDTH)),
                  full((GM_GROUPS, TT, TT)), full((TM, GM_WIDTH)),
                  full((1, Q_LORA)), full((Q_LORA, MLA_HEADS * HEAD_PAD)),
                  full((1, KV_LORA)), full((KV_LORA, MLA_HEADS * HEAD_PAD)),
                  full((MLA_HEADS * V_AUG, KV_LORA)), full((MLA_HEADS * V_AUG, 1)),
                  full((1, LANES)), full((1, LANES)), full((1, GM_WIDTH)),
                  pl.BlockSpec((TM, LANES), lambda i: (i, 0)),
                  pl.BlockSpec((TM, LANES), lambda i: (i, 0))],
        out_specs=(pl.BlockSpec((TM, GM_WIDTH), lambda i: (i, 0)), head_spec, head_spec,
                   pl.BlockSpec((None, MLA_HEADS, V_AUG, TM),
                                lambda i: (i // tiles_per_seq, 0, 0, i % tiles_per_seq))),
        compiler_params=_cparams(("arbitrary",)),
        name="mix_in",
    )(x, mod_l, p["norm_mix_g"], p["w_in"], p["gm_ln_g"], p["gm_ln_b"], p["gm_ws"], p["gm_bs"],
      p["q_a_g"], p["w_q_up"], p["kv_a_g"], p["w_k_up"], p["w_v_t"], p["v_ones"], p["q_norm_g"],
      p["k_norm_g"],
      p["out_norm_gm"], cos_t, sin_t)


def _attn_kernel(q_ref, k_ref, vt_ref, o_ref):
    qi = pl.program_id(2)
    krow = lax.broadcasted_iota(jnp.int32, (TK, TQ), 0)
    qcol = lax.broadcasted_iota(jnp.int32, (TK, TQ), 1)
    nt = (((1,), (1,)), ((), ()))
    per_q = TQ // TK

    def tile(j, carry, diag):
        start = pl.multiple_of(j * TK, TK)
        heads = range(ATT_HEADS)
        s = [lax.dot_general(k_ref[hh, pl.ds(start, TK), :], q_ref[hh], nt,
                             preferred_element_type=F32) for hh in heads]
        if diag is not None:
            allowed = ((krow + diag * TK) // CHUNK) <= (qcol // CHUNK)
            s = [jnp.where(allowed, sh, -jnp.inf) for sh in s]
        m_new = [jnp.maximum(carry[hh][0], jnp.max(s[hh], axis=0, keepdims=True)) for hh in heads]
        p = [jnp.exp2(s[hh] - m_new[hh]) for hh in heads]
        alpha = [jnp.exp2(carry[hh][0] - m_new[hh]) for hh in heads]
        pv = [jnp.dot(vt_ref[hh, :, pl.ds(start, TK)], p[hh].astype(BF16),
                      preferred_element_type=F32) for hh in heads]
        return tuple((m_new[hh], alpha[hh] * carry[hh][1] + pv[hh]) for hh in heads)

    init = tuple((jnp.full((1, TQ), -jnp.inf, F32), jnp.zeros((V_AUG, TQ), F32))
                 for _ in range(ATT_HEADS))
    carry = lax.fori_loop(0, qi * per_q, functools.partial(tile, diag=None), init)
    for t in range(per_q):
        carry = tile(qi * per_q + t, carry, t)
    out_t = jnp.concatenate([acc[:V_DIM] / acc[V_DIM:V_DIM + 1] for (_, acc) in carry], axis=0)
    o_ref[...] = out_t.T


def _attention(q, k, vt):
    groups = MLA_HEADS // ATT_HEADS
    qspec = pl.BlockSpec((None, ATT_HEADS, TQ, HEAD_PAD), lambda b, hp, i: (b, hp, i, 0))
    kspec = pl.BlockSpec((None, ATT_HEADS, SEQ, HEAD_PAD), lambda b, hp, i: (b, hp, 0, 0))
    vspec = pl.BlockSpec((None, ATT_HEADS, V_AUG, SEQ), lambda b, hp, i: (b, hp, 0, 0))
    return pl.pallas_call(
        _attn_kernel,
        out_shape=jax.ShapeDtypeStruct((BATCH, SEQ, MLA_WIDTH), F32),
        grid=(BATCH, groups, SEQ // TQ),
        in_specs=[qspec, kspec, vspec],
        out_specs=pl.BlockSpec((None, TQ, ATT_HEADS * V_DIM), lambda b, hp, i: (b, i, hp)),
        compiler_params=_cparams(("arbitrary", "arbitrary", "arbitrary")),
        name="mla_attention",
    )(q, k, vt)


def _route(logits_t, bias_col):
    t = logits_t.shape[1]
    scores = jax.nn.sigmoid(logits_t)
    biased = scores + bias_col
    neg = -jnp.inf
    b3 = biased.reshape(N_GROUPS, GROUP_SIZE, t)
    m1 = jnp.max(b3, axis=1, keepdims=True)
    n_max = jnp.sum((b3 == m1).astype(F32), axis=1, keepdims=True)
    m2 = jnp.max(jnp.where(b3 < m1, b3, neg), axis=1, keepdims=True)
    gs = (m1 + jnp.where(n_max >= 2.0, m1, m2)).reshape(N_GROUPS, t)
    gidx = lax.broadcasted_iota(jnp.int32, (N_GROUPS, t), 0)
    grank = jnp.zeros((N_GROUPS, t), F32)
    for g in range(N_GROUPS):
        other = gs[g:g + 1, :]
        ahead = (other > gs) | ((other == gs) & (gidx > g))
        grank = grank + ahead.astype(F32)
    gsel = grank < float(TOPK_GROUPS)
    emask = jnp.broadcast_to(gsel.reshape(N_GROUPS, 1, t), (N_GROUPS, GROUP_SIZE, t)).reshape(N_EXPERTS, t)
    ms = jnp.where(emask, biased, neg)
    eidx = lax.broadcasted_iota(jnp.int32, (N_EXPERTS, t), 0).astype(F32)
    picked = jnp.zeros((N_EXPERTS, t), F32)
    for _ in range(TOP_K):
        top = jnp.max(ms, axis=0, keepdims=True)
        first = jnp.min(jnp.where(ms == top, eidx, float(N_EXPERTS)), axis=0, keepdims=True)
        hit = eidx == first
        picked = jnp.where(hit, 1.0, picked)
        ms = jnp.where(hit, neg, ms)
    sel = picked > 0.5
    w = jnp.where(sel, scores, 0.0)
    denom = jnp.sum(w, axis=0, keepdims=True)
    return (w / denom) * ROUTED_SCALE, sel


def _mix_out_kernel(x_ref, gm_ref, mla_ref, mod_ref, og_ref, wout_ref, fg_ref, wr_ref, rb_ref,
                    x1_ref, h2_ref, rkt_ref, rktok_ref, cnt_ref):
    mla = mla_ref[...]
    mla_n = mla * _rms_scale(mla, MLA_WIDTH) * og_ref[...]
    mixed = jnp.concatenate([gm_ref[...], mla_n.astype(BF16)], axis=-1)
    y = jnp.dot(mixed, wout_ref[...], preferred_element_type=F32)
    g1 = mod_ref[2:3, :]
    sh2 = mod_ref[3:4, :]
    sc2 = mod_ref[4:5, :]
    x1 = x_ref[...] + g1 * y
    x1_ref[...] = x1
    h2 = x1 * _rms_scale(x1, D_MODEL) * fg_ref[...]
    h2 = h2 * (1.0 + sc2) + sh2
    nt = (((1,), (1,)), ((), ()))
    h_hi = h2.astype(BF16)
    h_lo = (h2 - h_hi.astype(F32)).astype(BF16)
    wr = wr_ref[...]
    w_hi = wr.astype(BF16)
    w_lo = (wr - w_hi.astype(F32)).astype(BF16)
    logits_t = (lax.dot_general(w_hi, h_hi, nt, preferred_element_type=F32)
                + lax.dot_general(w_hi, h_lo, nt, preferred_element_type=F32)
                + lax.dot_general(w_lo, h_hi, nt, preferred_element_type=F32))
    comb, sel = _route(logits_t, rb_ref[...])
    before = (lax.broadcasted_iota(jnp.int32, (TT, TT), 0)
              < lax.broadcasted_iota(jnp.int32, (TT, TT), 1))
    self = jnp.where(sel, 1.0, 0.0)
    before_b = jnp.where(before, 1.0, 0.0).astype(BF16)
    rank = jnp.concatenate(
        [jnp.dot(self[:, s * TT:(s + 1) * TT].astype(BF16), before_b, preferred_element_type=F32)
         for s in range(TM // TT)], axis=1)
    rk = jnp.where(sel, rank, -1.0)
    rkt_ref[...] = rk.astype(BF16)
    rktok_ref[...] = rk.T.astype(BF16)
    for s in range(TM // TT):
        cnt_ref[s] = jnp.broadcast_to(
            jnp.sum(self[:, s * TT:(s + 1) * TT], axis=1, keepdims=True), (N_EXPERTS, LANES))
    comb_tok = comb.T
    hi = comb_tok.astype(BF16)
    lo = (comb_tok - hi.astype(F32)).astype(BF16)
    h2_ref[:, :D_MODEL] = h2.astype(BF16)
    h2_ref[:, D_MODEL:] = jnp.concatenate([hi, lo], axis=-1)


def _mix_out(x, gm, mla, mod_l, p):
    tiles_per_seq = SEQ // TM
    full = lambda shape: pl.BlockSpec(shape, lambda i: (0,) * len(shape))
    tok = lambda width: pl.BlockSpec((TM, width), lambda i: (i, 0))
    return pl.pallas_call(
        _mix_out_kernel,
        out_shape=(jax.ShapeDtypeStruct((N_TOK, D_MODEL), F32),
                   jax.ShapeDtypeStruct((N_TOK, H_AUG), BF16),
                   jax.ShapeDtypeStruct((N_EXPERTS, N_TOK), BF16),
                   jax.ShapeDtypeStruct((N_TOK, N_EXPERTS), BF16),
                   jax.ShapeDtypeStruct((N_TILES, N_EXPERTS, LANES), F32)),
        grid=(N_TOK // TM,),
        in_specs=[tok(D_MODEL), tok(GM_WIDTH), tok(MLA_WIDTH),
                  pl.BlockSpec((None, 6, D_MODEL), lambda i: (i // tiles_per_seq, 0, 0)),
                  full((1, MLA_WIDTH)), full((D_MODEL, D_MODEL)), full((1, D_MODEL)),
                  full((N_EXPERTS, D_MODEL)), full((N_EXPERTS, 1))],
        out_specs=(tok(D_MODEL), tok(H_AUG), pl.BlockSpec((N_EXPERTS, TM), lambda i: (0, i)),
                   tok(N_EXPERTS), pl.BlockSpec((TM // TT, N_EXPERTS, LANES), lambda i: (i, 0, 0))),
        compiler_params=_cparams(("arbitrary",)),
        name="mix_out_route",
    )(x, gm, mla, mod_l, p["out_norm_mla"], p["w_out"], p["norm_ffn_g"], p["w_router_t"],
      p["router_bias"])


def _moe_plan(cnt):
    i32 = jnp.int32
    cnt = cnt.astype(i32)
    padlen = (cnt + SEG_ALIGN - 1) // SEG_ALIGN * SEG_ALIGN
    lend = jnp.cumsum(padlen, axis=1)
    lstart = lend - padlen
    ltot = lend[:, -1]
    region = jnp.sum(padlen, axis=0)
    region_pad = (region + TM_FFN - 1) // TM_FFN * TM_FFN
    base = jnp.cumsum(region_pad) - region_pad
    pos = base[None, :] + jnp.cumsum(padlen, axis=0) - padlen
    gidx = jnp.arange(MAX_GROUPS, dtype=i32)
    grow = gidx * SEG_ALIGN
    in_seg = (grow[None, :, None] >= lstart[:, None, :]) & (grow[None, :, None] < lend[:, None, :])
    shift = jnp.sum(jnp.where(in_seg, (pos - lstart)[:, None, :], 0), axis=-1)
    n_groups = ltot // SEG_ALIGN
    valid_g = gidx[None, :] < n_groups[:, None]
    real_dst = (shift + grow[None, :]) // SEG_ALIGN
    trash_base = SORTED_ROWS // SEG_ALIGN
    trash = trash_base + (jnp.arange(N_TILES, dtype=i32)[:, None] % 2) * MAX_GROUPS + gidx[None, :]
    n_chunks = (n_groups + GROUPS_PER_CHUNK - 1) // GROUPS_PER_CHUNK
    n_move = jnp.maximum(n_chunks, STATIC_CHUNKS)
    group_dst = jnp.where(valid_g, real_dst, trash)
    group_src = jnp.where(valid_g, real_dst, 0)
    lead = (trash_base + MAX_GROUPS + gidx)[None, :]
    static_c = jnp.full((1,), STATIC_CHUNKS, i32)
    tile_end = jnp.cumsum(region_pad) // TM_FFN
    n_used = tile_end[-1]
    tj = jnp.arange(FFN_TILES, dtype=i32)
    in_exp = (tj[:, None] * TM_FFN >= base[None, :]) & (tj[:, None] * TM_FFN < (base + region_pad)[None, :])
    t_exp = jnp.sum(jnp.where(in_exp, jnp.arange(N_EXPERTS, dtype=i32)[None, :], 0), axis=-1)
    t_valid = jnp.sum(jnp.where(in_exp, jnp.clip((base + region)[None, :] - tj[:, None] * TM_FFN, 0, TM_FFN), 0),
                      axis=-1)
    has_rows = region_pad > 0
    run_of_exp = jnp.cumsum(has_rows.astype(i32)) - 1
    eidx = jnp.arange(N_EXPERTS, dtype=i32)
    t_run = jnp.sum(jnp.where(in_exp, run_of_exp[None, :], 0), axis=-1)
    run_exp = jnp.sum(jnp.where(has_rows[None, :] & (run_of_exp[None, :] == eidx[:, None]), eidx[None, :], 0),
                      axis=-1)
    lstart_f = lstart.astype(F32)
    lend_f = lend.astype(F32)
    return {
        "t_run": t_run.astype(i32),
        "run_exp": run_exp.astype(i32),
        "n_runs": jnp.sum(has_rows.astype(i32)).reshape(1),
        "move_out": jnp.concatenate([static_c, n_move]).astype(i32),
        "move_in": jnp.concatenate([n_move, static_c]).astype(i32),
        "group_dst": jnp.concatenate([lead, group_dst], axis=0).reshape(-1).astype(i32),
        "group_src": jnp.concatenate([group_src, jnp.zeros((1, MAX_GROUPS), i32)], axis=0
                                     ).reshape(-1).astype(i32),
        "seg_row": jnp.stack([lstart_f, lend_f], axis=1),
        "seg_col": jnp.stack([lstart_f, lend_f], axis=2),
        "n_used": n_used.reshape(1).astype(i32),
        "t_exp": t_exp.astype(i32),
        "t_valid": t_valid.astype(i32),
    }


def _dispatch_kernel(nm_ref, gd_ref, h_ref, rkt_ref, srow_ref, scol_ref, xs_ref, loc_ref, sem):
    i = pl.program_id(0)
    slot = i % 2
    pslot = 1 - slot

    def wait_chunks(n, slot_):
        def body(c, carry):
            pltpu.make_async_copy(loc_ref.at[slot_, pl.ds(0, GROUPS_PER_CHUNK)],
                                  xs_ref.at[pl.ds(0, GROUPS_PER_CHUNK)], sem.at[slot_]).wait()
            return carry
        lax.fori_loop(0, n, body, 0)

    def issue_chunk(entry, slot_, c):
        for k in range(GROUPS_PER_CHUNK):
            g = c * GROUPS_PER_CHUNK + k
            pltpu.make_async_copy(loc_ref.at[slot_, g], xs_ref.at[gd_ref[entry * MAX_GROUPS + g]],
                                  sem.at[slot_]).start()

    @pl.when(i == 0)
    def _():
        loc_ref[1] = jnp.zeros(loc_ref.shape[1:], BF16)

    ls_row = srow_ref[0:1, :]
    le_row = srow_ref[1:2, :]
    ls_rep = jnp.broadcast_to(scol_ref[:, 0:1], (N_EXPERTS, LANES)).astype(BF16)
    rk_ls = jnp.concatenate([rkt_ref[...], ls_rep], axis=1)
    h = h_ref[...]
    n_c = nm_ref[i + 1]
    rio_e0 = lax.broadcasted_iota(jnp.int32, (CHUNK_ROWS, N_EXPERTS), 0).astype(F32)
    rio_t0 = lax.broadcasted_iota(jnp.int32, (CHUNK_ROWS, TT), 0).astype(F32)

    def select(r0f):
        rio_e = rio_e0 + r0f
        g = jnp.where((rio_e >= ls_row) & (rio_e < le_row), 1.0, 0.0).astype(BF16)
        both = jnp.dot(g, rk_ls, preferred_element_type=F32)
        want = (rio_t0 + r0f) - jnp.concatenate([both[:, TT:]] * (TT // LANES), axis=1)
        return jnp.where(both[:, :TT] == want, 1.0, 0.0).astype(BF16)

    def emit(g0, psel):
        loc_ref[slot, pl.ds(g0, GROUPS_PER_CHUNK)] = jnp.dot(
            psel, h, preferred_element_type=F32).astype(BF16).reshape(GROUPS_PER_CHUNK, SEG_ALIGN, H_AUG)

    psels = []
    for c in range(STATIC_CHUNKS):
        psels.append(select(float(c * CHUNK_ROWS)))
        issue_chunk(i, pslot, c)

    def issue_more(c, carry):
        issue_chunk(i, pslot, c)
        return carry

    lax.fori_loop(STATIC_CHUNKS, nm_ref[i], issue_more, 0)

    @pl.when(i >= 1)
    def _():
        wait_chunks(nm_ref[i - 1], slot)

    for c in range(STATIC_CHUNKS):
        emit(c * GROUPS_PER_CHUNK, psels[c])

    def chunk(c, carry):
        emit(pl.multiple_of(c * GROUPS_PER_CHUNK, GROUPS_PER_CHUNK), select((c * CHUNK_ROWS).astype(F32)))
        return carry

    lax.fori_loop(STATIC_CHUNKS, n_c, chunk, 0)

    @pl.when(i == N_TILES - 1)
    def _():
        def issue_last(c, carry):
            issue_chunk(i + 1, slot, c)
            return carry

        lax.fori_loop(0, n_c, issue_last, 0)
        wait_chunks(nm_ref[i], pslot)
        wait_chunks(n_c, slot)


def _dispatch(plan, h2aug, rkt):
    grid_spec = pltpu.PrefetchScalarGridSpec(
        num_scalar_prefetch=2,
        grid=(N_TILES,),
        in_specs=[pl.BlockSpec((TT, H_AUG), lambda i, *_: (i, 0)),
                  pl.BlockSpec((N_EXPERTS, TT), lambda i, *_: (0, i)),
                  pl.BlockSpec((None, 2, N_EXPERTS), lambda i, *_: (i, 0, 0)),
                  pl.BlockSpec((None, N_EXPERTS, 2), lambda i, *_: (i, 0, 0))],
        out_specs=pl.BlockSpec(memory_space=pl.ANY),
        scratch_shapes=[pltpu.VMEM((2, MAX_GROUPS, SEG_ALIGN, H_AUG), BF16),
                        pltpu.SemaphoreType.DMA((2,))],
    )
    return pl.pallas_call(
        _dispatch_kernel,
        out_shape=jax.ShapeDtypeStruct(((SORTED_ROWS + TRASH_ROWS) // SEG_ALIGN, SEG_ALIGN, H_AUG), BF16),
        grid_spec=grid_spec,
        compiler_params=_cparams(("arbitrary",)),
        name="moe_dispatch",
    )(plan["move_out"], plan["group_dst"], h2aug, rkt, plan["seg_row"], plan["seg_col"])


def _ffn_kernel(nu_ref, te_ref, tv_ref, tr_ref, re_ref, nr_ref,
                xs_ref, wg_hbm, wu_hbm, wd_hbm, y_hbm,
                xbuf_ref, ybuf_ref, wg_raw, wu_raw, wd_raw, wgu_ref, wd_ref, xsem, ysem, wsem,
                *, layer):
    n_used = nu_ref[0]
    n_runs = nr_ref[0]

    tile_groups = TM_FFN // SEG_ALIGN

    def groups_of(t):
        return pl.ds(pl.multiple_of(t * tile_groups, tile_groups), tile_groups)

    def x_copy(t):
        slot = t % FFN_RING
        return pltpu.make_async_copy(xs_ref.at[groups_of(t)], xbuf_ref.at[slot], xsem.at[slot])

    def y_copy(t):
        slot = t % 2
        return pltpu.make_async_copy(ybuf_ref.at[slot], y_hbm.at[groups_of(t)], ysem.at[slot])

    def w_copies(r):
        slot = r % 2
        e = re_ref[r]
        return (pltpu.make_async_copy(wg_hbm.at[layer, e], wg_raw.at[slot], wsem.at[slot]),
                pltpu.make_async_copy(wu_hbm.at[layer, e], wu_raw.at[slot], wsem.at[slot]),
                pltpu.make_async_copy(wd_hbm.at[layer, e], wd_raw.at[slot], wsem.at[slot]))

    x_copy(0).start()

    @pl.when(n_used > 1)
    def _():
        x_copy(1).start()

    for cp in w_copies(0):
        cp.start()

    def tile(j, carry):
        e = te_ref[j]
        r = tr_ref[j]

        @pl.when(j + 2 < n_used)
        def _():
            x_copy(j + 2).start()

        @pl.when((j == 0) | (r != tr_ref[jnp.maximum(j - 1, 0)]))
        def _():
            for cp in w_copies(r):
                cp.wait()
            slot = r % 2
            wgu_ref[:, :EXPERT_FF] = wg_raw[slot].astype(BF16)
            wgu_ref[:, EXPERT_FF:] = wu_raw[slot].astype(BF16)
            wd_ref[...] = wd_raw[slot].astype(BF16)

            @pl.when(r + 1 < n_runs)
            def _():
                for cp in w_copies(r + 1):
                    cp.start()

        x_copy(j).wait()

        @pl.when(j >= 2)
        def _():
            y_copy(j - 2).wait()

        xt = xbuf_ref[j % FFN_RING].reshape(TM_FFN, H_AUG)
        rows = lax.broadcasted_iota(jnp.int32, (TM_FFN, 1), 0)
        valid = rows < tv_ref[j]
        x = jnp.where(valid, xt[:, :D_MODEL], jnp.zeros((), BF16))
        lane = lax.broadcasted_iota(jnp.int32, (TM_FFN, LANES), 1)
        mine = ((lane == e) | (lane == e + N_EXPERTS)) & valid
        w = jnp.sum(jnp.where(mine, xt[:, D_MODEL:].astype(F32), 0.0), axis=-1, keepdims=True)
        hgu = jnp.dot(x, wgu_ref[...], preferred_element_type=F32)
        g = hgu[:, :EXPERT_FF]
        a = (g * jax.nn.sigmoid(g)) * hgu[:, EXPERT_FF:] * w
        ybuf_ref[j % 2] = jnp.dot(a.astype(BF16), wd_ref[...], preferred_element_type=F32).astype(
            BF16).reshape(tile_groups, SEG_ALIGN, D_MODEL)
        y_copy(j).start()
        return carry

    lax.fori_loop(0, n_used, tile, 0)

    @pl.when(n_used >= 2)
    def _():
        y_copy(n_used - 2).wait()

    y_copy(n_used - 1).wait()


def _expert_ffn(plan, xs, p):
    any_spec = pl.BlockSpec(memory_space=pl.ANY)
    grid_spec = pltpu.PrefetchScalarGridSpec(
        num_scalar_prefetch=6,
        grid=(1,),
        in_specs=[any_spec, any_spec, any_spec, any_spec],
        out_specs=any_spec,
        scratch_shapes=[pltpu.VMEM((FFN_RING, TM_FFN // SEG_ALIGN, SEG_ALIGN, H_AUG), BF16),
                        pltpu.VMEM((2, TM_FFN // SEG_ALIGN, SEG_ALIGN, D_MODEL), BF16),
                        pltpu.VMEM((2, D_MODEL, EXPERT_FF), F32),
                        pltpu.VMEM((2, D_MODEL, EXPERT_FF), F32),
                        pltpu.VMEM((2, EXPERT_FF, D_MODEL), F32),
                        pltpu.VMEM((D_MODEL, 2 * EXPERT_FF), BF16),
                        pltpu.VMEM((EXPERT_FF, D_MODEL), BF16),
                        pltpu.SemaphoreType.DMA((FFN_RING,)),
                        pltpu.SemaphoreType.DMA((2,)),
                        pltpu.SemaphoreType.DMA((2,))],
    )
    return pl.pallas_call(
        functools.partial(_ffn_kernel, layer=p["layer"]),
        out_shape=jax.ShapeDtypeStruct((SORTED_ROWS // SEG_ALIGN, SEG_ALIGN, D_MODEL), BF16),
        grid_spec=grid_spec,
        compiler_params=_cparams(("arbitrary",)),
        name="moe_expert_ffn",
    )(plan["n_used"], plan["t_exp"], plan["t_valid"], plan["t_run"], plan["run_exp"], plan["n_runs"],
      xs, p["w_exp_gate"], p["w_exp_up"], p["w_exp_down"])


def _combine_kernel(nc_ref, gs_ref, y_ref, rktok_ref, srow_ref, scol_ref, h_ref, x1_ref, mod_ref,
                    sgu_ref, sd_ref, o_ref, yl_ref, acc_ref, sem):
    i = pl.program_id(0)
    slot = i % 2
    pslot = 1 - slot

    def gather_chunk(tile, slot_, c):
        for k in range(GROUPS_PER_CHUNK):
            g = c * GROUPS_PER_CHUNK + k
            pltpu.make_async_copy(y_ref.at[gs_ref[tile * MAX_GROUPS + g]], yl_ref.at[slot_, g],
                                  sem.at[slot_]).start()

    def wait_chunks(n, slot_):
        def body(c, carry):
            pltpu.make_async_copy(y_ref.at[pl.ds(0, GROUPS_PER_CHUNK)],
                                  yl_ref.at[slot_, pl.ds(0, GROUPS_PER_CHUNK)], sem.at[slot_]).wait()
            return carry
        lax.fori_loop(0, n, body, 0)

    @pl.when(i == 0)
    def _():
        yl_ref[...] = jnp.zeros(yl_ref.shape, BF16)

        def first(c, carry):
            gather_chunk(0, 0, c)
            return carry

        lax.fori_loop(0, nc_ref[0], first, 0)

    h = h_ref[...]
    hs = jnp.dot(h, sgu_ref[...], preferred_element_type=F32)
    gs = hs[:, :EXPERT_FF]
    a = (gs * jax.nn.sigmoid(gs)) * hs[:, EXPERT_FF:]
    shared = jnp.dot(a.astype(BF16), sd_ref[...], preferred_element_type=F32)

    n_c = nc_ref[i]
    ls_col = scol_ref[:, 0:1]
    le_col = scol_ref[:, 1:2]
    ls_row8 = jnp.broadcast_to(srow_ref[0:1, :], (8, N_EXPERTS)).astype(BF16)
    rk_ls = jnp.concatenate([rktok_ref[...], ls_row8], axis=0)
    rio_e0 = lax.broadcasted_iota(jnp.int32, (N_EXPERTS, CHUNK_ROWS), 1).astype(F32)
    rio_r0 = lax.broadcasted_iota(jnp.int32, (1, CHUNK_ROWS), 1).astype(F32)

    def select_t(r0f):
        rio_e = rio_e0 + r0f
        gt = jnp.where((rio_e >= ls_col) & (rio_e < le_col), 1.0, 0.0).astype(BF16)
        both = jnp.dot(rk_ls, gt, preferred_element_type=F32)
        want = (rio_r0 + r0f) - both[TT:TT + 1, :]
        return jnp.where(both[:TT, :] == want, 1.0, 0.0).astype(BF16)

    static_rows = STATIC_CHUNKS * CHUNK_ROWS
    pts = []
    for c in range(STATIC_CHUNKS):
        pts.append(select_t(float(c * CHUNK_ROWS)))
        gather_chunk(i + 1, pslot, c)
    pt_all = jnp.concatenate(pts, axis=1)

    def gather_more(c, carry):
        gather_chunk(i + 1, pslot, c)
        return carry

    lax.fori_loop(STATIC_CHUNKS, nc_ref[i + 1], gather_more, 0)
    wait_chunks(n_c, slot)
    y_static = yl_ref[slot, pl.ds(0, STATIC_CHUNKS * GROUPS_PER_CHUNK)].reshape(static_rows, D_MODEL)
    acc_ref[...] = shared + jnp.dot(pt_all, y_static, preferred_element_type=F32)

    def chunk(c, carry):
        g0 = pl.multiple_of(c * GROUPS_PER_CHUNK, GROUPS_PER_CHUNK)
        y_c = yl_ref[slot, pl.ds(g0, GROUPS_PER_CHUNK)].reshape(CHUNK_ROWS, D_MODEL)
        acc_ref[...] += jnp.dot(select_t((c * CHUNK_ROWS).astype(F32)), y_c, preferred_element_type=F32)
        return carry

    lax.fori_loop(STATIC_CHUNKS, n_c, chunk, 0)
    o_ref[...] = x1_ref[...] + mod_ref[5:6, :] * acc_ref[...]

    @pl.when(i == N_TILES - 1)
    def _():
        wait_chunks(nc_ref[i + 1], pslot)


def _combine(plan, y, rktok, h2aug, x1, mod_l, p):
    tiles_per_seq = SEQ // TT
    grid_spec = pltpu.PrefetchScalarGridSpec(
        num_scalar_prefetch=2,
        grid=(N_TILES,),
        in_specs=[pl.BlockSpec(memory_space=pl.ANY),
                  pl.BlockSpec((TT, N_EXPERTS), lambda i, *_: (i, 0)),
                  pl.BlockSpec((None, 2, N_EXPERTS), lambda i, *_: (i, 0, 0)),
                  pl.BlockSpec((None, N_EXPERTS, 2), lambda i, *_: (i, 0, 0)),
                  pl.BlockSpec((TT, D_MODEL), lambda i, *_: (i, 0)),
                  pl.BlockSpec((TT, D_MODEL), lambda i, *_: (i, 0)),
                  pl.BlockSpec((None, 6, D_MODEL), lambda i, *_: (i // tiles_per_seq, 0, 0)),
                  pl.BlockSpec((D_MODEL, 2 * EXPERT_FF), lambda i, *_: (0, 0)),
                  pl.BlockSpec((EXPERT_FF, D_MODEL), lambda i, *_: (0, 0))],
        out_specs=pl.BlockSpec((TT, D_MODEL), lambda i, *_: (i, 0)),
        scratch_shapes=[pltpu.VMEM((2, MAX_GROUPS, SEG_ALIGN, D_MODEL), BF16),
                        pltpu.VMEM((TT, D_MODEL), F32),
                        pltpu.SemaphoreType.DMA((2,))],
    )
    return pl.pallas_call(
        _combine_kernel,
        out_shape=jax.ShapeDtypeStruct((N_TOK, D_MODEL), F32),
        grid_spec=grid_spec,
        compiler_params=_cparams(("arbitrary",)),
        name="moe_combine",
    )(plan["move_in"], plan["group_src"], y, rktok, plan["seg_row"], plan["seg_col"], h2aug, x1,
      mod_l, p["w_sh_gu"], p["w_sh_down"])


def _moe(h2aug, rkt, rktok, cnt, x1, mod_l, p):
    plan = _moe_plan(cnt[:, :, 0])
    xs = _dispatch(plan, h2aug, rkt)
    y = _expert_ffn(plan, xs, p)
    return _combine(plan, y, rktok, h2aug, x1, mod_l, p)


def _pad_heads(w, per_head):
    k = w.shape[0]
    w = w.reshape(k, MLA_HEADS, per_head)
    w = jnp.pad(w, ((0, 0), (0, 0), (0, HEAD_PAD - per_head)))
    return w.reshape(k, MLA_HEADS * HEAD_PAD)


def _layer_params(l, a):
    w_in = a["w_in"][l]
    half = QK_ROPE // 2
    pe_cols = jnp.zeros((D_MODEL, LANES), F32).at[:, QK_NOPE:QK_DIM].set(w_in[:, O_PE:])
    pe_cols = pe_cols.at[:, QK_DIM:QK_DIM + half].set(w_in[:, O_PE:O_PE + half])
    w_in_p = jnp.concatenate([w_in[:, :O_PE], pe_cols], axis=1).astype(BF16)
    w_q = a["w_q_up"][l].reshape(Q_LORA, MLA_HEADS, QK_DIM)
    w_q = jnp.concatenate([w_q, w_q[:, :, QK_NOPE:QK_NOPE + half],
                           jnp.zeros((Q_LORA, MLA_HEADS, HEAD_PAD - QK_DIM - half), F32)], axis=-1)
    w_kv = a["w_kv_up"][l].reshape(KV_LORA, MLA_HEADS, QK_NOPE + V_DIM)
    w_k = _pad_heads(w_kv[:, :, :QK_NOPE].reshape(KV_LORA, MLA_HEADS * QK_NOPE), QK_NOPE)
    w_v = w_kv[:, :, QK_NOPE:].reshape(KV_LORA, MLA_HEADS * V_DIM)
    blocks = TT // GM_BLOCK
    eye = jnp.eye(blocks, dtype=F32)
    ws_big = jnp.einsum("ab,gts->gatbs", eye, a["gm_ws"][l]).reshape(GM_GROUPS, TT, TT)
    bs_tile = jnp.tile(jnp.repeat(a["gm_bs"][l].T, GM_CH, axis=1), (TM // GM_BLOCK, 1))
    pad96 = lambda g: jnp.concatenate(
        [g, g[QK_NOPE:QK_NOPE + half], jnp.zeros((LANES - QK_DIM - half,), F32)]).reshape(1, LANES)
    return {
        "norm_mix_g": a["norm_mix_g"][l].reshape(1, D_MODEL),
        "norm_ffn_g": a["norm_ffn_g"][l].reshape(1, D_MODEL),
        "w_in": w_in_p,
        "gm_ln_g": a["gm_ln_g"][l].reshape(1, GM_WIDTH),
        "gm_ln_b": a["gm_ln_b"][l].reshape(1, GM_WIDTH),
        "gm_ws": ws_big.astype(BF16),
        "gm_bs": bs_tile,
        "q_a_g": a["q_a_g"][l].reshape(1, Q_LORA),
        "w_q_up": w_q.reshape(Q_LORA, MLA_HEADS * HEAD_PAD).astype(BF16),
        "kv_a_g": a["kv_a_g"][l].reshape(1, KV_LORA),
        "w_k_up": w_k.astype(BF16),
        "w_v_t": jnp.pad(w_v.T.reshape(MLA_HEADS, V_DIM, KV_LORA), ((0, 0), (0, V_AUG - V_DIM), (0, 0))
                         ).reshape(MLA_HEADS * V_AUG, KV_LORA).astype(BF16),
        "v_ones": jnp.tile((jnp.arange(V_AUG) == V_DIM).astype(F32), MLA_HEADS).reshape(-1, 1),
        "q_norm_g": pad96(a["q_norm_g"][l]),
        "k_norm_g": pad96(a["k_norm_g"][l]),
        "out_norm_gm": a["out_norm_g"][l, :GM_WIDTH].reshape(1, GM_WIDTH),
        "out_norm_mla": a["out_norm_g"][l, GM_WIDTH:].reshape(1, MLA_WIDTH),
        "w_out": a["w_out"][l].astype(BF16),
        "w_router_t": a["w_router"][l].T,
        "router_bias": a["router_bias"][l].reshape(N_EXPERTS, 1),
        "layer": l,
        "w_exp_gate": a["w_exp_gate"],
        "w_exp_up": a["w_exp_up"],
        "w_exp_down": a["w_exp_down"],
        "w_sh_gu": jnp.concatenate([a["w_sh_gate"][l], a["w_sh_up"][l]], axis=-1).astype(BF16),
        "w_sh_down": a["w_sh_down"][l].astype(BF16),
    }


def kernel(x, c, positions, norm_mix_g, norm_ffn_g, w_ada, b_ada, w_in, gm_ln_g, gm_ln_b, gm_ws, gm_bs, q_a_g, w_q_up, kv_a_g, w_kv_up, q_norm_g, k_norm_g, out_norm_g, w_out, w_router, router_bias, w_exp_gate, w_exp_up, w_exp_down, w_sh_gate, w_sh_up, w_sh_down):
    a = dict(norm_mix_g=norm_mix_g, norm_ffn_g=norm_ffn_g, w_in=w_in, gm_ln_g=gm_ln_g,
             gm_ln_b=gm_ln_b, gm_ws=gm_ws, gm_bs=gm_bs, q_a_g=q_a_g, w_q_up=w_q_up,
             kv_a_g=kv_a_g, w_kv_up=w_kv_up, q_norm_g=q_norm_g, k_norm_g=k_norm_g,
             out_norm_g=out_norm_g, w_out=w_out, w_router=w_router, router_bias=router_bias,
             w_exp_gate=w_exp_gate, w_exp_up=w_exp_up, w_exp_down=w_exp_down,
             w_sh_gate=w_sh_gate, w_sh_up=w_sh_up, w_sh_down=w_sh_down)
    cos_t, sin_t = _rope_tables(positions)
    mod = _ada_mod(c, w_ada, b_ada).reshape(DEPTH, BATCH, 6, D_MODEL)
    xt = x.reshape(N_TOK, D_MODEL)
    for l in range(DEPTH):
        p = _layer_params(l, a)
        gm, q, k, v = _mix_in(xt, mod[l], p, cos_t, sin_t)
        mla = _attention(q, k, v).reshape(N_TOK, MLA_WIDTH)
        x1, h2aug, rkt, rktok, cnt = _mix_out(xt, gm, mla, mod[l], p)
        xt = _moe(h2aug, rkt, rktok, cnt, x1, mod[l], p)
    return xt.reshape(BATCH, SEQ, D_MODEL)
```

```python
import functools

import jax
import jax.numpy as jnp
from jax import lax
from jax.experimental import pallas as pl
from jax.experimental.pallas import tpu as pltpu

D_MODEL = 1024
BATCH = 8
SEQ = 4096
DEPTH = 2
N_TOK = BATCH * SEQ
CHUNK = 64
GM_WIDTH = 512
GM_BLOCK = 128
GM_GROUPS = 4
GM_CH = 128
MLA_HEADS = 8
QK_NOPE = 64
QK_ROPE = 32
QK_DIM = 96
V_DIM = 64
MLA_WIDTH = 512
Q_LORA = 384
KV_LORA = 256
ROPE_THETA = 10000.0
N_EXPERTS = 64
TOP_K = 8
N_GROUPS = 8
GROUP_SIZE = N_EXPERTS // N_GROUPS
TOPK_GROUPS = 4
EXPERT_FF = 256
ROUTED_SCALE = 2.5
EPS = 1e-6

LANES = 128
HEAD_PAD = LANES
IN_COLS_PAD = 2 * GM_WIDTH + Q_LORA + KV_LORA + LANES
O_Q = 2 * GM_WIDTH
O_KV = O_Q + Q_LORA
O_PE = O_KV + KV_LORA

TT = 256
TM = 512
TQ = 512
TK = 256
ATT_HEADS = 8
V_AUG = 80
N_TILES = N_TOK // TT
H_AUG = D_MODEL + LANES
SEG_ALIGN = 16
CHUNK_ROWS = 256
LOCAL_ROWS = TT * TOP_K + N_EXPERTS * SEG_ALIGN
MAX_GROUPS = LOCAL_ROWS // SEG_ALIGN
TM_FFN = 512
FFN_RING = 3
SORTED_ROWS = -(-(N_TOK * TOP_K + N_TILES * N_EXPERTS * (SEG_ALIGN - 1)
                  + N_EXPERTS * (TM_FFN - 1)) // TM_FFN) * TM_FFN
FFN_TILES = SORTED_ROWS // TM_FFN
GROUPS_PER_CHUNK = CHUNK_ROWS // SEG_ALIGN
TRASH_ROWS = 2 * LOCAL_ROWS
STATIC_CHUNKS = 10
VMEM_LIMIT = 56 * 1024 * 1024

F32 = jnp.float32
BF16 = jnp.bfloat16


def _cparams(sem, flags=None):
    return pltpu.CompilerParams(dimension_semantics=sem, vmem_limit_bytes=VMEM_LIMIT, flags=flags)


def _rms_scale(v, width):
    return lax.rsqrt(jnp.sum(v * v, axis=-1, keepdims=True) * (1.0 / width) + EPS)


def _rope_kernel(pos_ref, inv_ref, cos_ref, sin_ref, nsin_ref):
    ang = pos_ref[...].astype(F32) * inv_ref[...]
    s = jnp.sin(ang)
    cos_ref[...] = jnp.cos(ang)
    sin_ref[...] = s
    nsin_ref[...] = -s


def _rope_tables(positions):
    half = QK_ROPE // 2
    per_row = LANES // half
    inv_freq = 1.0 / (ROPE_THETA ** (jnp.arange(0, QK_ROPE, 2, dtype=F32) / QK_ROPE))
    inv = jnp.tile(inv_freq, per_row).reshape(1, LANES)
    rows = N_TOK // per_row
    pos = jnp.repeat(positions.reshape(rows, per_row), half, axis=1)
    tile = rows // 2
    packed = jax.ShapeDtypeStruct((rows, LANES), F32)
    spec = pl.BlockSpec((tile, LANES), lambda i: (i, 0))
    c, s, ns = pl.pallas_call(
        _rope_kernel,
        out_shape=(packed, packed, packed),
        grid=(rows // tile,),
        in_specs=[spec, pl.BlockSpec((1, LANES), lambda i: (0, 0))],
        out_specs=(spec, spec, spec),
        compiler_params=_cparams(("arbitrary",)),
        name="rope_tables",
    )(pos, inv)
    c, s, ns = (t.reshape(N_TOK, half) for t in (c, s, ns))
    ones = jnp.ones((N_TOK, QK_NOPE), F32)
    cos_t = jnp.concatenate([ones, c, c, jnp.zeros((N_TOK, LANES - QK_DIM), F32)], axis=1)
    sin_t = jnp.concatenate([jnp.zeros((N_TOK, QK_NOPE), F32), ns, s,
                             jnp.zeros((N_TOK, LANES - QK_DIM), F32)], axis=1)
    return cos_t, sin_t


def _ada_kernel(c_ref, w_ref, b_ref, o_ref):
    c = c_ref[...]
    ca = c * jax.nn.sigmoid(c)
    o_ref[...] = jnp.dot(ca, w_ref[...], precision=lax.Precision.HIGHEST,
                         preferred_element_type=F32) + b_ref[...]


def _ada_mod(c, w_ada, b_ada):
    return pl.pallas_call(
        _ada_kernel,
        out_shape=jax.ShapeDtypeStruct((DEPTH, BATCH, 6 * D_MODEL), F32),
        grid=(DEPTH, 6),
        in_specs=[pl.BlockSpec((BATCH, D_MODEL), lambda l, j: (0, 0)),
                  pl.BlockSpec((None, D_MODEL, D_MODEL), lambda l, j: (l, 0, j)),
                  pl.BlockSpec((None, 1, D_MODEL), lambda l, j: (l, 0, j))],
        out_specs=pl.BlockSpec((None, BATCH, D_MODEL), lambda l, j: (l, 0, j)),
        compiler_params=_cparams(("arbitrary", "arbitrary")),
        name="ada_mod",
    )(c, w_ada, b_ada.reshape(DEPTH, 1, 6 * D_MODEL))


def _rope_rot(t, cos_t, sin_t):
    return t * cos_t + pltpu.roll(t, LANES - 16, 1) * sin_t


def _mix_in_kernel(x_ref, mod_ref, ng_ref, win_ref, lng_ref, lnb_ref, ws_ref, bs_ref,
                   qag_ref, wq_ref, kvg_ref, wkv_ref, wvt_ref, vone_ref, qng_ref, kng_ref, og_ref,
                   cos_ref, sin_ref,
                   gm_ref, q_ref, k_ref, vt_ref):
    x = x_ref[...]
    sh1 = mod_ref[0:1, :]
    sc1 = mod_ref[1:2, :]
    h = x * _rms_scale(x, D_MODEL) * ng_ref[...]
    h = h * (1.0 + sc1) + sh1
    proj = jnp.dot(h.astype(BF16), win_ref[...], preferred_element_type=F32)

    uv_pre = proj[:, :2 * GM_WIDTH]
    uv = 0.5 * uv_pre * (1.0 + lax.erf(uv_pre * (2.0 ** -0.5)))
    row = lax.broadcasted_iota(jnp.int32, (TT, TT), 0)
    col = lax.broadcasted_iota(jnp.int32, (TT, TT), 1)
    causal = (col <= row) & ((col // GM_BLOCK) == (row // GM_BLOCK))
    groups = range(GM_GROUPS)
    gsl = [slice(g * GM_CH, (g + 1) * GM_CH) for g in groups]
    vg = [uv[:, GM_WIDTH + g * GM_CH:GM_WIDTH + (g + 1) * GM_CH] for g in groups]
    mu = [jnp.mean(vg[g], axis=-1, keepdims=True) for g in groups]
    xc = [vg[g] - mu[g] for g in groups]
    var = [jnp.mean(xc[g] * xc[g], axis=-1, keepdims=True) for g in groups]
    vln = [(xc[g] * lax.rsqrt(var[g] + EPS) * lng_ref[:, gsl[g]] + lnb_ref[:, gsl[g]]).astype(BF16)
           for g in groups]
    ws = [jnp.where(causal, ws_ref[g], jnp.zeros((), BF16)) for g in groups]
    gate = [jnp.concatenate([jnp.dot(ws[g], vln[g][s * TT:(s + 1) * TT], preferred_element_type=F32)
                             for s in range(TM // TT)], axis=0) for g in groups]
    outs = [uv[:, gsl[g]] * (gate[g] + bs_ref[:, gsl[g]]) for g in groups]
    sq = [jnp.sum(outs[g] * outs[g], axis=-1, keepdims=True) for g in groups]
    rs = lax.rsqrt((sq[0] + sq[1] + sq[2] + sq[3]) * (1.0 / GM_WIDTH) + EPS)
    for g in groups:
        gm_ref[:, gsl[g]] = (outs[g] * rs * og_ref[:, gsl[g]]).astype(BF16)

    cos_t = cos_ref[...]
    sin_t = sin_ref[...]
    lane = lax.broadcasted_iota(jnp.int32, (TM, LANES), 1)
    ql = proj[:, O_Q:O_KV]
    qn = ql * _rms_scale(ql, Q_LORA) * qag_ref[...]
    qa = jnp.dot(qn.astype(BF16), wq_ref[...], preferred_element_type=F32)
    kl = proj[:, O_KV:O_PE]
    kn = kl * _rms_scale(kl, KV_LORA) * kvg_ref[...]
    kva = jnp.dot(kn.astype(BF16), wkv_ref[...], preferred_element_type=F32)
    pe = proj[:, O_PE:O_PE + LANES]
    real = lane < QK_DIM
    pe_ssq = jnp.sum(jnp.where(real, pe * pe, 0.0), axis=-1, keepdims=True)
    pe_rot = _rope_rot(pe * kng_ref[...], cos_t, sin_t)
    qscale = (QK_DIM ** -0.5) * 1.4426950408889634
    v_t = lax.dot_general(wvt_ref[...], kn.astype(BF16), (((1,), (1,)), ((), ())),
                          preferred_element_type=F32) + vone_ref[...]
    vt_ref[...] = v_t.reshape(MLA_HEADS, V_AUG, TM).astype(BF16)
    heads = range(MLA_HEADS)
    qh = [qa[:, hd * HEAD_PAD:(hd + 1) * HEAD_PAD] for hd in heads]
    kh = [kva[:, hd * HEAD_PAD:(hd + 1) * HEAD_PAD] for hd in heads]
    q_ssq = [jnp.sum(jnp.where(real, qh[hd] * qh[hd], 0.0), axis=-1, keepdims=True) for hd in heads]
    k_ssq = [jnp.sum(kh[hd] * kh[hd], axis=-1, keepdims=True) for hd in heads]
    qg = [qh[hd] * lax.rsqrt(q_ssq[hd] * (1.0 / QK_DIM) + EPS) * qng_ref[...] for hd in heads]
    q_sw = [pltpu.roll(qg[hd], LANES - 16, 1) for hd in heads]
    for hd in heads:
        q_ref[hd] = ((qg[hd] * cos_t + q_sw[hd] * sin_t) * qscale).astype(BF16)
    for hd in heads:
        krs = lax.rsqrt((k_ssq[hd] + pe_ssq) * (1.0 / QK_DIM) + EPS)
        k_ref[hd] = ((kh[hd] * kng_ref[...] + pe_rot) * krs).astype(BF16)


def _mix_in(x, mod_l, p, cos_t, sin_t):
    tiles_per_seq = SEQ // TM
    full = lambda shape: pl.BlockSpec(shape, lambda i: (0,) * len(shape))
    head_spec = pl.BlockSpec((None, MLA_HEADS, TM, HEAD_PAD),
                             lambda i: (i // tiles_per_seq, 0, i % tiles_per_seq, 0))
    head_shape = jax.ShapeDtypeStruct((BATCH, MLA_HEADS, SEQ, HEAD_PAD), BF16)
    return pl.pallas_call(
        _mix_in_kernel,
        out_shape=(jax.ShapeDtypeStruct((N_TOK, GM_WIDTH), BF16), head_shape, head_shape,
                   jax.ShapeDtypeStruct((BATCH, MLA_HEADS, V_AUG, SEQ), BF16)),
        grid=(N_TOK // TM,),
        in_specs=[pl.BlockSpec((TM, D_MODEL), lambda i: (i, 0)),
                  pl.BlockSpec((None, 6, D_MODEL), lambda i: (i // tiles_per_seq, 0, 0)),
                  full((1, D_MODEL)),
                  full((D_MODEL, IN_COLS_PAD)),
                  full((1, GM_WIDTH)), full((1, GM_WIDTH)),
                  full((GM_GROUPS, TT, TT)), full((TM, GM_WIDTH)),
                  full((1, Q_LORA)), full((Q_LORA, MLA_HEADS * HEAD_PAD)),
                  full((1, KV_LORA)), full((KV_LORA, MLA_HEADS * HEAD_PAD)),
                  full((MLA_HEADS * V_AUG, KV_LORA)), full((MLA_HEADS * V_AUG, 1)),
                  full((1, LANES)), full((1, LANES)), full((1, GM_WIDTH)),
                  pl.BlockSpec((TM, LANES), lambda i: (i, 0)),
                  pl.BlockSpec((TM, LANES), lambda i: (i, 0))],
        out_specs=(pl.BlockSpec((TM, GM_WIDTH), lambda i: (i, 0)), head_spec, head_spec,
                   pl.BlockSpec((None, MLA_HEADS, V_AUG, TM),
                                lambda i: (i // tiles_per_seq, 0, 0, i % tiles_per_seq))),
        compiler_params=_cparams(("arbitrary",)),
        name="mix_in",
    )(x, mod_l, p["norm_mix_g"], p["w_in"], p["gm_ln_g"], p["gm_ln_b"], p["gm_ws"], p["gm_bs"],
      p["q_a_g"], p["w_q_up"], p["kv_a_g"], p["w_k_up"], p["w_v_t"], p["v_ones"], p["q_norm_g"],
      p["k_norm_g"],
      p["out_norm_gm"], cos_t, sin_t)


def _attn_kernel(q_ref, k_ref, vt_ref, o_ref):
    qi = pl.program_id(2)
    krow = lax.broadcasted_iota(jnp.int32, (TK, TQ), 0)
    qcol = lax.broadcasted_iota(jnp.int32, (TK, TQ), 1)
    nt = (((1,), (1,)), ((), ()))
    per_q = TQ // TK

    def tile(j, carry, diag):
        start = pl.multiple_of(j * TK, TK)
        heads = range(ATT_HEADS)
        s = [lax.dot_general(k_ref[hh, pl.ds(start, TK), :], q_ref[hh], nt,
                             preferred_element_type=F32) for hh in heads]
        if diag is not None:
            allowed = ((krow + diag * TK) // CHUNK) <= (qcol // CHUNK)
            s = [jnp.where(allowed, sh, -jnp.inf) for sh in s]
        m_new = [jnp.maximum(carry[hh][0], jnp.max(s[hh], axis=0, keepdims=True)) for hh in heads]
        p = [jnp.exp2(s[hh] - m_new[hh]) for hh in heads]
        alpha = [jnp.exp2(carry[hh][0] - m_new[hh]) for hh in heads]
        pv = [jnp.dot(vt_ref[hh, :, pl.ds(start, TK)], p[hh].astype(BF16),
                      preferred_element_type=F32) for hh in heads]
        return tuple((m_new[hh], alpha[hh] * carry[hh][1] + pv[hh]) for hh in heads)

    init = tuple((jnp.full((1, TQ), -jnp.inf, F32), jnp.zeros((V_AUG, TQ), F32))
                 for _ in range(ATT_HEADS))
    carry = lax.fori_loop(0, qi * per_q, functools.partial(tile, diag=None), init)
    for t in range(per_q):
        carry = tile(qi * per_q + t, carry, t)
    out_t = jnp.concatenate([acc[:V_DIM] / acc[V_DIM:V_DIM + 1] for (_, acc) in carry], axis=0)
    o_ref[...] = out_t.T


def _attention(q, k, vt):
    groups = MLA_HEADS // ATT_HEADS
    qspec = pl.BlockSpec((None, ATT_HEADS, TQ, HEAD_PAD), lambda b, hp, i: (b, hp, i, 0))
    kspec = pl.BlockSpec((None, ATT_HEADS, SEQ, HEAD_PAD), lambda b, hp, i: (b, hp, 0, 0))
    vspec = pl.BlockSpec((None, ATT_HEADS, V_AUG, SEQ), lambda b, hp, i: (b, hp, 0, 0))
    return pl.pallas_call(
        _attn_kernel,
        out_shape=jax.ShapeDtypeStruct((BATCH, SEQ, MLA_WIDTH), F32),
        grid=(BATCH, groups, SEQ // TQ),
        in_specs=[qspec, kspec, vspec],
        out_specs=pl.BlockSpec((None, TQ, ATT_HEADS * V_DIM), lambda b, hp, i: (b, i, hp)),
        compiler_params=_cparams(("arbitrary", "arbitrary", "arbitrary")),
        name="mla_attention",
    )(q, k, vt)


def _route(logits_t, bias_col):
    t = logits_t.shape[1]
    scores = jax.nn.sigmoid(logits_t)
    biased = scores + bias_col
    neg = -jnp.inf
    b3 = biased.reshape(N_GROUPS, GROUP_SIZE, t)
    m1 = jnp.max(b3, axis=1, keepdims=True)
    n_max = jnp.sum((b3 == m1).astype(F32), axis=1, keepdims=True)
    m2 = jnp.max(jnp.where(b3 < m1, b3, neg), axis=1, keepdims=True)
    gs = (m1 + jnp.where(n_max >= 2.0, m1, m2)).reshape(N_GROUPS, t)
    gidx = lax.broadcasted_iota(jnp.int32, (N_GROUPS, t), 0)
    grank = jnp.zeros((N_GROUPS, t), F32)
    for g in range(N_GROUPS):
        other = gs[g:g + 1, :]
        ahead = (other > gs) | ((other == gs) & (gidx > g))
        grank = grank + ahead.astype(F32)
    gsel = grank < float(TOPK_GROUPS)
    emask = jnp.broadcast_to(gsel.reshape(N_GROUPS, 1, t), (N_GROUPS, GROUP_SIZE, t)).reshape(N_EXPERTS, t)
    ms = jnp.where(emask, biased, neg)
    eidx = lax.broadcasted_iota(jnp.int32, (N_EXPERTS, t), 0).astype(F32)
    picked = jnp.zeros((N_EXPERTS, t), F32)
    for _ in range(TOP_K):
        top = jnp.max(ms, axis=0, keepdims=True)
        first = jnp.min(jnp.where(ms == top, eidx, float(N_EXPERTS)), axis=0, keepdims=True)
        hit = eidx == first
        picked = jnp.where(hit, 1.0, picked)
        ms = jnp.where(hit, neg, ms)
    sel = picked > 0.5
    w = jnp.where(sel, scores, 0.0)
    denom = jnp.sum(w, axis=0, keepdims=True)
    return (w / denom) * ROUTED_SCALE, sel


def _mix_out_kernel(x_ref, gm_ref, mla_ref, mod_ref, og_ref, wout_ref, fg_ref, wr_ref, rb_ref,
                    x1_ref, h2_ref, rkt_ref, rktok_ref, cnt_ref):
    mla = mla_ref[...]
    mla_n = mla * _rms_scale(mla, MLA_WIDTH) * og_ref[...]
    mixed = jnp.concatenate([gm_ref[...], mla_n.astype(BF16)], axis=-1)
    y = jnp.dot(mixed, wout_ref[...], preferred_element_type=F32)
    g1 = mod_ref[2:3, :]
    sh2 = mod_ref[3:4, :]
    sc2 = mod_ref[4:5, :]
    x1 = x_ref[...] + g1 * y
    x1_ref[...] = x1
    h2 = x1 * _rms_scale(x1, D_MODEL) * fg_ref[...]
    h2 = h2 * (1.0 + sc2) + sh2
    nt = (((1,), (1,)), ((), ()))
    h_hi = h2.astype(BF16)
    h_lo = (h2 - h_hi.astype(F32)).astype(BF16)
    wr = wr_ref[...]
    w_hi = wr.astype(BF16)
    w_lo = (wr - w_hi.astype(F32)).astype(BF16)
    logits_t = (lax.dot_general(w_hi, h_hi, nt, preferred_element_type=F32)
                + lax.dot_general(w_hi, h_lo, nt, preferred_element_type=F32)
                + lax.dot_general(w_lo, h_hi, nt, preferred_element_type=F32))
    comb, sel = _route(logits_t, rb_ref[...])
    before = (lax.broadcasted_iota(jnp.int32, (TT, TT), 0)
              < lax.broadcasted_iota(jnp.int32, (TT, TT), 1))
    self = jnp.where(sel, 1.0, 0.0)
    before_b = jnp.where(before, 1.0, 0.0).astype(BF16)
    rank = jnp.concatenate(
        [jnp.dot(self[:, s * TT:(s + 1) * TT].astype(BF16), before_b, preferred_element_type=F32)
         for s in range(TM // TT)], axis=1)
    rk = jnp.where(sel, rank, -1.0)
    rkt_ref[...] = rk.astype(BF16)
    rktok_ref[...] = rk.T.astype(BF16)
    for s in range(TM // TT):
        cnt_ref[s] = jnp.broadcast_to(
            jnp.sum(self[:, s * TT:(s + 1) * TT], axis=1, keepdims=True), (N_EXPERTS, LANES))
    comb_tok = comb.T
    hi = comb_tok.astype(BF16)
    lo = (comb_tok - hi.astype(F32)).astype(BF16)
    h2_ref[:, :D_MODEL] = h2.astype(BF16)
    h2_ref[:, D_MODEL:] = jnp.concatenate([hi, lo], axis=-1)


def _mix_out(x, gm, mla, mod_l, p):
    tiles_per_seq = SEQ // TM
    full = lambda shape: pl.BlockSpec(shape, lambda i: (0,) * len(shape))
    tok = lambda width: pl.BlockSpec((TM, width), lambda i: (i, 0))
    return pl.pallas_call(
        _mix_out_kernel,
        out_shape=(jax.ShapeDtypeStruct((N_TOK, D_MODEL), F32),
                   jax.ShapeDtypeStruct((N_TOK, H_AUG), BF16),
                   jax.ShapeDtypeStruct((N_EXPERTS, N_TOK), BF16),
                   jax.ShapeDtypeStruct((N_TOK, N_EXPERTS), BF16),
                   jax.ShapeDtypeStruct((N_TILES, N_EXPERTS, LANES), F32)),
        grid=(N_TOK // TM,),
        in_specs=[tok(D_MODEL), tok(GM_WIDTH), tok(MLA_WIDTH),
                  pl.BlockSpec((None, 6, D_MODEL), lambda i: (i // tiles_per_seq, 0, 0)),
                  full((1, MLA_WIDTH)), full((D_MODEL, D_MODEL)), full((1, D_MODEL)),
                  full((N_EXPERTS, D_MODEL)), full((N_EXPERTS, 1))],
        out_specs=(tok(D_MODEL), tok(H_AUG), pl.BlockSpec((N_EXPERTS, TM), lambda i: (0, i)),
                   tok(N_EXPERTS), pl.BlockSpec((TM // TT, N_EXPERTS, LANES), lambda i: (i, 0, 0))),
        compiler_params=_cparams(("arbitrary",)),
        name="mix_out_route",
    )(x, gm, mla, mod_l, p["out_norm_mla"], p["w_out"], p["norm_ffn_g"], p["w_router_t"],
      p["router_bias"])


def _moe_plan(cnt):
    i32 = jnp.int32
    cnt = cnt.astype(i32)
    padlen = (cnt + SEG_ALIGN - 1) // SEG_ALIGN * SEG_ALIGN
    lend = jnp.cumsum(padlen, axis=1)
    lstart = lend - padlen
    ltot = lend[:, -1]
    region = jnp.sum(padlen, axis=0)
    region_pad = (region + TM_FFN - 1) // TM_FFN * TM_FFN
    base = jnp.cumsum(region_pad) - region_pad
    pos = base[None, :] + jnp.cumsum(padlen, axis=0) - padlen
    gidx = jnp.arange(MAX_GROUPS, dtype=i32)
    grow = gidx * SEG_ALIGN
    in_seg = (grow[None, :, None] >= lstart[:, None, :]) & (grow[None, :, None] < lend[:, None, :])
    shift = jnp.sum(jnp.where(in_seg, (pos - lstart)[:, None, :], 0), axis=-1)
    n_groups = ltot // SEG_ALIGN
    valid_g = gidx[None, :] < n_groups[:, None]
    real_dst = (shift + grow[None, :]) // SEG_ALIGN
    trash_base = SORTED_ROWS // SEG_ALIGN
    trash = trash_base + (jnp.arange(N_TILES, dtype=i32)[:, None] % 2) * MAX_GROUPS + gidx[None, :]
    n_chunks = (n_groups + GROUPS_PER_CHUNK - 1) // GROUPS_PER_CHUNK
    n_move = jnp.maximum(n_chunks, STATIC_CHUNKS)
    group_dst = jnp.where(valid_g, real_dst, trash)
    group_src = jnp.where(valid_g, real_dst, 0)
    lead = (trash_base + MAX_GROUPS + gidx)[None, :]
    static_c = jnp.full((1,), STATIC_CHUNKS, i32)
    tile_end = jnp.cumsum(region_pad) // TM_FFN
    n_used = tile_end[-1]
    tj = jnp.arange(FFN_TILES, dtype=i32)
    in_exp = (tj[:, None] * TM_FFN >= base[None, :]) & (tj[:, None] * TM_FFN < (base + region_pad)[None, :])
    t_exp = jnp.sum(jnp.where(in_exp, jnp.arange(N_EXPERTS, dtype=i32)[None, :], 0), axis=-1)
    t_valid = jnp.sum(jnp.where(in_exp, jnp.clip((base + region)[None, :] - tj[:, None] * TM_FFN, 0, TM_FFN), 0),
                      axis=-1)
    has_rows = region_pad > 0
    run_of_exp = jnp.cumsum(has_rows.astype(i32)) - 1
    eidx = jnp.arange(N_EXPERTS, dtype=i32)
    t_run = jnp.sum(jnp.where(in_exp, run_of_exp[None, :], 0), axis=-1)
    run_exp = jnp.sum(jnp.where(has_rows[None, :] & (run_of_exp[None, :] == eidx[:, None]), eidx[None, :], 0),
                      axis=-1)
    lstart_f = lstart.astype(F32)
    lend_f = lend.astype(F32)
    return {
        "t_run": t_run.astype(i32),
        "run_exp": run_exp.astype(i32),
        "n_runs": jnp.sum(has_rows.astype(i32)).reshape(1),
        "move_out": jnp.concatenate([static_c, n_move]).astype(i32),
        "move_in": jnp.concatenate([n_move, static_c]).astype(i32),
        "group_dst": jnp.concatenate([lead, group_dst], axis=0).reshape(-1).astype(i32),
        "group_src": jnp.concatenate([group_src, jnp.zeros((1, MAX_GROUPS), i32)], axis=0
                                     ).reshape(-1).astype(i32),
        "seg_row": jnp.stack([lstart_f, lend_f], axis=1),
        "seg_col": jnp.stack([lstart_f, lend_f], axis=2),
        "n_used": n_used.reshape(1).astype(i32),
        "t_exp": t_exp.astype(i32),
        "t_valid": t_valid.astype(i32),
    }


def _dispatch_kernel(nm_ref, gd_ref, h_ref, rkt_ref, srow_ref, scol_ref, xs_ref, loc_ref, sem):
    i = pl.program_id(0)
    slot = i % 2
    pslot = 1 - slot

    def wait_chunks(n, slot_):
        def body(c, carry):
            pltpu.make_async_copy(loc_ref.at[slot_, pl.ds(0, GROUPS_PER_CHUNK)],
                                  xs_ref.at[pl.ds(0, GROUPS_PER_CHUNK)], sem.at[slot_]).wait()
            return carry
        lax.fori_loop(0, n, body, 0)

    def issue_chunk(entry, slot_, c):
        for k in range(GROUPS_PER_CHUNK):
            g = c * GROUPS_PER_CHUNK + k
            pltpu.make_async_copy(loc_ref.at[slot_, g], xs_ref.at[gd_ref[entry * MAX_GROUPS + g]],
                                  sem.at[slot_]).start()

    @pl.when(i == 0)
    def _():
        loc_ref[1] = jnp.zeros(loc_ref.shape[1:], BF16)

    ls_row = srow_ref[0:1, :]
    le_row = srow_ref[1:2, :]
    ls_rep = jnp.broadcast_to(scol_ref[:, 0:1], (N_EXPERTS, LANES)).astype(BF16)
    rk_ls = jnp.concatenate([rkt_ref[...], ls_rep], axis=1)
    h = h_ref[...]
    n_c = nm_ref[i + 1]
    rio_e0 = lax.broadcasted_iota(jnp.int32, (CHUNK_ROWS, N_EXPERTS), 0).astype(F32)
    rio_t0 = lax.broadcasted_iota(jnp.int32, (CHUNK_ROWS, TT), 0).astype(F32)

    def select(r0f):
        rio_e = rio_e0 + r0f
        g = jnp.where((rio_e >= ls_row) & (rio_e < le_row), 1.0, 0.0).astype(BF16)
        both = jnp.dot(g, rk_ls, preferred_element_type=F32)
        want = (rio_t0 + r0f) - jnp.concatenate([both[:, TT:]] * (TT // LANES), axis=1)
        return jnp.where(both[:, :TT] == want, 1.0, 0.0).astype(BF16)

    def emit(g0, psel):
        loc_ref[slot, pl.ds(g0, GROUPS_PER_CHUNK)] = jnp.dot(
            psel, h, preferred_element_type=F32).astype(BF16).reshape(GROUPS_PER_CHUNK, SEG_ALIGN, H_AUG)

    psels = []
    for c in range(STATIC_CHUNKS):
        psels.append(select(float(c * CHUNK_ROWS)))
        issue_chunk(i, pslot, c)

    def issue_more(c, carry):
        issue_chunk(i, pslot, c)
        return carry

    lax.fori_loop(STATIC_CHUNKS, nm_ref[i], issue_more, 0)

    @pl.when(i >= 1)
    def _():
        wait_chunks(nm_ref[i - 1], slot)

    for c in range(STATIC_CHUNKS):
        emit(c * GROUPS_PER_CHUNK, psels[c])

    def chunk(c, carry):
        emit(pl.multiple_of(c * GROUPS_PER_CHUNK, GROUPS_PER_CHUNK), select((c * CHUNK_ROWS).astype(F32)))
        return carry

    lax.fori_loop(STATIC_CHUNKS, n_c, chunk, 0)

    @pl.when(i == N_TILES - 1)
    def _():
        def issue_last(c, carry):
            issue_chunk(i + 1, slot, c)
            return carry

        lax.fori_loop(0, n_c, issue_last, 0)
        wait_chunks(nm_ref[i], pslot)
        wait_chunks(n_c, slot)


def _dispatch(plan, h2aug, rkt):
    grid_spec = pltpu.PrefetchScalarGridSpec(
        num_scalar_prefetch=2,
        grid=(N_TILES,),
        in_specs=[pl.BlockSpec((TT, H_AUG), lambda i, *_: (i, 0)),
                  pl.BlockSpec((N_EXPERTS, TT), lambda i, *_: (0, i)),
                  pl.BlockSpec((None, 2, N_EXPERTS), lambda i, *_: (i, 0, 0)),
                  pl.BlockSpec((None, N_EXPERTS, 2), lambda i, *_: (i, 0, 0))],
        out_specs=pl.BlockSpec(memory_space=pl.ANY),
        scratch_shapes=[pltpu.VMEM((2, MAX_GROUPS, SEG_ALIGN, H_AUG), BF16),
                        pltpu.SemaphoreType.DMA((2,))],
    )
    return pl.pallas_call(
        _dispatch_kernel,
        out_shape=jax.ShapeDtypeStruct(((SORTED_ROWS + TRASH_ROWS) // SEG_ALIGN, SEG_ALIGN, H_AUG), BF16),
        grid_spec=grid_spec,
        compiler_params=_cparams(("arbitrary",)),
        name="moe_dispatch",
    )(plan["move_out"], plan["group_dst"], h2aug, rkt, plan["seg_row"], plan["seg_col"])


def _ffn_kernel(nu_ref, te_ref, tv_ref, tr_ref, re_ref, nr_ref,
                xs_ref, wg_hbm, wu_hbm, wd_hbm, y_hbm,
                xbuf_ref, ybuf_ref, wg_raw, wu_raw, wd_raw, wgu_ref, wd_ref, xsem, ysem, wsem,
                *, layer):
    n_used = nu_ref[0]
    n_runs = nr_ref[0]

    tile_groups = TM_FFN // SEG_ALIGN

    def groups_of(t):
        return pl.ds(pl.multiple_of(t * tile_groups, tile_groups), tile_groups)

    def x_copy(t):
        slot = t % FFN_RING
        return pltpu.make_async_copy(xs_ref.at[groups_of(t)], xbuf_ref.at[slot], xsem.at[slot])

    def y_copy(t):
        slot = t % 2
        return pltpu.make_async_copy(ybuf_ref.at[slot], y_hbm.at[groups_of(t)], ysem.at[slot])

    def w_copies(r):
        slot = r % 2
        e = re_ref[r]
        return (pltpu.make_async_copy(wg_hbm.at[layer, e], wg_raw.at[slot], wsem.at[slot]),
                pltpu.make_async_copy(wu_hbm.at[layer, e], wu_raw.at[slot], wsem.at[slot]),
                pltpu.make_async_copy(wd_hbm.at[layer, e], wd_raw.at[slot], wsem.at[slot]))

    x_copy(0).start()

    @pl.when(n_used > 1)
    def _():
        x_copy(1).start()

    for cp in w_copies(0):
        cp.start()

    def tile(j, carry):
        e = te_ref[j]
        r = tr_ref[j]

        @pl.when(j + 2 < n_used)
        def _():
            x_copy(j + 2).start()

        @pl.when((j == 0) | (r != tr_ref[jnp.maximum(j - 1, 0)]))
        def _():
            for cp in w_copies(r):
                cp.wait()
            slot = r % 2
            wgu_ref[:, :EXPERT_FF] = wg_raw[slot].astype(BF16)
            wgu_ref[:, EXPERT_FF:] = wu_raw[slot].astype(BF16)
            wd_ref[...] = wd_raw[slot].astype(BF16)

            @pl.when(r + 1 < n_runs)
            def _():
                for cp in w_copies(r + 1):
                    cp.start()

        x_copy(j).wait()

        @pl.when(j >= 2)
        def _():
            y_copy(j - 2).wait()

        xt = xbuf_ref[j % FFN_RING].reshape(TM_FFN, H_AUG)
        rows = lax.broadcasted_iota(jnp.int32, (TM_FFN, 1), 0)
        valid = rows < tv_ref[j]
        x = jnp.where(valid, xt[:, :D_MODEL], jnp.zeros((), BF16))
        lane = lax.broadcasted_iota(jnp.int32, (TM_FFN, LANES), 1)
        mine = ((lane == e) | (lane == e + N_EXPERTS)) & valid
        w = jnp.sum(jnp.where(mine, xt[:, D_MODEL:].astype(F32), 0.0), axis=-1, keepdims=True)
        hgu = jnp.dot(x, wgu_ref[...], preferred_element_type=F32)
        g = hgu[:, :EXPERT_FF]
        a = (g * jax.nn.sigmoid(g)) * hgu[:, EXPERT_FF:] * w
        ybuf_ref[j % 2] = jnp.dot(a.astype(BF16), wd_ref[...], preferred_element_type=F32).astype(
            BF16).reshape(tile_groups, SEG_ALIGN, D_MODEL)
        y_copy(j).start()
        return carry

    lax.fori_loop(0, n_used, tile, 0)

    @pl.when(n_used >= 2)
    def _():
        y_copy(n_used - 2).wait()

    y_copy(n_used - 1).wait()


def _expert_ffn(plan, xs, p):
    any_spec = pl.BlockSpec(memory_space=pl.ANY)
    grid_spec = pltpu.PrefetchScalarGridSpec(
        num_scalar_prefetch=6,
        grid=(1,),
        in_specs=[any_spec, any_spec, any_spec, any_spec],
        out_specs=any_spec,
        scratch_shapes=[pltpu.VMEM((FFN_RING, TM_FFN // SEG_ALIGN, SEG_ALIGN, H_AUG), BF16),
                        pltpu.VMEM((2, TM_FFN // SEG_ALIGN, SEG_ALIGN, D_MODEL), BF16),
                        pltpu.VMEM((2, D_MODEL, EXPERT_FF), F32),
                        pltpu.VMEM((2, D_MODEL, EXPERT_FF), F32),
                        pltpu.VMEM((2, EXPERT_FF, D_MODEL), F32),
                        pltpu.VMEM((D_MODEL, 2 * EXPERT_FF), BF16),
                        pltpu.VMEM((EXPERT_FF, D_MODEL), BF16),
                        pltpu.SemaphoreType.DMA((FFN_RING,)),
                        pltpu.SemaphoreType.DMA((2,)),
                        pltpu.SemaphoreType.DMA((2,))],
    )
    return pl.pallas_call(
        functools.partial(_ffn_kernel, layer=p["layer"]),
        out_shape=jax.ShapeDtypeStruct((SORTED_ROWS // SEG_ALIGN, SEG_ALIGN, D_MODEL), BF16),
        grid_spec=grid_spec,
        compiler_params=_cparams(("arbitrary",)),
        name="moe_expert_ffn",
    )(plan["n_used"], plan["t_exp"], plan["t_valid"], plan["t_run"], plan["run_exp"], plan["n_runs"],
      xs, p["w_exp_gate"], p["w_exp_up"], p["w_exp_down"])


def _combine_kernel(nc_ref, gs_ref, y_ref, rktok_ref, srow_ref, scol_ref, h_ref, x1_ref, mod_ref,
                    sgu_ref, sd_ref, o_ref, yl_ref, acc_ref, sem):
    i = pl.program_id(0)
    slot = i % 2
    pslot = 1 - slot

    def gather_chunk(tile, slot_, c):
        for k in range(GROUPS_PER_CHUNK):
            g = c * GROUPS_PER_CHUNK + k
            pltpu.make_async_copy(y_ref.at[gs_ref[tile * MAX_GROUPS + g]], yl_ref.at[slot_, g],
                                  sem.at[slot_]).start()

    def wait_chunks(n, slot_):
        def body(c, carry):
            pltpu.make_async_copy(y_ref.at[pl.ds(0, GROUPS_PER_CHUNK)],
                                  yl_ref.at[slot_, pl.ds(0, GROUPS_PER_CHUNK)], sem.at[slot_]).wait()
            return carry
        lax.fori_loop(0, n, body, 0)

    @pl.when(i == 0)
    def _():
        yl_ref[...] = jnp.zeros(yl_ref.shape, BF16)

        def first(c, carry):
            gather_chunk(0, 0, c)
            return carry

        lax.fori_loop(0, nc_ref[0], first, 0)

    h = h_ref[...]
    hs = jnp.dot(h, sgu_ref[...], preferred_element_type=F32)
    gs = hs[:, :EXPERT_FF]
    a = (gs * jax.nn.sigmoid(gs)) * hs[:, EXPERT_FF:]
    shared = jnp.dot(a.astype(BF16), sd_ref[...], preferred_element_type=F32)

    n_c = nc_ref[i]
    ls_col = scol_ref[:, 0:1]
    le_col = scol_ref[:, 1:2]
    ls_row8 = jnp.broadcast_to(srow_ref[0:1, :], (8, N_EXPERTS)).astype(BF16)
    rk_ls = jnp.concatenate([rktok_ref[...], ls_row8], axis=0)
    rio_e0 = lax.broadcasted_iota(jnp.int32, (N_EXPERTS, CHUNK_ROWS), 1).astype(F32)
    rio_r0 = lax.broadcasted_iota(jnp.int32, (1, CHUNK_ROWS), 1).astype(F32)

    def select_t(r0f):
        rio_e = rio_e0 + r0f
        gt = jnp.where((rio_e >= ls_col) & (rio_e < le_col), 1.0, 0.0).astype(BF16)
        both = jnp.dot(rk_ls, gt, preferred_element_type=F32)
        want = (rio_r0 + r0f) - both[TT:TT + 1, :]
        return jnp.where(both[:TT, :] == want, 1.0, 0.0).astype(BF16)

    static_rows = STATIC_CHUNKS * CHUNK_ROWS
    pts = []
    for c in range(STATIC_CHUNKS):
        pts.append(select_t(float(c * CHUNK_ROWS)))
        gather_chunk(i + 1, pslot, c)
    pt_all = jnp.concatenate(pts, axis=1)

    def gather_more(c, carry):
        gather_chunk(i + 1, pslot, c)
        return carry

    lax.fori_loop(STATIC_CHUNKS, nc_ref[i + 1], gather_more, 0)
    wait_chunks(n_c, slot)
    y_static = yl_ref[slot, pl.ds(0, STATIC_CHUNKS * GROUPS_PER_CHUNK)].reshape(static_rows, D_MODEL)
    acc_ref[...] = shared + jnp.dot(pt_all, y_static, preferred_element_type=F32)

    def chunk(c, carry):
        g0 = pl.multiple_of(c * GROUPS_PER_CHUNK, GROUPS_PER_CHUNK)
        y_c = yl_ref[slot, pl.ds(g0, GROUPS_PER_CHUNK)].reshape(CHUNK_ROWS, D_MODEL)
        acc_ref[...] += jnp.dot(select_t((c * CHUNK_ROWS).astype(F32)), y_c, preferred_element_type=F32)
        return carry

    lax.fori_loop(STATIC_CHUNKS, n_c, chunk, 0)
    o_ref[...] = x1_ref[...] + mod_ref[5:6, :] * acc_ref[...]

    @pl.when(i == N_TILES - 1)
    def _():
        wait_chunks(nc_ref[i + 1], pslot)


def _combine(plan, y, rktok, h2aug, x1, mod_l, p):
    tiles_per_seq = SEQ // TT
    grid_spec = pltpu.PrefetchScalarGridSpec(
        num_scalar_prefetch=2,
        grid=(N_TILES,),
        in_specs=[pl.BlockSpec(memory_space=pl.ANY),
                  pl.BlockSpec((TT, N_EXPERTS), lambda i, *_: (i, 0)),
                  pl.BlockSpec((None, 2, N_EXPERTS), lambda i, *_: (i, 0, 0)),
                  pl.BlockSpec((None, N_EXPERTS, 2), lambda i, *_: (i, 0, 0)),
                  pl.BlockSpec((TT, D_MODEL), lambda i, *_: (i, 0)),
                  pl.BlockSpec((TT, D_MODEL), lambda i, *_: (i, 0)),
                  pl.BlockSpec((None, 6, D_MODEL), lambda i, *_: (i // tiles_per_seq, 0, 0)),
                  pl.BlockSpec((D_MODEL, 2 * EXPERT_FF), lambda i, *_: (0, 0)),
                  pl.BlockSpec((EXPERT_FF, D_MODEL), lambda i, *_: (0, 0))],
        out_specs=pl.BlockSpec((TT, D_MODEL), lambda i, *_: (i, 0)),
        scratch_shapes=[pltpu.VMEM((2, MAX_GROUPS, SEG_ALIGN, D_MODEL), BF16),
                        pltpu.VMEM((TT, D_MODEL), F32),
                        pltpu.SemaphoreType.DMA((2,))],
    )
    return pl.pallas_call(
        _combine_kernel,
        out_shape=jax.ShapeDtypeStruct((N_TOK, D_MODEL), F32),
        grid_spec=grid_spec,
        compiler_params=_cparams(("arbitrary",)),
        name="moe_combine",
    )(plan["move_in"], plan["group_src"], y, rktok, plan["seg_row"], plan["seg_col"], h2aug, x1,
      mod_l, p["w_sh_gu"], p["w_sh_down"])


def _moe(h2aug, rkt, rktok, cnt, x1, mod_l, p):
    plan = _moe_plan(cnt[:, :, 0])
    xs = _dispatch(plan, h2aug, rkt)
    y = _expert_ffn(plan, xs, p)
    return _combine(plan, y, rktok, h2aug, x1, mod_l, p)


def _pad_heads(w, per_head):
    k = w.shape[0]
    w = w.reshape(k, MLA_HEADS, per_head)
    w = jnp.pad(w, ((0, 0), (0, 0), (0, HEAD_PAD - per_head)))
    return w.reshape(k, MLA_HEADS * HEAD_PAD)


def _layer_params(l, a):
    w_in = a["w_in"][l]
    half = QK_ROPE // 2
    pe_cols = jnp.zeros((D_MODEL, LANES), F32).at[:, QK_NOPE:QK_DIM].set(w_in[:, O_PE:])
    pe_cols = pe_cols.at[:, QK_DIM:QK_DIM + half].set(w_in[:, O_PE:O_PE + half])
    w_in_p = jnp.concatenate([w_in[:, :O_PE], pe_cols], axis=1).astype(BF16)
    w_q = a["w_q_up"][l].reshape(Q_LORA, MLA_HEADS, QK_DIM)
    w_q = jnp.concatenate([w_q, w_q[:, :, QK_NOPE:QK_NOPE + half],
                           jnp.zeros((Q_LORA, MLA_HEADS, HEAD_PAD - QK_DIM - half), F32)], axis=-1)
    w_kv = a["w_kv_up"][l].reshape(KV_LORA, MLA_HEADS, QK_NOPE + V_DIM)
    w_k = _pad_heads(w_kv[:, :, :QK_NOPE].reshape(KV_LORA, MLA_HEADS * QK_NOPE), QK_NOPE)
    w_v = w_kv[:, :, QK_NOPE:].reshape(KV_LORA, MLA_HEADS * V_DIM)
    blocks = TT // GM_BLOCK
    eye = jnp.eye(blocks, dtype=F32)
    ws_big = jnp.einsum("ab,gts->gatbs", eye, a["gm_ws"][l]).reshape(GM_GROUPS, TT, TT)
    bs_tile = jnp.tile(jnp.repeat(a["gm_bs"][l].T, GM_CH, axis=1), (TM // GM_BLOCK, 1))
    pad96 = lambda g: jnp.concatenate(
        [g, g[QK_NOPE:QK_NOPE + half], jnp.zeros((LANES - QK_DIM - half,), F32)]).reshape(1, LANES)
    return {
        "norm_mix_g": a["norm_mix_g"][l].reshape(1, D_MODEL),
        "norm_ffn_g": a["norm_ffn_g"][l].reshape(1, D_MODEL),
        "w_in": w_in_p,
        "gm_ln_g": a["gm_ln_g"][l].reshape(1, GM_WIDTH),
        "gm_ln_b": a["gm_ln_b"][l].reshape(1, GM_WIDTH),
        "gm_ws": ws_big.astype(BF16),
        "gm_bs": bs_tile,
        "q_a_g": a["q_a_g"][l].reshape(1, Q_LORA),
        "w_q_up": w_q.reshape(Q_LORA, MLA_HEADS * HEAD_PAD).astype(BF16),
        "kv_a_g": a["kv_a_g"][l].reshape(1, KV_LORA),
        "w_k_up": w_k.astype(BF16),
        "w_v_t": jnp.pad(w_v.T.reshape(MLA_HEADS, V_DIM, KV_LORA), ((0, 0), (0, V_AUG - V_DIM), (0, 0))
                         ).reshape(MLA_HEADS * V_AUG, KV_LORA).astype(BF16),
        "v_ones": jnp.tile((jnp.arange(V_AUG) == V_DIM).astype(F32), MLA_HEADS).reshape(-1, 1),
        "q_norm_g": pad96(a["q_norm_g"][l]),
        "k_norm_g": pad96(a["k_norm_g"][l]),
        "out_norm_gm": a["out_norm_g"][l, :GM_WIDTH].reshape(1, GM_WIDTH),
        "out_norm_mla": a["out_norm_g"][l, GM_WIDTH:].reshape(1, MLA_WIDTH),
        "w_out": a["w_out"][l].astype(BF16),
        "w_router_t": a["w_router"][l].T,
        "router_bias": a["router_bias"][l].reshape(N_EXPERTS, 1),
        "layer": l,
        "w_exp_gate": a["w_exp_gate"],
        "w_exp_up": a["w_exp_up"],
        "w_exp_down": a["w_exp_down"],
        "w_sh_gu": jnp.concatenate([a["w_sh_gate"][l], a["w_sh_up"][l]], axis=-1).astype(BF16),
        "w_sh_down": a["w_sh_down"][l].astype(BF16),
    }


def kernel(x, c, positions, norm_mix_g, norm_ffn_g, w_ada, b_ada, w_in, gm_ln_g, gm_ln_b, gm_ws, gm_bs, q_a_g, w_q_up, kv_a_g, w_kv_up, q_norm_g, k_norm_g, out_norm_g, w_out, w_router, router_bias, w_exp_gate, w_exp_up, w_exp_down, w_sh_gate, w_sh_up, w_sh_down):
    a = dict(norm_mix_g=norm_mix_g, norm_ffn_g=norm_ffn_g, w_in=w_in, gm_ln_g=gm_ln_g,
             gm_ln_b=gm_ln_b, gm_ws=gm_ws, gm_bs=gm_bs, q_a_g=q_a_g, w_q_up=w_q_up,
             kv_a_g=kv_a_g, w_kv_up=w_kv_up, q_norm_g=q_norm_g, k_norm_g=k_norm_g,
             out_norm_g=out_norm_g, w_out=w_out, w_router=w_router, router_bias=router_bias,
             w_exp_gate=w_exp_gate, w_exp_up=w_exp_up, w_exp_down=w_exp_down,
             w_sh_gate=w_sh_gate, w_sh_up=w_sh_up, w_sh_down=w_sh_down)
    cos_t, sin_t = _rope_tables(positions)
    mod = _ada_mod(c, w_ada, b_ada).reshape(DEPTH, BATCH, 6, D_MODEL)
    xt = x.reshape(N_TOK, D_MODEL)
    for l in range(DEPTH):
        p = _layer_params(l, a)
        gm, q, k, v = _mix_in(xt, mod[l], p, cos_t, sin_t)
        mla = _attention(q, k, v).reshape(N_TOK, MLA_WIDTH)
        x1, h2aug, rkt, rktok, cnt = _mix_out(xt, gm, mla, mod[l], p)
        xt = _moe(h2aug, rkt, rktok, cnt, x1, mod[l], p)
    return xt.reshape(BATCH, SEQ, D_MODEL)
```

```python
import functools

import jax
import jax.numpy as jnp
from jax import lax
from jax.experimental import pallas as pl
from jax.experimental.pallas import tpu as pltpu

D_MODEL = 1024
BATCH = 8
SEQ = 4096
DEPTH = 2
N_TOK = BATCH * SEQ
CHUNK = 64
GM_WIDTH = 512
GM_BLOCK = 128
GM_GROUPS = 4
GM_CH = 128
MLA_HEADS = 8
QK_NOPE = 64
QK_ROPE = 32
QK_DIM = 96
V_DIM = 64
MLA_WIDTH = 512
Q_LORA = 384
KV_LORA = 256
ROPE_THETA = 10000.0
N_EXPERTS = 64
TOP_K = 8
N_GROUPS = 8
GROUP_SIZE = N_EXPERTS // N_GROUPS
TOPK_GROUPS = 4
EXPERT_FF = 256
ROUTED_SCALE = 2.5
EPS = 1e-6

LANES = 128
HEAD_PAD = LANES
IN_COLS_PAD = 2 * GM_WIDTH + Q_LORA + KV_LORA + LANES
O_Q = 2 * GM_WIDTH
O_KV = O_Q + Q_LORA
O_PE = O_KV + KV_LORA

TT = 256
TM = 512
TQ = 512
TK = 256
ATT_HEADS = 8
V_AUG = 80
N_TILES = N_TOK // TT
H_AUG = D_MODEL + LANES
SEG_ALIGN = 16
CHUNK_ROWS = 256
LOCAL_ROWS = TT * TOP_K + N_EXPERTS * SEG_ALIGN
MAX_GROUPS = LOCAL_ROWS // SEG_ALIGN
TM_FFN = 512
FFN_RING = 3
SORTED_ROWS = -(-(N_TOK * TOP_K + N_TILES * N_EXPERTS * (SEG_ALIGN - 1)
                  + N_EXPERTS * (TM_FFN - 1)) // TM_FFN) * TM_FFN
FFN_TILES = SORTED_ROWS // TM_FFN
GROUPS_PER_CHUNK = CHUNK_ROWS // SEG_ALIGN
TRASH_ROWS = 2 * LOCAL_ROWS
STATIC_CHUNKS = 10
VMEM_LIMIT = 56 * 1024 * 1024

F32 = jnp.float32
BF16 = jnp.bfloat16


def _cparams(sem, flags=None):
    return pltpu.CompilerParams(dimension_semantics=sem, vmem_limit_bytes=VMEM_LIMIT, flags=flags)


def _rms_scale(v, width):
    return lax.rsqrt(jnp.sum(v * v, axis=-1, keepdims=True) * (1.0 / width) + EPS)


def _rope_kernel(pos_ref, inv_ref, cos_ref, sin_ref):
    ang = pos_ref[...].astype(F32) * inv_ref[...]
    lane = lax.broadcasted_iota(jnp.int32, ang.shape, 1)
    c = jnp.cos(ang)
    s = jnp.sin(ang)
    cos_ref[...] = jnp.where(lane < QK_NOPE, 1.0, jnp.where(lane < QK_DIM, c, 0.0))
    sin_ref[...] = jnp.where((lane >= QK_NOPE) & (lane < QK_NOPE + 16), -s,
                             jnp.where((lane >= QK_NOPE + 16) & (lane < QK_DIM), s, 0.0))


def _rope_tables(positions):
    inv_freq = 1.0 / (ROPE_THETA ** (jnp.arange(0, QK_ROPE, 2, dtype=F32) / QK_ROPE))
    inv = jnp.zeros((1, LANES), F32)
    inv = inv.at[0, QK_NOPE:QK_NOPE + 16].set(inv_freq)
    inv = inv.at[0, QK_NOPE + 16:QK_DIM].set(inv_freq)
    pos = positions.reshape(N_TOK, 1)
    tile = 2048
    return pl.pallas_call(
        _rope_kernel,
        out_shape=(jax.ShapeDtypeStruct((N_TOK, LANES), F32),
                   jax.ShapeDtypeStruct((N_TOK, LANES), F32)),
        grid=(N_TOK // tile,),
        in_specs=[pl.BlockSpec((tile, 1), lambda i: (i, 0)),
                  pl.BlockSpec((1, LANES), lambda i: (0, 0))],
        out_specs=(pl.BlockSpec((tile, LANES), lambda i: (i, 0)),
                   pl.BlockSpec((tile, LANES), lambda i: (i, 0))),
        compiler_params=_cparams(("arbitrary",)),
        name="rope_tables",
    )(pos, inv)


def _ada_kernel(c_ref, w_ref, b_ref, o_ref):
    c = c_ref[...]
    ca = c * jax.nn.sigmoid(c)
    o_ref[...] = jnp.dot(ca, w_ref[...], precision=lax.Precision.HIGHEST,
                         preferred_element_type=F32) + b_ref[...]


def _ada_mod(c, w_ada, b_ada):
    return pl.pallas_call(
        _ada_kernel,
        out_shape=jax.ShapeDtypeStruct((DEPTH, BATCH, 6 * D_MODEL), F32),
        grid=(DEPTH, 6),
        in_specs=[pl.BlockSpec((BATCH, D_MODEL), lambda l, j: (0, 0)),
                  pl.BlockSpec((None, D_MODEL, D_MODEL), lambda l, j: (l, 0, j)),
                  pl.BlockSpec((None, 1, D_MODEL), lambda l, j: (l, 0, j))],
        out_specs=pl.BlockSpec((None, BATCH, D_MODEL), lambda l, j: (l, 0, j)),
        compiler_params=_cparams(("arbitrary", "arbitrary")),
        name="ada_mod",
    )(c, w_ada, b_ada.reshape(DEPTH, 1, 6 * D_MODEL))


def _rope_rot(t, cos_t, sin_t):
    return t * cos_t + pltpu.roll(t, LANES - 16, 1) * sin_t


def _mix_in_kernel(x_ref, mod_ref, ng_ref, win_ref, lng_ref, lnb_ref, ws_ref, bs_ref,
                   qag_ref, wq_ref, kvg_ref, wkv_ref, wvt_ref, vone_ref, qng_ref, kng_ref, og_ref,
                   cos_ref, sin_ref,
                   gm_ref, q_ref, k_ref, vt_ref):
    x = x_ref[...]
    sh1 = mod_ref[0:1, :]
    sc1 = mod_ref[1:2, :]
    h = x * _rms_scale(x, D_MODEL) * ng_ref[...]
    h = h * (1.0 + sc1) + sh1
    proj = jnp.dot(h.astype(BF16), win_ref[...], preferred_element_type=F32)

    uv_pre = proj[:, :2 * GM_WIDTH]
    uv = 0.5 * uv_pre * (1.0 + lax.erf(uv_pre * (2.0 ** -0.5)))
    row = lax.broadcasted_iota(jnp.int32, (TT, TT), 0)
    col = lax.broadcasted_iota(jnp.int32, (TT, TT), 1)
    causal = (col <= row) & ((col // GM_BLOCK) == (row // GM_BLOCK))
    groups = range(GM_GROUPS)
    gsl = [slice(g * GM_CH, (g + 1) * GM_CH) for g in groups]
    vg = [uv[:, GM_WIDTH + g * GM_CH:GM_WIDTH + (g + 1) * GM_CH] for g in groups]
    mu = [jnp.mean(vg[g], axis=-1, keepdims=True) for g in groups]
    xc = [vg[g] - mu[g] for g in groups]
    var = [jnp.mean(xc[g] * xc[g], axis=-1, keepdims=True) for g in groups]
    vln = [(xc[g] * lax.rsqrt(var[g] + EPS) * lng_ref[:, gsl[g]] + lnb_ref[:, gsl[g]]).astype(BF16)
           for g in groups]
    ws = [jnp.where(causal, ws_ref[g], jnp.zeros((), BF16)) for g in groups]
    gate = [jnp.concatenate([jnp.dot(ws[g], vln[g][s * TT:(s + 1) * TT], preferred_element_type=F32)
                             for s in range(TM // TT)], axis=0) for g in groups]
    outs = [uv[:, gsl[g]] * (gate[g] + bs_ref[:, gsl[g]]) for g in groups]
    sq = [jnp.sum(outs[g] * outs[g], axis=-1, keepdims=True) for g in groups]
    rs = lax.rsqrt((sq[0] + sq[1] + sq[2] + sq[3]) * (1.0 / GM_WIDTH) + EPS)
    for g in groups:
        gm_ref[:, gsl[g]] = (outs[g] * rs * og_ref[:, gsl[g]]).astype(BF16)

    cos_t = cos_ref[...]
    sin_t = sin_ref[...]
    lane = lax.broadcasted_iota(jnp.int32, (TM, LANES), 1)
    ql = proj[:, O_Q:O_KV]
    qn = ql * _rms_scale(ql, Q_LORA) * qag_ref[...]
    qa = jnp.dot(qn.astype(BF16), wq_ref[...], preferred_element_type=F32)
    kl = proj[:, O_KV:O_PE]
    kn = kl * _rms_scale(kl, KV_LORA) * kvg_ref[...]
    kva = jnp.dot(kn.astype(BF16), wkv_ref[...], preferred_element_type=F32)
    pe = proj[:, O_PE:O_PE + LANES]
    real = lane < QK_DIM
    pe_ssq = jnp.sum(jnp.where(real, pe * pe, 0.0), axis=-1, keepdims=True)
    pe_rot = _rope_rot(pe * kng_ref[...], cos_t, sin_t)
    qscale = (QK_DIM ** -0.5) * 1.4426950408889634
    v_t = lax.dot_general(wvt_ref[...], kn.astype(BF16), (((1,), (1,)), ((), ())),
                          preferred_element_type=F32) + vone_ref[...]
    vt_ref[...] = v_t.reshape(MLA_HEADS, V_AUG, TM).astype(BF16)
    heads = range(MLA_HEADS)
    qh = [qa[:, hd * HEAD_PAD:(hd + 1) * HEAD_PAD] for hd in heads]
    kh = [kva[:, hd * HEAD_PAD:(hd + 1) * HEAD_PAD] for hd in heads]
    q_ssq = [jnp.sum(jnp.where(real, qh[hd] * qh[hd], 0.0), axis=-1, keepdims=True) for hd in heads]
    k_ssq = [jnp.sum(kh[hd] * kh[hd], axis=-1, keepdims=True) for hd in heads]
    qg = [qh[hd] * lax.rsqrt(q_ssq[hd] * (1.0 / QK_DIM) + EPS) * qng_ref[...] for hd in heads]
    q_sw = [pltpu.roll(qg[hd], LANES - 16, 1) for hd in heads]
    for hd in heads:
        q_ref[hd] = ((qg[hd] * cos_t + q_sw[hd] * sin_t) * qscale).astype(BF16)
    for hd in heads:
        krs = lax.rsqrt((k_ssq[hd] + pe_ssq) * (1.0 / QK_DIM) + EPS)
        k_ref[hd] = ((kh[hd] * kng_ref[...] + pe_rot) * krs).astype(BF16)


def _mix_in(x, mod_l, p, cos_t, sin_t):
    tiles_per_seq = SEQ // TM
    full = lambda shape: pl.BlockSpec(shape, lambda i: (0,) * len(shape))
    head_spec = pl.BlockSpec((None, MLA_HEADS, TM, HEAD_PAD),
                             lambda i: (i // tiles_per_seq, 0, i % tiles_per_seq, 0))
    head_shape = jax.ShapeDtypeStruct((BATCH, MLA_HEADS, SEQ, HEAD_PAD), BF16)
    return pl.pallas_call(
        _mix_in_kernel,
        out_shape=(jax.ShapeDtypeStruct((N_TOK, GM_WIDTH), BF16), head_shape, head_shape,
                   jax.ShapeDtypeStruct((BATCH, MLA_HEADS, V_AUG, SEQ), BF16)),
        grid=(N_TOK // TM,),
        in_specs=[pl.BlockSpec((TM, D_MODEL), lambda i: (i, 0)),
                  pl.BlockSpec((None, 6, D_MODEL), lambda i: (i // tiles_per_seq, 0, 0)),
                  full((1, D_MODEL)),
                  full((D_MODEL, IN_COLS_PAD)),
                  full((1, GM_WIDTH)), full((1, GM_WIDTH)),
                  full((GM_GROUPS, TT, TT)), full((TM, GM_WIDTH)),
                  full((1, Q_LORA)), full((Q_LORA, MLA_HEADS * HEAD_PAD)),
                  full((1, KV_LORA)), full((KV_LORA, MLA_HEADS * HEAD_PAD)),
                  full((MLA_HEADS * V_AUG, KV_LORA)), full((MLA_HEADS * V_AUG, 1)),
                  full((1, LANES)), full((1, LANES)), full((1, GM_WIDTH)),
                  pl.BlockSpec((TM, LANES), lambda i: (i, 0)),
                  pl.BlockSpec((TM, LANES), lambda i: (i, 0))],
        out_specs=(pl.BlockSpec((TM, GM_WIDTH), lambda i: (i, 0)), head_spec, head_spec,
                   pl.BlockSpec((None, MLA_HEADS, V_AUG, TM),
                                lambda i: (i // tiles_per_seq, 0, 0, i % tiles_per_seq))),
        compiler_params=_cparams(("arbitrary",)),
        name="mix_in",
    )(x, mod_l, p["norm_mix_g"], p["w_in"], p["gm_ln_g"], p["gm_ln_b"], p["gm_ws"], p["gm_bs"],
      p["q_a_g"], p["w_q_up"], p["kv_a_g"], p["w_k_up"], p["w_v_t"], p["v_ones"], p["q_norm_g"],
      p["k_norm_g"],
      p["out_norm_gm"], cos_t, sin_t)


def _attn_kernel(q_ref, k_ref, vt_ref, o_ref):
    qi = pl.program_id(2)
    krow = lax.broadcasted_iota(jnp.int32, (TK, TQ), 0)
    qcol = lax.broadcasted_iota(jnp.int32, (TK, TQ), 1)
    nt = (((1,), (1,)), ((), ()))
    per_q = TQ // TK

    def tile(j, carry, diag):
        start = pl.multiple_of(j * TK, TK)
        heads = range(ATT_HEADS)
        s = [lax.dot_general(k_ref[hh, pl.ds(start, TK), :], q_ref[hh], nt,
                             preferred_element_type=F32) for hh in heads]
        if diag is not None:
            allowed = ((krow + diag * TK) // CHUNK) <= (qcol // CHUNK)
            s = [jnp.where(allowed, sh, -jnp.inf) for sh in s]
        m_new = [jnp.maximum(carry[hh][0], jnp.max(s[hh], axis=0, keepdims=True)) for hh in heads]
        p = [jnp.exp2(s[hh] - m_new[hh]) for hh in heads]
        alpha = [jnp.exp2(carry[hh][0] - m_new[hh]) for hh in heads]
        pv = [jnp.dot(vt_ref[hh, :, pl.ds(start, TK)], p[hh].astype(BF16),
                      preferred_element_type=F32) for hh in heads]
        return tuple((m_new[hh], alpha[hh] * carry[hh][1] + pv[hh]) for hh in heads)

    init = tuple((jnp.full((1, TQ), -jnp.inf, F32), jnp.zeros((V_AUG, TQ), F32))
                 for _ in range(ATT_HEADS))
    carry = lax.fori_loop(0, qi * per_q, functools.partial(tile, diag=None), init)
    for t in range(per_q):
        carry = tile(qi * per_q + t, carry, t)
    out_t = jnp.concatenate([acc[:V_DIM] / acc[V_DIM:V_DIM + 1] for (_, acc) in carry], axis=0)
    o_ref[...] = out_t.T


def _attention(q, k, vt):
    groups = MLA_HEADS // ATT_HEADS
    qspec = pl.BlockSpec((None, ATT_HEADS, TQ, HEAD_PAD), lambda b, hp, i: (b, hp, i, 0))
    kspec = pl.BlockSpec((None, ATT_HEADS, SEQ, HEAD_PAD), lambda b, hp, i: (b, hp, 0, 0))
    vspec = pl.BlockSpec((None, ATT_HEADS, V_AUG, SEQ), lambda b, hp, i: (b, hp, 0, 0))
    return pl.pallas_call(
        _attn_kernel,
        out_shape=jax.ShapeDtypeStruct((BATCH, SEQ, MLA_WIDTH), F32),
        grid=(BATCH, groups, SEQ // TQ),
        in_specs=[qspec, kspec, vspec],
        out_specs=pl.BlockSpec((None, TQ, ATT_HEADS * V_DIM), lambda b, hp, i: (b, i, hp)),
        compiler_params=_cparams(("arbitrary", "arbitrary", "arbitrary")),
        name="mla_attention",
    )(q, k, vt)


def _route(logits_t, bias_col):
    t = logits_t.shape[1]
    scores = jax.nn.sigmoid(logits_t)
    biased = scores + bias_col
    neg = -jnp.inf
    b3 = biased.reshape(N_GROUPS, GROUP_SIZE, t)
    m1 = jnp.max(b3, axis=1, keepdims=True)
    n_max = jnp.sum((b3 == m1).astype(F32), axis=1, keepdims=True)
    m2 = jnp.max(jnp.where(b3 < m1, b3, neg), axis=1, keepdims=True)
    gs = (m1 + jnp.where(n_max >= 2.0, m1, m2)).reshape(N_GROUPS, t)
    gidx = lax.broadcasted_iota(jnp.int32, (N_GROUPS, t), 0)
    grank = jnp.zeros((N_GROUPS, t), F32)
    for g in range(N_GROUPS):
        other = gs[g:g + 1, :]
        ahead = (other > gs) | ((other == gs) & (gidx > g))
        grank = grank + ahead.astype(F32)
    gsel = grank < float(TOPK_GROUPS)
    emask = jnp.broadcast_to(gsel.reshape(N_GROUPS, 1, t), (N_GROUPS, GROUP_SIZE, t)).reshape(N_EXPERTS, t)
    ms = jnp.where(emask, biased, neg)
    eidx = lax.broadcasted_iota(jnp.int32, (N_EXPERTS, t), 0).astype(F32)
    picked = jnp.zeros((N_EXPERTS, t), F32)
    for _ in range(TOP_K):
        top = jnp.max(ms, axis=0, keepdims=True)
        first = jnp.min(jnp.where(ms == top, eidx, float(N_EXPERTS)), axis=0, keepdims=True)
        hit = eidx == first
        picked = jnp.where(hit, 1.0, picked)
        ms = jnp.where(hit, neg, ms)
    sel = picked > 0.5
    w = jnp.where(sel, scores, 0.0)
    denom = jnp.sum(w, axis=0, keepdims=True)
    return (w / denom) * ROUTED_SCALE, sel


def _mix_out_kernel(x_ref, gm_ref, mla_ref, mod_ref, og_ref, wout_ref, fg_ref, wr_ref, rb_ref,
                    x1_ref, h2_ref, rkt_ref, rktok_ref, cnt_ref):
    mla = mla_ref[...]
    mla_n = mla * _rms_scale(mla, MLA_WIDTH) * og_ref[...]
    mixed = jnp.concatenate([gm_ref[...], mla_n.astype(BF16)], axis=-1)
    y = jnp.dot(mixed, wout_ref[...], preferred_element_type=F32)
    g1 = mod_ref[2:3, :]
    sh2 = mod_ref[3:4, :]
    sc2 = mod_ref[4:5, :]
    x1 = x_ref[...] + g1 * y
    x1_ref[...] = x1
    h2 = x1 * _rms_scale(x1, D_MODEL) * fg_ref[...]
    h2 = h2 * (1.0 + sc2) + sh2
    nt = (((1,), (1,)), ((), ()))
    h_hi = h2.astype(BF16)
    h_lo = (h2 - h_hi.astype(F32)).astype(BF16)
    wr = wr_ref[...]
    w_hi = wr.astype(BF16)
    w_lo = (wr - w_hi.astype(F32)).astype(BF16)
    logits_t = (lax.dot_general(w_hi, h_hi, nt, preferred_element_type=F32)
                + lax.dot_general(w_hi, h_lo, nt, preferred_element_type=F32)
                + lax.dot_general(w_lo, h_hi, nt, preferred_element_type=F32))
    comb, sel = _route(logits_t, rb_ref[...])
    before = (lax.broadcasted_iota(jnp.int32, (TT, TT), 0)
              < lax.broadcasted_iota(jnp.int32, (TT, TT), 1))
    self = jnp.where(sel, 1.0, 0.0)
    before_b = jnp.where(before, 1.0, 0.0).astype(BF16)
    rank = jnp.concatenate(
        [jnp.dot(self[:, s * TT:(s + 1) * TT].astype(BF16), before_b, preferred_element_type=F32)
         for s in range(TM // TT)], axis=1)
    rk = jnp.where(sel, rank, -1.0)
    rkt_ref[...] = rk.astype(BF16)
    rktok_ref[...] = rk.T.astype(BF16)
    for s in range(TM // TT):
        cnt_ref[s] = jnp.broadcast_to(
            jnp.sum(self[:, s * TT:(s + 1) * TT], axis=1, keepdims=True), (N_EXPERTS, LANES))
    comb_tok = comb.T
    hi = comb_tok.astype(BF16)
    lo = (comb_tok - hi.astype(F32)).astype(BF16)
    h2_ref[:, :D_MODEL] = h2.astype(BF16)
    h2_ref[:, D_MODEL:] = jnp.concatenate([hi, lo], axis=-1)


def _mix_out(x, gm, mla, mod_l, p):
    tiles_per_seq = SEQ // TM
    full = lambda shape: pl.BlockSpec(shape, lambda i: (0,) * len(shape))
    tok = lambda width: pl.BlockSpec((TM, width), lambda i: (i, 0))
    return pl.pallas_call(
        _mix_out_kernel,
        out_shape=(jax.ShapeDtypeStruct((N_TOK, D_MODEL), F32),
                   jax.ShapeDtypeStruct((N_TOK, H_AUG), BF16),
                   jax.ShapeDtypeStruct((N_EXPERTS, N_TOK), BF16),
                   jax.ShapeDtypeStruct((N_TOK, N_EXPERTS), BF16),
                   jax.ShapeDtypeStruct((N_TILES, N_EXPERTS, LANES), F32)),
        grid=(N_TOK // TM,),
        in_specs=[tok(D_MODEL), tok(GM_WIDTH), tok(MLA_WIDTH),
                  pl.BlockSpec((None, 6, D_MODEL), lambda i: (i // tiles_per_seq, 0, 0)),
                  full((1, MLA_WIDTH)), full((D_MODEL, D_MODEL)), full((1, D_MODEL)),
                  full((N_EXPERTS, D_MODEL)), full((N_EXPERTS, 1))],
        out_specs=(tok(D_MODEL), tok(H_AUG), pl.BlockSpec((N_EXPERTS, TM), lambda i: (0, i)),
                   tok(N_EXPERTS), pl.BlockSpec((TM // TT, N_EXPERTS, LANES), lambda i: (i, 0, 0))),
        compiler_params=_cparams(("arbitrary",)),
        name="mix_out_route",
    )(x, gm, mla, mod_l, p["out_norm_mla"], p["w_out"], p["norm_ffn_g"], p["w_router_t"],
      p["router_bias"])


def _moe_plan(cnt):
    i32 = jnp.int32
    cnt = cnt.astype(i32)
    padlen = (cnt + SEG_ALIGN - 1) // SEG_ALIGN * SEG_ALIGN
    lend = jnp.cumsum(padlen, axis=1)
    lstart = lend - padlen
    ltot = lend[:, -1]
    region = jnp.sum(padlen, axis=0)
    region_pad = (region + TM_FFN - 1) // TM_FFN * TM_FFN
    base = jnp.cumsum(region_pad) - region_pad
    pos = base[None, :] + jnp.cumsum(padlen, axis=0) - padlen
    gidx = jnp.arange(MAX_GROUPS, dtype=i32)
    grow = gidx * SEG_ALIGN
    in_seg = (grow[None, :, None] >= lstart[:, None, :]) & (grow[None, :, None] < lend[:, None, :])
    shift = jnp.sum(jnp.where(in_seg, (pos - lstart)[:, None, :], 0), axis=-1)
    n_groups = ltot // SEG_ALIGN
    valid_g = gidx[None, :] < n_groups[:, None]
    real_dst = (shift + grow[None, :]) // SEG_ALIGN
    trash_base = SORTED_ROWS // SEG_ALIGN
    trash = trash_base + (jnp.arange(N_TILES, dtype=i32)[:, None] % 2) * MAX_GROUPS + gidx[None, :]
    n_chunks = (n_groups + GROUPS_PER_CHUNK - 1) // GROUPS_PER_CHUNK
    n_move = jnp.maximum(n_chunks, STATIC_CHUNKS)
    group_dst = jnp.where(valid_g, real_dst, trash)
    group_src = jnp.where(valid_g, real_dst, 0)
    lead = (trash_base + MAX_GROUPS + gidx)[None, :]
    static_c = jnp.full((1,), STATIC_CHUNKS, i32)
    tile_end = jnp.cumsum(region_pad) // TM_FFN
    n_used = tile_end[-1]
    tj = jnp.arange(FFN_TILES, dtype=i32)
    in_exp = (tj[:, None] * TM_FFN >= base[None, :]) & (tj[:, None] * TM_FFN < (base + region_pad)[None, :])
    t_exp = jnp.sum(jnp.where(in_exp, jnp.arange(N_EXPERTS, dtype=i32)[None, :], 0), axis=-1)
    t_valid = jnp.sum(jnp.where(in_exp, jnp.clip((base + region)[None, :] - tj[:, None] * TM_FFN, 0, TM_FFN), 0),
                      axis=-1)
    has_rows = region_pad > 0
    run_of_exp = jnp.cumsum(has_rows.astype(i32)) - 1
    eidx = jnp.arange(N_EXPERTS, dtype=i32)
    t_run = jnp.sum(jnp.where(in_exp, run_of_exp[None, :], 0), axis=-1)
    run_exp = jnp.sum(jnp.where(has_rows[None, :] & (run_of_exp[None, :] == eidx[:, None]), eidx[None, :], 0),
                      axis=-1)
    lstart_f = lstart.astype(F32)
    lend_f = lend.astype(F32)
    return {
        "t_run": t_run.astype(i32),
        "run_exp": run_exp.astype(i32),
        "n_runs": jnp.sum(has_rows.astype(i32)).reshape(1),
        "move_out": jnp.concatenate([static_c, n_move]).astype(i32),
        "move_in": jnp.concatenate([n_move, static_c]).astype(i32),
        "group_dst": jnp.concatenate([lead, group_dst], axis=0).reshape(-1).astype(i32),
        "group_src": jnp.concatenate([group_src, jnp.zeros((1, MAX_GROUPS), i32)], axis=0
                                     ).reshape(-1).astype(i32),
        "seg_row": jnp.stack([lstart_f, lend_f], axis=1),
        "seg_col": jnp.stack([lstart_f, lend_f], axis=2),
        "n_used": n_used.reshape(1).astype(i32),
        "t_exp": t_exp.astype(i32),
        "t_valid": t_valid.astype(i32),
    }


def _dispatch_kernel(nm_ref, gd_ref, h_ref, rkt_ref, srow_ref, scol_ref, xs_ref, loc_ref, sem):
    i = pl.program_id(0)
    slot = i % 2
    pslot = 1 - slot

    def wait_chunks(n, slot_):
        def body(c, carry):
            pltpu.make_async_copy(loc_ref.at[slot_, pl.ds(0, GROUPS_PER_CHUNK)],
                                  xs_ref.at[pl.ds(0, GROUPS_PER_CHUNK)], sem.at[slot_]).wait()
            return carry
        lax.fori_loop(0, n, body, 0)

    def issue_chunk(entry, slot_, c):
        for k in range(GROUPS_PER_CHUNK):
            g = c * GROUPS_PER_CHUNK + k
            pltpu.make_async_copy(loc_ref.at[slot_, g], xs_ref.at[gd_ref[entry * MAX_GROUPS + g]],
                                  sem.at[slot_]).start()

    @pl.when(i == 0)
    def _():
        loc_ref[1] = jnp.zeros(loc_ref.shape[1:], BF16)

    ls_row = srow_ref[0:1, :]
    le_row = srow_ref[1:2, :]
    ls_rep = jnp.broadcast_to(scol_ref[:, 0:1], (N_EXPERTS, LANES)).astype(BF16)
    rk_ls = jnp.concatenate([rkt_ref[...], ls_rep], axis=1)
    h = h_ref[...]
    n_c = nm_ref[i + 1]
    rio_e0 = lax.broadcasted_iota(jnp.int32, (CHUNK_ROWS, N_EXPERTS), 0).astype(F32)
    rio_t0 = lax.broadcasted_iota(jnp.int32, (CHUNK_ROWS, TT), 0).astype(F32)

    def select(r0f):
        rio_e = rio_e0 + r0f
        g = jnp.where((rio_e >= ls_row) & (rio_e < le_row), 1.0, 0.0).astype(BF16)
        both = jnp.dot(g, rk_ls, preferred_element_type=F32)
        want = (rio_t0 + r0f) - jnp.concatenate([both[:, TT:]] * (TT // LANES), axis=1)
        return jnp.where(both[:, :TT] == want, 1.0, 0.0).astype(BF16)

    def emit(g0, psel):
        loc_ref[slot, pl.ds(g0, GROUPS_PER_CHUNK)] = jnp.dot(
            psel, h, preferred_element_type=F32).astype(BF16).reshape(GROUPS_PER_CHUNK, SEG_ALIGN, H_AUG)

    psels = []
    for c in range(STATIC_CHUNKS):
        psels.append(select(float(c * CHUNK_ROWS)))
        issue_chunk(i, pslot, c)

    def issue_more(c, carry):
        issue_chunk(i, pslot, c)
        return carry

    lax.fori_loop(STATIC_CHUNKS, nm_ref[i], issue_more, 0)

    @pl.when(i >= 1)
    def _():
        wait_chunks(nm_ref[i - 1], slot)

    for c in range(STATIC_CHUNKS):
        emit(c * GROUPS_PER_CHUNK, psels[c])

    def chunk(c, carry):
        emit(pl.multiple_of(c * GROUPS_PER_CHUNK, GROUPS_PER_CHUNK), select((c * CHUNK_ROWS).astype(F32)))
        return carry

    lax.fori_loop(STATIC_CHUNKS, n_c, chunk, 0)

    @pl.when(i == N_TILES - 1)
    def _():
        def issue_last(c, carry):
            issue_chunk(i + 1, slot, c)
            return carry

        lax.fori_loop(0, n_c, issue_last, 0)
        wait_chunks(nm_ref[i], pslot)
        wait_chunks(n_c, slot)


def _dispatch(plan, h2aug, rkt):
    grid_spec = pltpu.PrefetchScalarGridSpec(
        num_scalar_prefetch=2,
        grid=(N_TILES,),
        in_specs=[pl.BlockSpec((TT, H_AUG), lambda i, *_: (i, 0)),
                  pl.BlockSpec((N_EXPERTS, TT), lambda i, *_: (0, i)),
                  pl.BlockSpec((None, 2, N_EXPERTS), lambda i, *_: (i, 0, 0)),
                  pl.BlockSpec((None, N_EXPERTS, 2), lambda i, *_: (i, 0, 0))],
        out_specs=pl.BlockSpec(memory_space=pl.ANY),
        scratch_shapes=[pltpu.VMEM((2, MAX_GROUPS, SEG_ALIGN, H_AUG), BF16),
                        pltpu.SemaphoreType.DMA((2,))],
    )
    return pl.pallas_call(
        _dispatch_kernel,
        out_shape=jax.ShapeDtypeStruct(((SORTED_ROWS + TRASH_ROWS) // SEG_ALIGN, SEG_ALIGN, H_AUG), BF16),
        grid_spec=grid_spec,
        compiler_params=_cparams(("arbitrary",)),
        name="moe_dispatch",
    )(plan["move_out"], plan["group_dst"], h2aug, rkt, plan["seg_row"], plan["seg_col"])


def _ffn_kernel(nu_ref, te_ref, tv_ref, tr_ref, re_ref, nr_ref,
                xs_ref, wg_hbm, wu_hbm, wd_hbm, y_hbm,
                xbuf_ref, ybuf_ref, wg_raw, wu_raw, wd_raw, wgu_ref, wd_ref, xsem, ysem, wsem,
                *, layer):
    n_used = nu_ref[0]
    n_runs = nr_ref[0]

    tile_groups = TM_FFN // SEG_ALIGN

    def groups_of(t):
        return pl.ds(pl.multiple_of(t * tile_groups, tile_groups), tile_groups)

    def x_copy(t):
        slot = t % FFN_RING
        return pltpu.make_async_copy(xs_ref.at[groups_of(t)], xbuf_ref.at[slot], xsem.at[slot])

    def y_copy(t):
        slot = t % 2
        return pltpu.make_async_copy(ybuf_ref.at[slot], y_hbm.at[groups_of(t)], ysem.at[slot])

    def w_copies(r):
        slot = r % 2
        e = re_ref[r]
        return (pltpu.make_async_copy(wg_hbm.at[layer, e], wg_raw.at[slot], wsem.at[slot]),
                pltpu.make_async_copy(wu_hbm.at[layer, e], wu_raw.at[slot], wsem.at[slot]),
                pltpu.make_async_copy(wd_hbm.at[layer, e], wd_raw.at[slot], wsem.at[slot]))

    x_copy(0).start()

    @pl.when(n_used > 1)
    def _():
        x_copy(1).start()

    for cp in w_copies(0):
        cp.start()

    def tile(j, carry):
        e = te_ref[j]
        r = tr_ref[j]

        @pl.when(j + 2 < n_used)
        def _():
            x_copy(j + 2).start()

        @pl.when((j == 0) | (r != tr_ref[jnp.maximum(j - 1, 0)]))
        def _():
            for cp in w_copies(r):
                cp.wait()
            slot = r % 2
            wgu_ref[:, :EXPERT_FF] = wg_raw[slot].astype(BF16)
            wgu_ref[:, EXPERT_FF:] = wu_raw[slot].astype(BF16)
            wd_ref[...] = wd_raw[slot].astype(BF16)

            @pl.when(r + 1 < n_runs)
            def _():
                for cp in w_copies(r + 1):
                    cp.start()

        x_copy(j).wait()

        @pl.when(j >= 2)
        def _():
            y_copy(j - 2).wait()

        xt = xbuf_ref[j % FFN_RING].reshape(TM_FFN, H_AUG)
        rows = lax.broadcasted_iota(jnp.int32, (TM_FFN, 1), 0)
        valid = rows < tv_ref[j]
        x = jnp.where(valid, xt[:, :D_MODEL], jnp.zeros((), BF16))
        lane = lax.broadcasted_iota(jnp.int32, (TM_FFN, LANES), 1)
        mine = ((lane == e) | (lane == e + N_EXPERTS)) & valid
        w = jnp.sum(jnp.where(mine, xt[:, D_MODEL:].astype(F32), 0.0), axis=-1, keepdims=True)
        hgu = jnp.dot(x, wgu_ref[...], preferred_element_type=F32)
        g = hgu[:, :EXPERT_FF]
        a = (g * jax.nn.sigmoid(g)) * hgu[:, EXPERT_FF:] * w
        ybuf_ref[j % 2] = jnp.dot(a.astype(BF16), wd_ref[...], preferred_element_type=F32).astype(
            BF16).reshape(tile_groups, SEG_ALIGN, D_MODEL)
        y_copy(j).start()
        return carry

    lax.fori_loop(0, n_used, tile, 0)

    @pl.when(n_used >= 2)
    def _():
        y_copy(n_used - 2).wait()

    y_copy(n_used - 1).wait()


def _expert_ffn(plan, xs, p):
    any_spec = pl.BlockSpec(memory_space=pl.ANY)
    grid_spec = pltpu.PrefetchScalarGridSpec(
        num_scalar_prefetch=6,
        grid=(1,),
        in_specs=[any_spec, any_spec, any_spec, any_spec],
        out_specs=any_spec,
        scratch_shapes=[pltpu.VMEM((FFN_RING, TM_FFN // SEG_ALIGN, SEG_ALIGN, H_AUG), BF16),
                        pltpu.VMEM((2, TM_FFN // SEG_ALIGN, SEG_ALIGN, D_MODEL), BF16),
                        pltpu.VMEM((2, D_MODEL, EXPERT_FF), F32),
                        pltpu.VMEM((2, D_MODEL, EXPERT_FF), F32),
                        pltpu.VMEM((2, EXPERT_FF, D_MODEL), F32),
                        pltpu.VMEM((D_MODEL, 2 * EXPERT_FF), BF16),
                        pltpu.VMEM((EXPERT_FF, D_MODEL), BF16),
                        pltpu.SemaphoreType.DMA((FFN_RING,)),
                        pltpu.SemaphoreType.DMA((2,)),
                        pltpu.SemaphoreType.DMA((2,))],
    )
    return pl.pallas_call(
        functools.partial(_ffn_kernel, layer=p["layer"]),
        out_shape=jax.ShapeDtypeStruct((SORTED_ROWS // SEG_ALIGN, SEG_ALIGN, D_MODEL), BF16),
        grid_spec=grid_spec,
        compiler_params=_cparams(("arbitrary",)),
        name="moe_expert_ffn",
    )(plan["n_used"], plan["t_exp"], plan["t_valid"], plan["t_run"], plan["run_exp"], plan["n_runs"],
      xs, p["w_exp_gate"], p["w_exp_up"], p["w_exp_down"])


def _combine_kernel(nc_ref, gs_ref, y_ref, rktok_ref, srow_ref, scol_ref, h_ref, x1_ref, mod_ref,
                    sgu_ref, sd_ref, o_ref, yl_ref, acc_ref, sem):
    i = pl.program_id(0)
    slot = i % 2
    pslot = 1 - slot

    def gather_chunk(tile, slot_, c):
        for k in range(GROUPS_PER_CHUNK):
            g = c * GROUPS_PER_CHUNK + k
            pltpu.make_async_copy(y_ref.at[gs_ref[tile * MAX_GROUPS + g]], yl_ref.at[slot_, g],
                                  sem.at[slot_]).start()

    def wait_chunks(n, slot_):
        def body(c, carry):
            pltpu.make_async_copy(y_ref.at[pl.ds(0, GROUPS_PER_CHUNK)],
                                  yl_ref.at[slot_, pl.ds(0, GROUPS_PER_CHUNK)], sem.at[slot_]).wait()
            return carry
        lax.fori_loop(0, n, body, 0)

    @pl.when(i == 0)
    def _():
        yl_ref[...] = jnp.zeros(yl_ref.shape, BF16)

        def first(c, carry):
            gather_chunk(0, 0, c)
            return carry

        lax.fori_loop(0, nc_ref[0], first, 0)

    h = h_ref[...]
    hs = jnp.dot(h, sgu_ref[...], preferred_element_type=F32)
    gs = hs[:, :EXPERT_FF]
    a = (gs * jax.nn.sigmoid(gs)) * hs[:, EXPERT_FF:]
    shared = jnp.dot(a.astype(BF16), sd_ref[...], preferred_element_type=F32)

    n_c = nc_ref[i]
    ls_col = scol_ref[:, 0:1]
    le_col = scol_ref[:, 1:2]
    ls_row8 = jnp.broadcast_to(srow_ref[0:1, :], (8, N_EXPERTS)).astype(BF16)
    rk_ls = jnp.concatenate([rktok_ref[...], ls_row8], axis=0)
    rio_e0 = lax.broadcasted_iota(jnp.int32, (N_EXPERTS, CHUNK_ROWS), 1).astype(F32)
    rio_r0 = lax.broadcasted_iota(jnp.int32, (1, CHUNK_ROWS), 1).astype(F32)

    def select_t(r0f):
        rio_e = rio_e0 + r0f
        gt = jnp.where((rio_e >= ls_col) & (rio_e < le_col), 1.0, 0.0).astype(BF16)
        both = jnp.dot(rk_ls, gt, preferred_element_type=F32)
        want = (rio_r0 + r0f) - both[TT:TT + 1, :]
        return jnp.where(both[:TT, :] == want, 1.0, 0.0).astype(BF16)

    static_rows = STATIC_CHUNKS * CHUNK_ROWS
    pts = []
    for c in range(STATIC_CHUNKS):
        pts.append(select_t(float(c * CHUNK_ROWS)))
        gather_chunk(i + 1, pslot, c)
    pt_all = jnp.concatenate(pts, axis=1)

    def gather_more(c, carry):
        gather_chunk(i + 1, pslot, c)
        return carry

    lax.fori_loop(STATIC_CHUNKS, nc_ref[i + 1], gather_more, 0)
    wait_chunks(n_c, slot)
    y_static = yl_ref[slot, pl.ds(0, STATIC_CHUNKS * GROUPS_PER_CHUNK)].reshape(static_rows, D_MODEL)
    acc_ref[...] = shared + jnp.dot(pt_all, y_static, preferred_element_type=F32)

    def chunk(c, carry):
        g0 = pl.multiple_of(c * GROUPS_PER_CHUNK, GROUPS_PER_CHUNK)
        y_c = yl_ref[slot, pl.ds(g0, GROUPS_PER_CHUNK)].reshape(CHUNK_ROWS, D_MODEL)
        acc_ref[...] += jnp.dot(select_t((c * CHUNK_ROWS).astype(F32)), y_c, preferred_element_type=F32)
        return carry

    lax.fori_loop(STATIC_CHUNKS, n_c, chunk, 0)
    o_ref[...] = x1_ref[...] + mod_ref[5:6, :] * acc_ref[...]

    @pl.when(i == N_TILES - 1)
    def _():
        wait_chunks(nc_ref[i + 1], pslot)


def _combine(plan, y, rktok, h2aug, x1, mod_l, p):
    tiles_per_seq = SEQ // TT
    grid_spec = pltpu.PrefetchScalarGridSpec(
        num_scalar_prefetch=2,
        grid=(N_TILES,),
        in_specs=[pl.BlockSpec(memory_space=pl.ANY),
                  pl.BlockSpec((TT, N_EXPERTS), lambda i, *_: (i, 0)),
                  pl.BlockSpec((None, 2, N_EXPERTS), lambda i, *_: (i, 0, 0)),
                  pl.BlockSpec((None, N_EXPERTS, 2), lambda i, *_: (i, 0, 0)),
                  pl.BlockSpec((TT, D_MODEL), lambda i, *_: (i, 0)),
                  pl.BlockSpec((TT, D_MODEL), lambda i, *_: (i, 0)),
                  pl.BlockSpec((None, 6, D_MODEL), lambda i, *_: (i // tiles_per_seq, 0, 0)),
                  pl.BlockSpec((D_MODEL, 2 * EXPERT_FF), lambda i, *_: (0, 0)),
                  pl.BlockSpec((EXPERT_FF, D_MODEL), lambda i, *_: (0, 0))],
        out_specs=pl.BlockSpec((TT, D_MODEL), lambda i, *_: (i, 0)),
        scratch_shapes=[pltpu.VMEM((2, MAX_GROUPS, SEG_ALIGN, D_MODEL), BF16),
                        pltpu.VMEM((TT, D_MODEL), F32),
                        pltpu.SemaphoreType.DMA((2,))],
    )
    return pl.pallas_call(
        _combine_kernel,
        out_shape=jax.ShapeDtypeStruct((N_TOK, D_MODEL), F32),
        grid_spec=grid_spec,
        compiler_params=_cparams(("arbitrary",)),
        name="moe_combine",
    )(plan["move_in"], plan["group_src"], y, rktok, plan["seg_row"], plan["seg_col"], h2aug, x1,
      mod_l, p["w_sh_gu"], p["w_sh_down"])


def _moe(h2aug, rkt, rktok, cnt, x1, mod_l, p):
    plan = _moe_plan(cnt[:, :, 0])
    xs = _dispatch(plan, h2aug, rkt)
    y = _expert_ffn(plan, xs, p)
    return _combine(plan, y, rktok, h2aug, x1, mod_l, p)


def _pad_heads(w, per_head):
    k = w.shape[0]
    w = w.reshape(k, MLA_HEADS, per_head)
    w = jnp.pad(w, ((0, 0), (0, 0), (0, HEAD_PAD - per_head)))
    return w.reshape(k, MLA_HEADS * HEAD_PAD)


def _layer_params(l, a):
    w_in = a["w_in"][l]
    half = QK_ROPE // 2
    pe_cols = jnp.zeros((D_MODEL, LANES), F32).at[:, QK_NOPE:QK_DIM].set(w_in[:, O_PE:])
    pe_cols = pe_cols.at[:, QK_DIM:QK_DIM + half].set(w_in[:, O_PE:O_PE + half])
    w_in_p = jnp.concatenate([w_in[:, :O_PE], pe_cols], axis=1).astype(BF16)
    w_q = a["w_q_up"][l].reshape(Q_LORA, MLA_HEADS, QK_DIM)
    w_q = jnp.concatenate([w_q, w_q[:, :, QK_NOPE:QK_NOPE + half],
                           jnp.zeros((Q_LORA, MLA_HEADS, HEAD_PAD - QK_DIM - half), F32)], axis=-1)
    w_kv = a["w_kv_up"][l].reshape(KV_LORA, MLA_HEADS, QK_NOPE + V_DIM)
    w_k = _pad_heads(w_kv[:, :, :QK_NOPE].reshape(KV_LORA, MLA_HEADS * QK_NOPE), QK_NOPE)
    w_v = w_kv[:, :, QK_NOPE:].reshape(KV_LORA, MLA_HEADS * V_DIM)
    blocks = TT // GM_BLOCK
    eye = jnp.eye(blocks, dtype=F32)
    ws_big = jnp.einsum("ab,gts->gatbs", eye, a["gm_ws"][l]).reshape(GM_GROUPS, TT, TT)
    bs_tile = jnp.tile(jnp.repeat(a["gm_bs"][l].T, GM_CH, axis=1), (TM // GM_BLOCK, 1))
    pad96 = lambda g: jnp.concatenate(
        [g, g[QK_NOPE:QK_NOPE + half], jnp.zeros((LANES - QK_DIM - half,), F32)]).reshape(1, LANES)
    return {
        "norm_mix_g": a["norm_mix_g"][l].reshape(1, D_MODEL),
        "norm_ffn_g": a["norm_ffn_g"][l].reshape(1, D_MODEL),
        "w_in": w_in_p,
        "gm_ln_g": a["gm_ln_g"][l].reshape(1, GM_WIDTH),
        "gm_ln_b": a["gm_ln_b"][l].reshape(1, GM_WIDTH),
        "gm_ws": ws_big.astype(BF16),
        "gm_bs": bs_tile,
        "q_a_g": a["q_a_g"][l].reshape(1, Q_LORA),
        "w_q_up": w_q.reshape(Q_LORA, MLA_HEADS * HEAD_PAD).astype(BF16),
        "kv_a_g": a["kv_a_g"][l].reshape(1, KV_LORA),
        "w_k_up": w_k.astype(BF16),
        "w_v_t": jnp.pad(w_v.T.reshape(MLA_HEADS, V_DIM, KV_LORA), ((0, 0), (0, V_AUG - V_DIM), (0, 0))
                         ).reshape(MLA_HEADS * V_AUG, KV_LORA).astype(BF16),
        "v_ones": jnp.tile((jnp.arange(V_AUG) == V_DIM).astype(F32), MLA_HEADS).reshape(-1, 1),
        "q_norm_g": pad96(a["q_norm_g"][l]),
        "k_norm_g": pad96(a["k_norm_g"][l]),
        "out_norm_gm": a["out_norm_g"][l, :GM_WIDTH].reshape(1, GM_WIDTH),
        "out_norm_mla": a["out_norm_g"][l, GM_WIDTH:].reshape(1, MLA_WIDTH),
        "w_out": a["w_out"][l].astype(BF16),
        "w_router_t": a["w_router"][l].T,
        "router_bias": a["router_bias"][l].reshape(N_EXPERTS, 1),
        "layer": l,
        "w_exp_gate": a["w_exp_gate"],
        "w_exp_up": a["w_exp_up"],
        "w_exp_down": a["w_exp_down"],
        "w_sh_gu": jnp.concatenate([a["w_sh_gate"][l], a["w_sh_up"][l]], axis=-1).astype(BF16),
        "w_sh_down": a["w_sh_down"][l].astype(BF16),
    }


def kernel(x, c, positions, norm_mix_g, norm_ffn_g, w_ada, b_ada, w_in, gm_ln_g, gm_ln_b, gm_ws, gm_bs, q_a_g, w_q_up, kv_a_g, w_kv_up, q_norm_g, k_norm_g, out_norm_g, w_out, w_router, router_bias, w_exp_gate, w_exp_up, w_exp_down, w_sh_gate, w_sh_up, w_sh_down):
    a = dict(norm_mix_g=norm_mix_g, norm_ffn_g=norm_ffn_g, w_in=w_in, gm_ln_g=gm_ln_g,
             gm_ln_b=gm_ln_b, gm_ws=gm_ws, gm_bs=gm_bs, q_a_g=q_a_g, w_q_up=w_q_up,
             kv_a_g=kv_a_g, w_kv_up=w_kv_up, q_norm_g=q_norm_g, k_norm_g=k_norm_g,
             out_norm_g=out_norm_g, w_out=w_out, w_router=w_router, router_bias=router_bias,
             w_exp_gate=w_exp_gate, w_exp_up=w_exp_up, w_exp_down=w_exp_down,
             w_sh_gate=w_sh_gate, w_sh_up=w_sh_up, w_sh_down=w_sh_down)
    cos_t, sin_t = _rope_tables(positions)
    mod = _ada_mod(c, w_ada, b_ada).reshape(DEPTH, BATCH, 6, D_MODEL)
    xt = x.reshape(N_TOK, D_MODEL)
    for l in range(DEPTH):
        p = _layer_params(l, a)
        gm, q, k, v = _mix_in(xt, mod[l], p, cos_t, sin_t)
        mla = _attention(q, k, v).reshape(N_TOK, MLA_WIDTH)
        x1, h2aug, rkt, rktok, cnt = _mix_out(xt, gm, mla, mod[l], p)
        xt = _moe(h2aug, rkt, rktok, cnt, x1, mod[l], p)
    return xt.reshape(BATCH, SEQ, D_MODEL)
```

```python
import functools

import jax
import jax.numpy as jnp
from jax import lax
from jax.experimental import pallas as pl
from jax.experimental.pallas import tpu as pltpu

D_MODEL = 1024
BATCH = 8
SEQ = 4096
DEPTH = 2
N_TOK = BATCH * SEQ
CHUNK = 64
GM_WIDTH = 512
GM_BLOCK = 128
GM_GROUPS = 4
GM_CH = 128
MLA_HEADS = 8
QK_NOPE = 64
QK_ROPE = 32
QK_DIM = 96
V_DIM = 64
MLA_WIDTH = 512
Q_LORA = 384
KV_LORA = 256
ROPE_THETA = 10000.0
N_EXPERTS = 64
TOP_K = 8
N_GROUPS = 8
GROUP_SIZE = N_EXPERTS // N_GROUPS
TOPK_GROUPS = 4
EXPERT_FF = 256
ROUTED_SCALE = 2.5
EPS = 1e-6

LANES = 128
HEAD_PAD = LANES
IN_COLS_PAD = 2 * GM_WIDTH + Q_LORA + KV_LORA + LANES
O_Q = 2 * GM_WIDTH
O_KV = O_Q + Q_LORA
O_PE = O_KV + KV_LORA

TT = 256
TM = 512
TQ = 512
TK = 256
ATT_HEADS = 8
V_AUG = 80
N_TILES = N_TOK // TT
H_AUG = D_MODEL + LANES
SEG_ALIGN = 16
CHUNK_ROWS = 256
LOCAL_ROWS = TT * TOP_K + N_EXPERTS * SEG_ALIGN
MAX_GROUPS = LOCAL_ROWS // SEG_ALIGN
TM_FFN = 512
FFN_RING = 3
SORTED_ROWS = -(-(N_TOK * TOP_K + N_TILES * N_EXPERTS * (SEG_ALIGN - 1)
                  + N_EXPERTS * (TM_FFN - 1)) // TM_FFN) * TM_FFN
FFN_TILES = SORTED_ROWS // TM_FFN
GROUPS_PER_CHUNK = CHUNK_ROWS // SEG_ALIGN
TRASH_ROWS = 2 * LOCAL_ROWS
STATIC_CHUNKS = 10
VMEM_LIMIT = 56 * 1024 * 1024

F32 = jnp.float32
BF16 = jnp.bfloat16


def _cparams(sem, flags=None):
    return pltpu.CompilerParams(dimension_semantics=sem, vmem_limit_bytes=VMEM_LIMIT, flags=flags)


def _rms_scale(v, width):
    return lax.rsqrt(jnp.sum(v * v, axis=-1, keepdims=True) * (1.0 / width) + EPS)


def _rope_kernel(pos_ref, inv_ref, cos_ref, sin_ref):
    ang = pos_ref[...].astype(F32) * inv_ref[...]
    lane = lax.broadcasted_iota(jnp.int32, ang.shape, 1)
    c = jnp.cos(ang)
    s = jnp.sin(ang)
    cos_ref[...] = jnp.where(lane < QK_NOPE, 1.0, jnp.where(lane < QK_DIM, c, 0.0))
    sin_ref[...] = jnp.where((lane >= QK_NOPE) & (lane < QK_NOPE + 16), -s,
                             jnp.where((lane >= QK_NOPE + 16) & (lane < QK_DIM), s, 0.0))


def _rope_tables(positions):
    inv_freq = 1.0 / (ROPE_THETA ** (jnp.arange(0, QK_ROPE, 2, dtype=F32) / QK_ROPE))
    inv = jnp.zeros((1, LANES), F32)
    inv = inv.at[0, QK_NOPE:QK_NOPE + 16].set(inv_freq)
    inv = inv.at[0, QK_NOPE + 16:QK_DIM].set(inv_freq)
    pos = positions.reshape(N_TOK, 1)
    tile = 2048
    return pl.pallas_call(
        _rope_kernel,
        out_shape=(jax.ShapeDtypeStruct((N_TOK, LANES), F32),
                   jax.ShapeDtypeStruct((N_TOK, LANES), F32)),
        grid=(N_TOK // tile,),
        in_specs=[pl.BlockSpec((tile, 1), lambda i: (i, 0)),
                  pl.BlockSpec((1, LANES), lambda i: (0, 0))],
        out_specs=(pl.BlockSpec((tile, LANES), lambda i: (i, 0)),
                   pl.BlockSpec((tile, LANES), lambda i: (i, 0))),
        compiler_params=_cparams(("arbitrary",)),
        name="rope_tables",
    )(pos, inv)


def _ada_kernel(c_ref, w_ref, b_ref, o_ref):
    c = c_ref[...]
    ca = c * jax.nn.sigmoid(c)
    o_ref[...] = jnp.dot(ca, w_ref[...], precision=lax.Precision.HIGHEST,
                         preferred_element_type=F32) + b_ref[...]


def _ada_mod(c, w_ada, b_ada):
    return pl.pallas_call(
        _ada_kernel,
        out_shape=jax.ShapeDtypeStruct((DEPTH, BATCH, 6 * D_MODEL), F32),
        grid=(DEPTH, 6),
        in_specs=[pl.BlockSpec((BATCH, D_MODEL), lambda l, j: (0, 0)),
                  pl.BlockSpec((None, D_MODEL, D_MODEL), lambda l, j: (l, 0, j)),
                  pl.BlockSpec((None, 1, D_MODEL), lambda l, j: (l, 0, j))],
        out_specs=pl.BlockSpec((None, BATCH, D_MODEL), lambda l, j: (l, 0, j)),
        compiler_params=_cparams(("arbitrary", "arbitrary")),
        name="ada_mod",
    )(c, w_ada, b_ada.reshape(DEPTH, 1, 6 * D_MODEL))


def _rope_rot(t, cos_t, sin_t):
    return t * cos_t + pltpu.roll(t, LANES - 16, 1) * sin_t


def _mix_in_kernel(x_ref, mod_ref, ng_ref, win_ref, lng_ref, lnb_ref, ws_ref, bs_ref,
                   qag_ref, wq_ref, kvg_ref, wkv_ref, wvt_ref, vone_ref, qng_ref, kng_ref, og_ref,
                   cos_ref, sin_ref,
                   gm_ref, q_ref, k_ref, vt_ref):
    x = x_ref[...]
    sh1 = mod_ref[0:1, :]
    sc1 = mod_ref[1:2, :]
    h = x * _rms_scale(x, D_MODEL) * ng_ref[...]
    h = h * (1.0 + sc1) + sh1
    proj = jnp.dot(h.astype(BF16), win_ref[...], preferred_element_type=F32)

    uv_pre = proj[:, :2 * GM_WIDTH]
    uv = 0.5 * uv_pre * (1.0 + lax.erf(uv_pre * (2.0 ** -0.5)))
    row = lax.broadcasted_iota(jnp.int32, (TT, TT), 0)
    col = lax.broadcasted_iota(jnp.int32, (TT, TT), 1)
    causal = (col <= row) & ((col // GM_BLOCK) == (row // GM_BLOCK))
    groups = range(GM_GROUPS)
    gsl = [slice(g * GM_CH, (g + 1) * GM_CH) for g in groups]
    vg = [uv[:, GM_WIDTH + g * GM_CH:GM_WIDTH + (g + 1) * GM_CH] for g in groups]
    mu = [jnp.mean(vg[g], axis=-1, keepdims=True) for g in groups]
    xc = [vg[g] - mu[g] for g in groups]
    var = [jnp.mean(xc[g] * xc[g], axis=-1, keepdims=True) for g in groups]
    vln = [(xc[g] * lax.rsqrt(var[g] + EPS) * lng_ref[:, gsl[g]] + lnb_ref[:, gsl[g]]).astype(BF16)
           for g in groups]
    ws = [jnp.where(causal, ws_ref[g], jnp.zeros((), BF16)) for g in groups]
    gate = [jnp.concatenate([jnp.dot(ws[g], vln[g][s * TT:(s + 1) * TT], preferred_element_type=F32)
                             for s in range(TM // TT)], axis=0) for g in groups]
    outs = [uv[:, gsl[g]] * (gate[g] + bs_ref[:, gsl[g]]) for g in groups]
    sq = [jnp.sum(outs[g] * outs[g], axis=-1, keepdims=True) for g in groups]
    rs = lax.rsqrt((sq[0] + sq[1] + sq[2] + sq[3]) * (1.0 / GM_WIDTH) + EPS)
    for g in groups:
        gm_ref[:, gsl[g]] = (outs[g] * rs * og_ref[:, gsl[g]]).astype(BF16)

    cos_t = cos_ref[...]
    sin_t = sin_ref[...]
    lane = lax.broadcasted_iota(jnp.int32, (TM, LANES), 1)
    ql = proj[:, O_Q:O_KV]
    qn = ql * _rms_scale(ql, Q_LORA) * qag_ref[...]
    qa = jnp.dot(qn.astype(BF16), wq_ref[...], preferred_element_type=F32)
    kl = proj[:, O_KV:O_PE]
    kn = kl * _rms_scale(kl, KV_LORA) * kvg_ref[...]
    kva = jnp.dot(kn.astype(BF16), wkv_ref[...], preferred_element_type=F32)
    pe = proj[:, O_PE:O_PE + LANES]
    real = lane < QK_DIM
    pe_ssq = jnp.sum(jnp.where(real, pe * pe, 0.0), axis=-1, keepdims=True)
    pe_rot = _rope_rot(pe * kng_ref[...], cos_t, sin_t)
    qscale = (QK_DIM ** -0.5) * 1.4426950408889634
    v_t = lax.dot_general(wvt_ref[...], kn.astype(BF16), (((1,), (1,)), ((), ())),
                          preferred_element_type=F32) + vone_ref[...]
    vt_ref[...] = v_t.reshape(MLA_HEADS, V_AUG, TM).astype(BF16)
    heads = range(MLA_HEADS)
    qh = [qa[:, hd * HEAD_PAD:(hd + 1) * HEAD_PAD] for hd in heads]
    kh = [kva[:, hd * HEAD_PAD:(hd + 1) * HEAD_PAD] for hd in heads]
    q_ssq = [jnp.sum(jnp.where(real, qh[hd] * qh[hd], 0.0), axis=-1, keepdims=True) for hd in heads]
    k_ssq = [jnp.sum(kh[hd] * kh[hd], axis=-1, keepdims=True) for hd in heads]
    qg = [qh[hd] * lax.rsqrt(q_ssq[hd] * (1.0 / QK_DIM) + EPS) * qng_ref[...] for hd in heads]
    q_sw = [pltpu.roll(qg[hd], LANES - 16, 1) for hd in heads]
    for hd in heads:
        q_ref[hd] = ((qg[hd] * cos_t + q_sw[hd] * sin_t) * qscale).astype(BF16)
    for hd in heads:
        krs = lax.rsqrt((k_ssq[hd] + pe_ssq) * (1.0 / QK_DIM) + EPS)
        k_ref[hd] = ((kh[hd] * kng_ref[...] + pe_rot) * krs).astype(BF16)


def _mix_in(x, mod_l, p, cos_t, sin_t):
    tiles_per_seq = SEQ // TM
    full = lambda shape: pl.BlockSpec(shape, lambda i: (0,) * len(shape))
    head_spec = pl.BlockSpec((None, MLA_HEADS, TM, HEAD_PAD),
                             lambda i: (i // tiles_per_seq, 0, i % tiles_per_seq, 0))
    head_shape = jax.ShapeDtypeStruct((BATCH, MLA_HEADS, SEQ, HEAD_PAD), BF16)
    return pl.pallas_call(
        _mix_in_kernel,
        out_shape=(jax.ShapeDtypeStruct((N_TOK, GM_WIDTH), BF16), head_shape, head_shape,
                   jax.ShapeDtypeStruct((BATCH, MLA_HEADS, V_AUG, SEQ), BF16)),
        grid=(N_TOK // TM,),
        in_specs=[pl.BlockSpec((TM, D_MODEL), lambda i: (i, 0)),
                  pl.BlockSpec((None, 6, D_MODEL), lambda i: (i // tiles_per_seq, 0, 0)),
                  full((1, D_MODEL)),
                  full((D_MODEL, IN_COLS_PAD)),
                  full((1, GM_WIDTH)), full((1, GM_WIDTH)),
                  full((GM_GROUPS, TT, TT)), full((TM, GM_WIDTH)),
                  full((1, Q_LORA)), full((Q_LORA, MLA_HEADS * HEAD_PAD)),
                  full((1, KV_LORA)), full((KV_LORA, MLA_HEADS * HEAD_PAD)),
                  full((MLA_HEADS * V_AUG, KV_LORA)), full((MLA_HEADS * V_AUG, 1)),
                  full((1, LANES)), full((1, LANES)), full((1, GM_WIDTH)),
                  pl.BlockSpec((TM, LANES), lambda i: (i, 0)),
                  pl.BlockSpec((TM, LANES), lambda i: (i, 0))],
        out_specs=(pl.BlockSpec((TM, GM_WIDTH), lambda i: (i, 0)), head_spec, head_spec,
                   pl.BlockSpec((None, MLA_HEADS, V_AUG, TM),
                                lambda i: (i // tiles_per_seq, 0, 0, i % tiles_per_seq))),
        compiler_params=_cparams(("arbitrary",)),
        name="mix_in",
    )(x, mod_l, p["norm_mix_g"], p["w_in"], p["gm_ln_g"], p["gm_ln_b"], p["gm_ws"], p["gm_bs"],
      p["q_a_g"], p["w_q_up"], p["kv_a_g"], p["w_k_up"], p["w_v_t"], p["v_ones"], p["q_norm_g"],
      p["k_norm_g"],
      p["out_norm_gm"], cos_t, sin_t)


def _attn_kernel(q_ref, k_ref, vt_ref, o_ref):
    qi = pl.program_id(2)
    krow = lax.broadcasted_iota(jnp.int32, (TK, TQ), 0)
    qcol = lax.broadcasted_iota(jnp.int32, (TK, TQ), 1)
    nt = (((1,), (1,)), ((), ()))
    per_q = TQ // TK

    def tile(j, carry, diag):
        start = pl.multiple_of(j * TK, TK)
        heads = range(ATT_HEADS)
        s = [lax.dot_general(k_ref[hh, pl.ds(start, TK), :], q_ref[hh], nt,
                             preferred_element_type=F32) for hh in heads]
        if diag is not None:
            allowed = ((krow + diag * TK) // CHUNK) <= (qcol // CHUNK)
            s = [jnp.where(allowed, sh, -jnp.inf) for sh in s]
        m_new = [jnp.maximum(carry[hh][0], jnp.max(s[hh], axis=0, keepdims=True)) for hh in heads]
        p = [jnp.exp2(s[hh] - m_new[hh]) for hh in heads]
        alpha = [jnp.exp2(carry[hh][0] - m_new[hh]) for hh in heads]
        pv = [jnp.dot(vt_ref[hh, :, pl.ds(start, TK)], p[hh].astype(BF16),
                      preferred_element_type=F32) for hh in heads]
        return tuple((m_new[hh], alpha[hh] * carry[hh][1] + pv[hh]) for hh in heads)

    init = tuple((jnp.full((1, TQ), -jnp.inf, F32), jnp.zeros((V_AUG, TQ), F32))
                 for _ in range(ATT_HEADS))
    carry = lax.fori_loop(0, qi * per_q, functools.partial(tile, diag=None), init)
    for t in range(per_q):
        carry = tile(qi * per_q + t, carry, t)
    out_t = jnp.concatenate([acc[:V_DIM] / acc[V_DIM:V_DIM + 1] for (_, acc) in carry], axis=0)
    o_ref[...] = out_t


def _attention(q, k, vt):
    groups = MLA_HEADS // ATT_HEADS
    qspec = pl.BlockSpec((None, ATT_HEADS, TQ, HEAD_PAD), lambda b, hp, i: (b, hp, i, 0))
    kspec = pl.BlockSpec((None, ATT_HEADS, SEQ, HEAD_PAD), lambda b, hp, i: (b, hp, 0, 0))
    vspec = pl.BlockSpec((None, ATT_HEADS, V_AUG, SEQ), lambda b, hp, i: (b, hp, 0, 0))
    return pl.pallas_call(
        _attn_kernel,
        out_shape=jax.ShapeDtypeStruct((BATCH, MLA_WIDTH, SEQ), F32),
        grid=(BATCH, groups, SEQ // TQ),
        in_specs=[qspec, kspec, vspec],
        out_specs=pl.BlockSpec((None, ATT_HEADS * V_DIM, TQ), lambda b, hp, i: (b, hp, i)),
        compiler_params=_cparams(("arbitrary", "arbitrary", "arbitrary")),
        name="mla_attention",
    )(q, k, vt)


def _route(logits_t, bias_col):
    t = logits_t.shape[1]
    scores = jax.nn.sigmoid(logits_t)
    biased = scores + bias_col
    neg = -jnp.inf
    b3 = biased.reshape(N_GROUPS, GROUP_SIZE, t)
    m1 = jnp.max(b3, axis=1, keepdims=True)
    n_max = jnp.sum((b3 == m1).astype(F32), axis=1, keepdims=True)
    m2 = jnp.max(jnp.where(b3 < m1, b3, neg), axis=1, keepdims=True)
    gs = (m1 + jnp.where(n_max >= 2.0, m1, m2)).reshape(N_GROUPS, t)
    gidx = lax.broadcasted_iota(jnp.int32, (N_GROUPS, t), 0)
    grank = jnp.zeros((N_GROUPS, t), F32)
    for g in range(N_GROUPS):
        other = gs[g:g + 1, :]
        ahead = (other > gs) | ((other == gs) & (gidx > g))
        grank = grank + ahead.astype(F32)
    gsel = grank < float(TOPK_GROUPS)
    emask = jnp.broadcast_to(gsel.reshape(N_GROUPS, 1, t), (N_GROUPS, GROUP_SIZE, t)).reshape(N_EXPERTS, t)
    ms = jnp.where(emask, biased, neg)
    eidx = lax.broadcasted_iota(jnp.int32, (N_EXPERTS, t), 0).astype(F32)
    picked = jnp.zeros((N_EXPERTS, t), F32)
    for _ in range(TOP_K):
        top = jnp.max(ms, axis=0, keepdims=True)
        first = jnp.min(jnp.where(ms == top, eidx, float(N_EXPERTS)), axis=0, keepdims=True)
        hit = eidx == first
        picked = jnp.where(hit, 1.0, picked)
        ms = jnp.where(hit, neg, ms)
    sel = picked > 0.5
    w = jnp.where(sel, scores, 0.0)
    denom = jnp.sum(w, axis=0, keepdims=True)
    return (w / denom) * ROUTED_SCALE, sel


def _mix_out_kernel(x_ref, gm_ref, mla_ref, mod_ref, og_ref, wout_ref, fg_ref, wr_ref, rb_ref,
                    x1_ref, h2_ref, rkt_ref, rktok_ref, cnt_ref):
    mla_t = mla_ref[...]
    rs_t = lax.rsqrt(jnp.sum(mla_t * mla_t, axis=0, keepdims=True) * (1.0 / MLA_WIDTH) + EPS)
    mla_nt = (mla_t * rs_t * og_ref[...]).astype(BF16)
    y = (jnp.dot(gm_ref[...], wout_ref[:GM_WIDTH, :], preferred_element_type=F32)
         + lax.dot_general(mla_nt, wout_ref[GM_WIDTH:, :], (((0,), (0,)), ((), ())),
                           preferred_element_type=F32))
    g1 = mod_ref[2:3, :]
    sh2 = mod_ref[3:4, :]
    sc2 = mod_ref[4:5, :]
    x1 = x_ref[...] + g1 * y
    x1_ref[...] = x1
    h2 = x1 * _rms_scale(x1, D_MODEL) * fg_ref[...]
    h2 = h2 * (1.0 + sc2) + sh2
    nt = (((1,), (1,)), ((), ()))
    h_hi = h2.astype(BF16)
    h_lo = (h2 - h_hi.astype(F32)).astype(BF16)
    wr = wr_ref[...]
    w_hi = wr.astype(BF16)
    w_lo = (wr - w_hi.astype(F32)).astype(BF16)
    logits_t = (lax.dot_general(w_hi, h_hi, nt, preferred_element_type=F32)
                + lax.dot_general(w_hi, h_lo, nt, preferred_element_type=F32)
                + lax.dot_general(w_lo, h_hi, nt, preferred_element_type=F32))
    comb, sel = _route(logits_t, rb_ref[...])
    before = (lax.broadcasted_iota(jnp.int32, (TT, TT), 0)
              < lax.broadcasted_iota(jnp.int32, (TT, TT), 1))
    self = jnp.where(sel, 1.0, 0.0)
    before_b = jnp.where(before, 1.0, 0.0).astype(BF16)
    rank = jnp.concatenate(
        [jnp.dot(self[:, s * TT:(s + 1) * TT].astype(BF16), before_b, preferred_element_type=F32)
         for s in range(TM // TT)], axis=1)
    rk = jnp.where(sel, rank, -1.0)
    rkt_ref[...] = rk.astype(BF16)
    rktok_ref[...] = rk.T.astype(BF16)
    for s in range(TM // TT):
        cnt_ref[s] = jnp.broadcast_to(
            jnp.sum(self[:, s * TT:(s + 1) * TT], axis=1, keepdims=True), (N_EXPERTS, LANES))
    comb_tok = comb.T
    hi = comb_tok.astype(BF16)
    lo = (comb_tok - hi.astype(F32)).astype(BF16)
    h2_ref[:, :D_MODEL] = h2.astype(BF16)
    h2_ref[:, D_MODEL:] = jnp.concatenate([hi, lo], axis=-1)


def _mix_out(x, gm, mla, mod_l, p):
    tiles_per_seq = SEQ // TM
    full = lambda shape: pl.BlockSpec(shape, lambda i: (0,) * len(shape))
    tok = lambda width: pl.BlockSpec((TM, width), lambda i: (i, 0))
    return pl.pallas_call(
        _mix_out_kernel,
        out_shape=(jax.ShapeDtypeStruct((N_TOK, D_MODEL), F32),
                   jax.ShapeDtypeStruct((N_TOK, H_AUG), BF16),
                   jax.ShapeDtypeStruct((N_EXPERTS, N_TOK), BF16),
                   jax.ShapeDtypeStruct((N_TOK, N_EXPERTS), BF16),
                   jax.ShapeDtypeStruct((N_TILES, N_EXPERTS, LANES), F32)),
        grid=(N_TOK // TM,),
        in_specs=[tok(D_MODEL), tok(GM_WIDTH),
                  pl.BlockSpec((None, MLA_WIDTH, TM), lambda i: (i // tiles_per_seq, 0, i % tiles_per_seq)),
                  pl.BlockSpec((None, 6, D_MODEL), lambda i: (i // tiles_per_seq, 0, 0)),
                  full((MLA_WIDTH, 1)), full((D_MODEL, D_MODEL)), full((1, D_MODEL)),
                  full((N_EXPERTS, D_MODEL)), full((N_EXPERTS, 1))],
        out_specs=(tok(D_MODEL), tok(H_AUG), pl.BlockSpec((N_EXPERTS, TM), lambda i: (0, i)),
                   tok(N_EXPERTS), pl.BlockSpec((TM // TT, N_EXPERTS, LANES), lambda i: (i, 0, 0))),
        compiler_params=_cparams(("arbitrary",)),
        name="mix_out_route",
    )(x, gm, mla, mod_l, p["out_norm_mla"], p["w_out"], p["norm_ffn_g"], p["w_router_t"],
      p["router_bias"])


def _moe_plan(cnt):
    i32 = jnp.int32
    cnt = cnt.astype(i32)
    padlen = (cnt + SEG_ALIGN - 1) // SEG_ALIGN * SEG_ALIGN
    lend = jnp.cumsum(padlen, axis=1)
    lstart = lend - padlen
    ltot = lend[:, -1]
    region = jnp.sum(padlen, axis=0)
    region_pad = (region + TM_FFN - 1) // TM_FFN * TM_FFN
    base = jnp.cumsum(region_pad) - region_pad
    pos = base[None, :] + jnp.cumsum(padlen, axis=0) - padlen
    gidx = jnp.arange(MAX_GROUPS, dtype=i32)
    grow = gidx * SEG_ALIGN
    in_seg = (grow[None, :, None] >= lstart[:, None, :]) & (grow[None, :, None] < lend[:, None, :])
    shift = jnp.sum(jnp.where(in_seg, (pos - lstart)[:, None, :], 0), axis=-1)
    n_groups = ltot // SEG_ALIGN
    valid_g = gidx[None, :] < n_groups[:, None]
    real_dst = (shift + grow[None, :]) // SEG_ALIGN
    trash_base = SORTED_ROWS // SEG_ALIGN
    trash = trash_base + (jnp.arange(N_TILES, dtype=i32)[:, None] % 2) * MAX_GROUPS + gidx[None, :]
    n_chunks = (n_groups + GROUPS_PER_CHUNK - 1) // GROUPS_PER_CHUNK
    n_move = jnp.maximum(n_chunks, STATIC_CHUNKS)
    group_dst = jnp.where(valid_g, real_dst, trash)
    group_src = jnp.where(valid_g, real_dst, 0)
    lead = (trash_base + MAX_GROUPS + gidx)[None, :]
    static_c = jnp.full((1,), STATIC_CHUNKS, i32)
    tile_end = jnp.cumsum(region_pad) // TM_FFN
    n_used = tile_end[-1]
    tj = jnp.arange(FFN_TILES, dtype=i32)
    in_exp = (tj[:, None] * TM_FFN >= base[None, :]) & (tj[:, None] * TM_FFN < (base + region_pad)[None, :])
    t_exp = jnp.sum(jnp.where(in_exp, jnp.arange(N_EXPERTS, dtype=i32)[None, :], 0), axis=-1)
    t_valid = jnp.sum(jnp.where(in_exp, jnp.clip((base + region)[None, :] - tj[:, None] * TM_FFN, 0, TM_FFN), 0),
                      axis=-1)
    has_rows = region_pad > 0
    run_of_exp = jnp.cumsum(has_rows.astype(i32)) - 1
    eidx = jnp.arange(N_EXPERTS, dtype=i32)
    t_run = jnp.sum(jnp.where(in_exp, run_of_exp[None, :], 0), axis=-1)
    run_exp = jnp.sum(jnp.where(has_rows[None, :] & (run_of_exp[None, :] == eidx[:, None]), eidx[None, :], 0),
                      axis=-1)
    lstart_f = lstart.astype(F32)
    lend_f = lend.astype(F32)
    return {
        "t_run": t_run.astype(i32),
        "run_exp": run_exp.astype(i32),
        "n_runs": jnp.sum(has_rows.astype(i32)).reshape(1),
        "move_out": jnp.concatenate([static_c, n_move]).astype(i32),
        "move_in": jnp.concatenate([n_move, static_c]).astype(i32),
        "group_dst": jnp.concatenate([lead, group_dst], axis=0).reshape(-1).astype(i32),
        "group_src": jnp.concatenate([group_src, jnp.zeros((1, MAX_GROUPS), i32)], axis=0
                                     ).reshape(-1).astype(i32),
        "seg_row": jnp.stack([lstart_f, lend_f], axis=1),
        "seg_col": jnp.stack([lstart_f, lend_f], axis=2),
        "n_used": n_used.reshape(1).astype(i32),
        "t_exp": t_exp.astype(i32),
        "t_valid": t_valid.astype(i32),
    }


def _dispatch_kernel(nm_ref, gd_ref, h_ref, rkt_ref, srow_ref, scol_ref, xs_ref, loc_ref, sem):
    i = pl.program_id(0)
    slot = i % 2
    pslot = 1 - slot

    def wait_chunks(n, slot_):
        def body(c, carry):
            pltpu.make_async_copy(loc_ref.at[slot_, pl.ds(0, GROUPS_PER_CHUNK)],
                                  xs_ref.at[pl.ds(0, GROUPS_PER_CHUNK)], sem.at[slot_]).wait()
            return carry
        lax.fori_loop(0, n, body, 0)

    def issue_chunk(entry, slot_, c):
        for k in range(GROUPS_PER_CHUNK):
            g = c * GROUPS_PER_CHUNK + k
            pltpu.make_async_copy(loc_ref.at[slot_, g], xs_ref.at[gd_ref[entry * MAX_GROUPS + g]],
                                  sem.at[slot_]).start()

    @pl.when(i == 0)
    def _():
        loc_ref[1] = jnp.zeros(loc_ref.shape[1:], BF16)

    ls_row = srow_ref[0:1, :]
    le_row = srow_ref[1:2, :]
    ls_rep = jnp.broadcast_to(scol_ref[:, 0:1], (N_EXPERTS, LANES)).astype(BF16)
    rk_ls = jnp.concatenate([rkt_ref[...], ls_rep], axis=1)
    h = h_ref[...]
    n_c = nm_ref[i + 1]
    rio_e0 = lax.broadcasted_iota(jnp.int32, (CHUNK_ROWS, N_EXPERTS), 0).astype(F32)
    rio_t0 = lax.broadcasted_iota(jnp.int32, (CHUNK_ROWS, TT), 0).astype(F32)

    def select(r0f):
        rio_e = rio_e0 + r0f
        g = jnp.where((rio_e >= ls_row) & (rio_e < le_row), 1.0, 0.0).astype(BF16)
        both = jnp.dot(g, rk_ls, preferred_element_type=F32)
        want = (rio_t0 + r0f) - jnp.concatenate([both[:, TT:]] * (TT // LANES), axis=1)
        return jnp.where(both[:, :TT] == want, 1.0, 0.0).astype(BF16)

    def emit(g0, psel):
        loc_ref[slot, pl.ds(g0, GROUPS_PER_CHUNK)] = jnp.dot(
            psel, h, preferred_element_type=F32).astype(BF16).reshape(GROUPS_PER_CHUNK, SEG_ALIGN, H_AUG)

    psels = []
    for c in range(STATIC_CHUNKS):
        psels.append(select(float(c * CHUNK_ROWS)))
        issue_chunk(i, pslot, c)

    def issue_more(c, carry):
        issue_chunk(i, pslot, c)
        return carry

    lax.fori_loop(STATIC_CHUNKS, nm_ref[i], issue_more, 0)

    @pl.when(i >= 1)
    def _():
        wait_chunks(nm_ref[i - 1], slot)

    for c in range(STATIC_CHUNKS):
        emit(c * GROUPS_PER_CHUNK, psels[c])

    def chunk(c, carry):
        emit(pl.multiple_of(c * GROUPS_PER_CHUNK, GROUPS_PER_CHUNK), select((c * CHUNK_ROWS).astype(F32)))
        return carry

    lax.fori_loop(STATIC_CHUNKS, n_c, chunk, 0)

    @pl.when(i == N_TILES - 1)
    def _():
        def issue_last(c, carry):
            issue_chunk(i + 1, slot, c)
            return carry

        lax.fori_loop(0, n_c, issue_last, 0)
        wait_chunks(nm_ref[i], pslot)
        wait_chunks(n_c, slot)


def _dispatch(plan, h2aug, rkt):
    grid_spec = pltpu.PrefetchScalarGridSpec(
        num_scalar_prefetch=2,
        grid=(N_TILES,),
        in_specs=[pl.BlockSpec((TT, H_AUG), lambda i, *_: (i, 0)),
                  pl.BlockSpec((N_EXPERTS, TT), lambda i, *_: (0, i)),
                  pl.BlockSpec((None, 2, N_EXPERTS), lambda i, *_: (i, 0, 0)),
                  pl.BlockSpec((None, N_EXPERTS, 2), lambda i, *_: (i, 0, 0))],
        out_specs=pl.BlockSpec(memory_space=pl.ANY),
        scratch_shapes=[pltpu.VMEM((2, MAX_GROUPS, SEG_ALIGN, H_AUG), BF16),
                        pltpu.SemaphoreType.DMA((2,))],
    )
    return pl.pallas_call(
        _dispatch_kernel,
        out_shape=jax.ShapeDtypeStruct(((SORTED_ROWS + TRASH_ROWS) // SEG_ALIGN, SEG_ALIGN, H_AUG), BF16),
        grid_spec=grid_spec,
        compiler_params=_cparams(("arbitrary",)),
        name="moe_dispatch",
    )(plan["move_out"], plan["group_dst"], h2aug, rkt, plan["seg_row"], plan["seg_col"])


def _ffn_kernel(nu_ref, te_ref, tv_ref, tr_ref, re_ref, nr_ref,
                xs_ref, wg_hbm, wu_hbm, wd_hbm, y_hbm,
                xbuf_ref, ybuf_ref, wg_raw, wu_raw, wd_raw, wgu_ref, wd_ref, xsem, ysem, wsem,
                *, layer):
    n_used = nu_ref[0]
    n_runs = nr_ref[0]

    tile_groups = TM_FFN // SEG_ALIGN

    def groups_of(t):
        return pl.ds(pl.multiple_of(t * tile_groups, tile_groups), tile_groups)

    def x_copy(t):
        slot = t % FFN_RING
        return pltpu.make_async_copy(xs_ref.at[groups_of(t)], xbuf_ref.at[slot], xsem.at[slot])

    def y_copy(t):
        slot = t % 2
        return pltpu.make_async_copy(ybuf_ref.at[slot], y_hbm.at[groups_of(t)], ysem.at[slot])

    def w_copies(r):
        slot = r % 2
        e = re_ref[r]
        return (pltpu.make_async_copy(wg_hbm.at[layer, e], wg_raw.at[slot], wsem.at[slot]),
                pltpu.make_async_copy(wu_hbm.at[layer, e], wu_raw.at[slot], wsem.at[slot]),
                pltpu.make_async_copy(wd_hbm.at[layer, e], wd_raw.at[slot], wsem.at[slot]))

    x_copy(0).start()

    @pl.when(n_used > 1)
    def _():
        x_copy(1).start()

    for cp in w_copies(0):
        cp.start()

    def tile(j, carry):
        e = te_ref[j]
        r = tr_ref[j]

        @pl.when(j + 2 < n_used)
        def _():
            x_copy(j + 2).start()

        @pl.when((j == 0) | (r != tr_ref[jnp.maximum(j - 1, 0)]))
        def _():
            for cp in w_copies(r):
                cp.wait()
            slot = r % 2
            wgu_ref[:, :EXPERT_FF] = wg_raw[slot].astype(BF16)
            wgu_ref[:, EXPERT_FF:] = wu_raw[slot].astype(BF16)
            wd_ref[...] = wd_raw[slot].astype(BF16)

            @pl.when(r + 1 < n_runs)
            def _():
                for cp in w_copies(r + 1):
                    cp.start()

        x_copy(j).wait()

        @pl.when(j >= 2)
        def _():
            y_copy(j - 2).wait()

        xt = xbuf_ref[j % FFN_RING].reshape(TM_FFN, H_AUG)
        rows = lax.broadcasted_iota(jnp.int32, (TM_FFN, 1), 0)
        valid = rows < tv_ref[j]
        x = jnp.where(valid, xt[:, :D_MODEL], jnp.zeros((), BF16))
        lane = lax.broadcasted_iota(jnp.int32, (TM_FFN, LANES), 1)
        mine = ((lane == e) | (lane == e + N_EXPERTS)) & valid
        w = jnp.sum(jnp.where(mine, xt[:, D_MODEL:].astype(F32), 0.0), axis=-1, keepdims=True)
        hgu = jnp.dot(x, wgu_ref[...], preferred_element_type=F32)
        g = hgu[:, :EXPERT_FF]
        a = (g * jax.nn.sigmoid(g)) * hgu[:, EXPERT_FF:] * w
        ybuf_ref[j % 2] = jnp.dot(a.astype(BF16), wd_ref[...], preferred_element_type=F32).astype(
            BF16).reshape(tile_groups, SEG_ALIGN, D_MODEL)
        y_copy(j).start()
        return carry

    lax.fori_loop(0, n_used, tile, 0)

    @pl.when(n_used >= 2)
    def _():
        y_copy(n_used - 2).wait()

    y_copy(n_used - 1).wait()


def _expert_ffn(plan, xs, p):
    any_spec = pl.BlockSpec(memory_space=pl.ANY)
    grid_spec = pltpu.PrefetchScalarGridSpec(
        num_scalar_prefetch=6,
        grid=(1,),
        in_specs=[any_spec, any_spec, any_spec, any_spec],
        out_specs=any_spec,
        scratch_shapes=[pltpu.VMEM((FFN_RING, TM_FFN // SEG_ALIGN, SEG_ALIGN, H_AUG), BF16),
                        pltpu.VMEM((2, TM_FFN // SEG_ALIGN, SEG_ALIGN, D_MODEL), BF16),
                        pltpu.VMEM((2, D_MODEL, EXPERT_FF), F32),
                        pltpu.VMEM((2, D_MODEL, EXPERT_FF), F32),
                        pltpu.VMEM((2, EXPERT_FF, D_MODEL), F32),
                        pltpu.VMEM((D_MODEL, 2 * EXPERT_FF), BF16),
                        pltpu.VMEM((EXPERT_FF, D_MODEL), BF16),
                        pltpu.SemaphoreType.DMA((FFN_RING,)),
                        pltpu.SemaphoreType.DMA((2,)),
                        pltpu.SemaphoreType.DMA((2,))],
    )
    return pl.pallas_call(
        functools.partial(_ffn_kernel, layer=p["layer"]),
        out_shape=jax.ShapeDtypeStruct((SORTED_ROWS // SEG_ALIGN, SEG_ALIGN, D_MODEL), BF16),
        grid_spec=grid_spec,
        compiler_params=_cparams(("arbitrary",)),
        name="moe_expert_ffn",
    )(plan["n_used"], plan["t_exp"], plan["t_valid"], plan["t_run"], plan["run_exp"], plan["n_runs"],
      xs, p["w_exp_gate"], p["w_exp_up"], p["w_exp_down"])


def _combine_kernel(nc_ref, gs_ref, y_ref, rktok_ref, srow_ref, scol_ref, h_ref, x1_ref, mod_ref,
                    sgu_ref, sd_ref, o_ref, yl_ref, acc_ref, sem):
    i = pl.program_id(0)
    slot = i % 2
    pslot = 1 - slot

    def gather_chunk(tile, slot_, c):
        for k in range(GROUPS_PER_CHUNK):
            g = c * GROUPS_PER_CHUNK + k
            pltpu.make_async_copy(y_ref.at[gs_ref[tile * MAX_GROUPS + g]], yl_ref.at[slot_, g],
                                  sem.at[slot_]).start()

    def wait_chunks(n, slot_):
        def body(c, carry):
            pltpu.make_async_copy(y_ref.at[pl.ds(0, GROUPS_PER_CHUNK)],
                                  yl_ref.at[slot_, pl.ds(0, GROUPS_PER_CHUNK)], sem.at[slot_]).wait()
            return carry
        lax.fori_loop(0, n, body, 0)

    @pl.when(i == 0)
    def _():
        yl_ref[...] = jnp.zeros(yl_ref.shape, BF16)

        def first(c, carry):
            gather_chunk(0, 0, c)
            return carry

        lax.fori_loop(0, nc_ref[0], first, 0)

    h = h_ref[...]
    hs = jnp.dot(h, sgu_ref[...], preferred_element_type=F32)
    gs = hs[:, :EXPERT_FF]
    a = (gs * jax.nn.sigmoid(gs)) * hs[:, EXPERT_FF:]
    shared = jnp.dot(a.astype(BF16), sd_ref[...], preferred_element_type=F32)

    n_c = nc_ref[i]
    ls_col = scol_ref[:, 0:1]
    le_col = scol_ref[:, 1:2]
    ls_row8 = jnp.broadcast_to(srow_ref[0:1, :], (8, N_EXPERTS)).astype(BF16)
    rk_ls = jnp.concatenate([rktok_ref[...], ls_row8], axis=0)
    rio_e0 = lax.broadcasted_iota(jnp.int32, (N_EXPERTS, CHUNK_ROWS), 1).astype(F32)
    rio_r0 = lax.broadcasted_iota(jnp.int32, (1, CHUNK_ROWS), 1).astype(F32)

    def select_t(r0f):
        rio_e = rio_e0 + r0f
        gt = jnp.where((rio_e >= ls_col) & (rio_e < le_col), 1.0, 0.0).astype(BF16)
        both = jnp.dot(rk_ls, gt, preferred_element_type=F32)
        want = (rio_r0 + r0f) - both[TT:TT + 1, :]
        return jnp.where(both[:TT, :] == want, 1.0, 0.0).astype(BF16)

    static_rows = STATIC_CHUNKS * CHUNK_ROWS
    pts = []
    for c in range(STATIC_CHUNKS):
        pts.append(select_t(float(c * CHUNK_ROWS)))
        gather_chunk(i + 1, pslot, c)
    pt_all = jnp.concatenate(pts, axis=1)

    def gather_more(c, carry):
        gather_chunk(i + 1, pslot, c)
        return carry

    lax.fori_loop(STATIC_CHUNKS, nc_ref[i + 1], gather_more, 0)
    wait_chunks(n_c, slot)
    y_static = yl_ref[slot, pl.ds(0, STATIC_CHUNKS * GROUPS_PER_CHUNK)].reshape(static_rows, D_MODEL)
    acc_ref[...] = shared + jnp.dot(pt_all, y_static, preferred_element_type=F32)

    def chunk(c, carry):
        g0 = pl.multiple_of(c * GROUPS_PER_CHUNK, GROUPS_PER_CHUNK)
        y_c = yl_ref[slot, pl.ds(g0, GROUPS_PER_CHUNK)].reshape(CHUNK_ROWS, D_MODEL)
        acc_ref[...] += jnp.dot(select_t((c * CHUNK_ROWS).astype(F32)), y_c, preferred_element_type=F32)
        return carry

    lax.fori_loop(STATIC_CHUNKS, n_c, chunk, 0)
    o_ref[...] = x1_ref[...] + mod_ref[5:6, :] * acc_ref[...]

    @pl.when(i == N_TILES - 1)
    def _():
        wait_chunks(nc_ref[i + 1], pslot)


def _combine(plan, y, rktok, h2aug, x1, mod_l, p):
    tiles_per_seq = SEQ // TT
    grid_spec = pltpu.PrefetchScalarGridSpec(
        num_scalar_prefetch=2,
        grid=(N_TILES,),
        in_specs=[pl.BlockSpec(memory_space=pl.ANY),
                  pl.BlockSpec((TT, N_EXPERTS), lambda i, *_: (i, 0)),
                  pl.BlockSpec((None, 2, N_EXPERTS), lambda i, *_: (i, 0, 0)),
                  pl.BlockSpec((None, N_EXPERTS, 2), lambda i, *_: (i, 0, 0)),
                  pl.BlockSpec((TT, D_MODEL), lambda i, *_: (i, 0)),
                  pl.BlockSpec((TT, D_MODEL), lambda i, *_: (i, 0)),
                  pl.BlockSpec((None, 6, D_MODEL), lambda i, *_: (i // tiles_per_seq, 0, 0)),
                  pl.BlockSpec((D_MODEL, 2 * EXPERT_FF), lambda i, *_: (0, 0)),
                  pl.BlockSpec((EXPERT_FF, D_MODEL), lambda i, *_: (0, 0))],
        out_specs=pl.BlockSpec((TT, D_MODEL), lambda i, *_: (i, 0)),
        scratch_shapes=[pltpu.VMEM((2, MAX_GROUPS, SEG_ALIGN, D_MODEL), BF16),
                        pltpu.VMEM((TT, D_MODEL), F32),
                        pltpu.SemaphoreType.DMA((2,))],
    )
    return pl.pallas_call(
        _combine_kernel,
        out_shape=jax.ShapeDtypeStruct((N_TOK, D_MODEL), F32),
        grid_spec=grid_spec,
        compiler_params=_cparams(("arbitrary",)),
        name="moe_combine",
    )(plan["move_in"], plan["group_src"], y, rktok, plan["seg_row"], plan["seg_col"], h2aug, x1,
      mod_l, p["w_sh_gu"], p["w_sh_down"])


def _moe(h2aug, rkt, rktok, cnt, x1, mod_l, p):
    plan = _moe_plan(cnt[:, :, 0])
    xs = _dispatch(plan, h2aug, rkt)
    y = _expert_ffn(plan, xs, p)
    return _combine(plan, y, rktok, h2aug, x1, mod_l, p)


def _pad_heads(w, per_head):
    k = w.shape[0]
    w = w.reshape(k, MLA_HEADS, per_head)
    w = jnp.pad(w, ((0, 0), (0, 0), (0, HEAD_PAD - per_head)))
    return w.reshape(k, MLA_HEADS * HEAD_PAD)


def _layer_params(l, a):
    w_in = a["w_in"][l]
    half = QK_ROPE // 2
    pe_cols = jnp.zeros((D_MODEL, LANES), F32).at[:, QK_NOPE:QK_DIM].set(w_in[:, O_PE:])
    pe_cols = pe_cols.at[:, QK_DIM:QK_DIM + half].set(w_in[:, O_PE:O_PE + half])
    w_in_p = jnp.concatenate([w_in[:, :O_PE], pe_cols], axis=1).astype(BF16)
    w_q = a["w_q_up"][l].reshape(Q_LORA, MLA_HEADS, QK_DIM)
    w_q = jnp.concatenate([w_q, w_q[:, :, QK_NOPE:QK_NOPE + half],
                           jnp.zeros((Q_LORA, MLA_HEADS, HEAD_PAD - QK_DIM - half), F32)], axis=-1)
    w_kv = a["w_kv_up"][l].reshape(KV_LORA, MLA_HEADS, QK_NOPE + V_DIM)
    w_k = _pad_heads(w_kv[:, :, :QK_NOPE].reshape(KV_LORA, MLA_HEADS * QK_NOPE), QK_NOPE)
    w_v = w_kv[:, :, QK_NOPE:].reshape(KV_LORA, MLA_HEADS * V_DIM)
    blocks = TT // GM_BLOCK
    eye = jnp.eye(blocks, dtype=F32)
    ws_big = jnp.einsum("ab,gts->gatbs", eye, a["gm_ws"][l]).reshape(GM_GROUPS, TT, TT)
    bs_tile = jnp.tile(jnp.repeat(a["gm_bs"][l].T, GM_CH, axis=1), (TM // GM_BLOCK, 1))
    pad96 = lambda g: jnp.concatenate(
        [g, g[QK_NOPE:QK_NOPE + half], jnp.zeros((LANES - QK_DIM - half,), F32)]).reshape(1, LANES)
    return {
        "norm_mix_g": a["norm_mix_g"][l].reshape(1, D_MODEL),
        "norm_ffn_g": a["norm_ffn_g"][l].reshape(1, D_MODEL),
        "w_in": w_in_p,
        "gm_ln_g": a["gm_ln_g"][l].reshape(1, GM_WIDTH),
        "gm_ln_b": a["gm_ln_b"][l].reshape(1, GM_WIDTH),
        "gm_ws": ws_big.astype(BF16),
        "gm_bs": bs_tile,
        "q_a_g": a["q_a_g"][l].reshape(1, Q_LORA),
        "w_q_up": w_q.reshape(Q_LORA, MLA_HEADS * HEAD_PAD).astype(BF16),
        "kv_a_g": a["kv_a_g"][l].reshape(1, KV_LORA),
        "w_k_up": w_k.astype(BF16),
        "w_v_t": jnp.pad(w_v.T.reshape(MLA_HEADS, V_DIM, KV_LORA), ((0, 0), (0, V_AUG - V_DIM), (0, 0))
                         ).reshape(MLA_HEADS * V_AUG, KV_LORA).astype(BF16),
        "v_ones": jnp.tile((jnp.arange(V_AUG) == V_DIM).astype(F32), MLA_HEADS).reshape(-1, 1),
        "q_norm_g": pad96(a["q_norm_g"][l]),
        "k_norm_g": pad96(a["k_norm_g"][l]),
        "out_norm_gm": a["out_norm_g"][l, :GM_WIDTH].reshape(1, GM_WIDTH),
        "out_norm_mla": a["out_norm_g"][l, GM_WIDTH:].reshape(MLA_WIDTH, 1),
        "w_out": a["w_out"][l].astype(BF16),
        "w_router_t": a["w_router"][l].T,
        "router_bias": a["router_bias"][l].reshape(N_EXPERTS, 1),
        "layer": l,
        "w_exp_gate": a["w_exp_gate"],
        "w_exp_up": a["w_exp_up"],
        "w_exp_down": a["w_exp_down"],
        "w_sh_gu": jnp.concatenate([a["w_sh_gate"][l], a["w_sh_up"][l]], axis=-1).astype(BF16),
        "w_sh_down": a["w_sh_down"][l].astype(BF16),
    }


def kernel(x, c, positions, norm_mix_g, norm_ffn_g, w_ada, b_ada, w_in, gm_ln_g, gm_ln_b, gm_ws, gm_bs, q_a_g, w_q_up, kv_a_g, w_kv_up, q_norm_g, k_norm_g, out_norm_g, w_out, w_router, router_bias, w_exp_gate, w_exp_up, w_exp_down, w_sh_gate, w_sh_up, w_sh_down):
    a = dict(norm_mix_g=norm_mix_g, norm_ffn_g=norm_ffn_g, w_in=w_in, gm_ln_g=gm_ln_g,
             gm_ln_b=gm_ln_b, gm_ws=gm_ws, gm_bs=gm_bs, q_a_g=q_a_g, w_q_up=w_q_up,
             kv_a_g=kv_a_g, w_kv_up=w_kv_up, q_norm_g=q_norm_g, k_norm_g=k_norm_g,
             out_norm_g=out_norm_g, w_out=w_out, w_router=w_router, router_bias=router_bias,
             w_exp_gate=w_exp_gate, w_exp_up=w_exp_up, w_exp_down=w_exp_down,
             w_sh_gate=w_sh_gate, w_sh_up=w_sh_up, w_sh_down=w_sh_down)
    cos_t, sin_t = _rope_tables(positions)
    mod = _ada_mod(c, w_ada, b_ada).reshape(DEPTH, BATCH, 6, D_MODEL)
    xt = x.reshape(N_TOK, D_MODEL)
    for l in range(DEPTH):
        p = _layer_params(l, a)
        gm, q, k, v = _mix_in(xt, mod[l], p, cos_t, sin_t)
        mla = _attention(q, k, v)
        x1, h2aug, rkt, rktok, cnt = _mix_out(xt, gm, mla, mod[l], p)
        xt = _moe(h2aug, rkt, rktok, cnt, x1, mod[l], p)
    return xt.reshape(BATCH, SEQ, D_MODEL)
```
